```python
import math
import jax
import jax.numpy as jnp
from jax import lax
import numpy as np

D_MODEL = 1024
BATCH = 8
SEQ = 2048
DEPTH = 4
DEC_BATCH = 32
DEC_SEQ = 8
PAST_LEN = 8192
PAGE_SIZE = 128

N_HEADS = 12
HEAD_DIM = 64
ATTN_DIM = N_HEADS * HEAD_DIM
D_FF = 2816
RMS_EPS = 1e-6
N_MIXERS = 2
N_NSA_LAYERS = (DEPTH + 1) // 2
N_DIL_LAYERS = DEPTH // 2
Q_BLOCK = 128

NSA_KV_GROUPS = 4
NSA_HPG = N_HEADS // NSA_KV_GROUPS
CMP_BLOCK = 32
CMP_STRIDE = 16
CMP_HIDDEN = 256
SEL_BLOCK = 64
SEL_TOPK = 16
CMP_PER_SEL = SEL_BLOCK // CMP_STRIDE
CMP_IN_SEL = (SEL_BLOCK - CMP_BLOCK) // CMP_STRIDE + 1
WIN_A = 512
FORCE_SCORE = 1e4
SLC_Q_BLOCK = 32
NSA_KV_DIM = 6 * NSA_KV_GROUPS * HEAD_DIM
NSA_IN_DIM = ATTN_DIM + NSA_KV_DIM + 3 * N_HEADS

DIL_PAIRS = ((128, 1), (512, 4), (2048, 16))
DIL_HPG = N_HEADS // len(DIL_PAIRS)
DIL_IN_DIM = 3 * ATTN_DIM

REL_BUCKETS = 32
REL_EXACT = 16
REL_MAX_DIST = 2048

kernel_name = 'hybrid_nsa_dilated_macaron_step'


def rms_norm(x, g):
    x32 = x.astype(jnp.float32)
    y = x32 * lax.rsqrt(jnp.mean(x32 * x32, axis=-1, keepdims=True) + RMS_EPS)
    return (y * g.astype(jnp.float32)).astype(x.dtype)


def swiglu(x, w_in, w_out):
    gate, up = jnp.split(x @ w_in, 2, axis=-1)
    return (jax.nn.silu(gate) * up) @ w_out


def half_ffn(x, g_pre, g_post, w_in, w_out):
    return x + 0.5 * rms_norm(swiglu(rms_norm(x, g_pre), w_in, w_out), g_post)


def rel_bucket(dist):
    dist = jnp.maximum(dist, 0)
    d32 = jnp.maximum(dist, 1).astype(jnp.float32)
    large = REL_EXACT + (jnp.log(d32 / REL_EXACT) / math.log(REL_MAX_DIST / REL_EXACT)
                         * (REL_BUCKETS - REL_EXACT)).astype(jnp.int32)
    large = jnp.minimum(large, REL_BUCKETS - 1)
    return jnp.where(dist < REL_EXACT, dist, large)


def masked_softmax(logits, mask):
    l = jnp.where(mask, logits.astype(jnp.float32), -1e30)
    m = jnp.max(l, axis=-1, keepdims=True)
    e = jnp.where(mask, jnp.exp(l - m), 0.0)
    s = jnp.maximum(jnp.sum(e, axis=-1, keepdims=True), 1e-30)
    return e / s, (m + jnp.log(s))[..., 0]


def nsa_compress(k, pos, w1, w2):
    b, l = k.shape[:2]
    n_c = (l - CMP_BLOCK) // CMP_STRIDE + 1
    idx = CMP_STRIDE * jnp.arange(n_c)[:, None] + jnp.arange(CMP_BLOCK)[None, :]
    blk = k[:, idx] + pos[None, None, :, None, :]
    flat = jnp.moveaxis(blk, 3, 2).reshape(b, n_c, NSA_KV_GROUPS, CMP_BLOCK * HEAD_DIM)
    return jax.nn.silu(flat @ w1) @ w2


def nsa_project(h, w_in):
    b, t = h.shape[:2]
    z = h @ w_in
    q = z[..., :ATTN_DIM].reshape(b, t, N_HEADS, HEAD_DIM)
    kv = z[..., ATTN_DIM:ATTN_DIM + NSA_KV_DIM].reshape(b, t, 6, NSA_KV_GROUPS, HEAD_DIM)
    gates = z[..., ATTN_DIM + NSA_KV_DIM:].reshape(b, t, 3, N_HEADS)
    return q, kv, gates


def nsa_attend(q, gates, kv, kwin, q_pos0, w_pos0, cmp_pos, cmp_w1, cmp_w2, rel_bias):
    G, R = NSA_KV_GROUPS, NSA_HPG
    b, tq = q.shape[:2]
    l = kv.shape[1]
    qg = (q * (HEAD_DIM ** -0.5)).reshape(b, tq, G, R, HEAD_DIM)
    q_pos = q_pos0 + jnp.arange(tq)
    tbl = rel_bias.reshape(REL_BUCKETS, G, R)

    kc = nsa_compress(kv[:, :, 0], cmp_pos[0], cmp_w1[0], cmp_w2[0])
    vc = nsa_compress(kv[:, :, 1], cmp_pos[1], cmp_w1[1], cmp_w2[1])
    n_c = kc.shape[1]
    c_end = CMP_STRIDE * jnp.arange(n_c) + CMP_BLOCK - 1
    p_c, _ = masked_softmax(jnp.einsum('btgrd,bcgd->bgrtc', qg, kc), c_end[None, :] <= q_pos[:, None])
    o_cmp = jnp.einsum('bgrtc,bcgd->btgrd', p_c.astype(vc.dtype), vc)

    n_sel = -(-l // SEL_BLOCK)
    imp = jnp.pad(jnp.sum(p_c, axis=2), ((0, 0), (0, 0), (0, 0), (0, CMP_PER_SEL * n_sel - n_c)))
    imp = imp.reshape(b, G, tq, n_sel, CMP_PER_SEL)[..., :CMP_IN_SEL].sum(-1)
    blk = jnp.arange(n_sel)[None, :]
    cur = (q_pos // SEL_BLOCK)[:, None]
    forced = (blk == 0) | (blk == cur) | (blk == cur - 1)
    imp = jnp.where(forced, FORCE_SCORE, jnp.where(blk > cur, -FORCE_SCORE, imp))
    k_sel = min(SEL_TOPK, n_sel)
    _, sel_idx = lax.top_k(imp, k_sel)

    pad = n_sel * SEL_BLOCK - l

    def to_blocks(x):
        x = jnp.pad(x, ((0, 0), (0, pad), (0, 0), (0, 0)))
        return jnp.moveaxis(x.reshape(b, n_sel, SEL_BLOCK, G, HEAD_DIM), 3, 1)

    kb = to_blocks(kv[:, :, 2])
    vb = to_blocks(kv[:, :, 3])
    gather = jax.vmap(jax.vmap(lambda src, ix: src[ix]))
    tbl_g = jnp.moveaxis(tbl, 1, 0)
    g_ar = jnp.arange(G)[None, :, None, None]
    qbs = math.gcd(tq, SLC_Q_BLOCK)
    n_keys = k_sel * SEL_BLOCK

    def slc_block(i0):
        qblk = lax.dynamic_slice_in_dim(qg, i0, qbs, axis=1)
        ix = lax.dynamic_slice_in_dim(sel_idx, i0, qbs, axis=2)
        pq = q_pos0 + i0 + jnp.arange(qbs)
        ks = gather(kb, ix).reshape(b, G, qbs, n_keys, HEAD_DIM)
        vs = gather(vb, ix).reshape(b, G, qbs, n_keys, HEAD_DIM)
        kpos = (ix[..., None] * SEL_BLOCK + jnp.arange(SEL_BLOCK)).reshape(b, G, qbs, n_keys)
        dist = pq[None, None, :, None] - kpos
        bias = jnp.moveaxis(tbl_g[g_ar, rel_bucket(dist)], 4, 3)
        logits = jnp.einsum('bqgrd,bgqsd->bgqrs', qblk, ks) + bias
        p, _ = masked_softmax(logits, (dist >= 0)[:, :, :, None, :])
        return jnp.einsum('bgqrs,bgqsd->bqgrd', p.astype(vs.dtype), vs)

    o_slc = lax.map(slc_block, jnp.arange(tq // qbs) * qbs)
    o_slc = jnp.moveaxis(o_slc, 0, 1).reshape(b, tq, G, R, HEAD_DIM)

    kwp = jnp.pad(kwin, ((0, 0), (WIN_A, 0), (0, 0), (0, 0), (0, 0)))
    qbw = math.gcd(tq, Q_BLOCK)
    span = WIN_A - 1 + qbw

    def win_block(i0):
        qblk = lax.dynamic_slice_in_dim(qg, i0, qbw, axis=1)
        pq = q_pos0 + i0 + jnp.arange(qbw)
        kvs = lax.dynamic_slice_in_dim(kwp, q_pos0 + i0 - w_pos0 + 1, span, axis=1)
        kpos = q_pos0 + i0 - WIN_A + 1 + jnp.arange(span)
        dist = pq[:, None] - kpos[None, :]
        mask = (kpos[None, :] >= w_pos0) & (dist >= 0) & (dist < WIN_A)
        bias = jnp.transpose(tbl[rel_bucket(dist)], (2, 3, 0, 1))
        logits = jnp.einsum('bqgrd,bsgd->bgrqs', qblk, kvs[:, :, 0]) + bias
        p, _ = masked_softmax(logits, mask)
        return jnp.einsum('bgrqs,bsgd->bqgrd', p.astype(kvs.dtype), kvs[:, :, 1])

    o_win = lax.map(win_block, jnp.arange(tq // qbw) * qbw)
    o_win = jnp.moveaxis(o_win, 0, 1).reshape(b, tq, G, R, HEAD_DIM)

    g = jax.nn.sigmoid(gates.astype(jnp.float32)).astype(q.dtype).reshape(b, tq, 3, G, R, 1)
    o = g[:, :, 0] * o_cmp + g[:, :, 1] * o_slc + g[:, :, 2] * o_win
    return o.reshape(b, tq, ATTN_DIM)


def nsa_prompt(h, w_in, w_out, cmp_pos, cmp_w1, cmp_w2, rel_bias):
    q, kv, gates = nsa_project(h, w_in)
    t = h.shape[1]
    o = nsa_attend(q, gates, kv[:, :, :4], kv[:, :, 4:], 0, 0, cmp_pos, cmp_w1, cmp_w2, rel_bias)
    return o @ w_out, kv[:, :, :4], kv[:, t - min(WIN_A, t):, 4:]


def nsa_sample(h, cache_pool, page_table, win_buf, w_in, w_out, cmp_pos, cmp_w1, cmp_w2, rel_bias):
    q, kv, gates = nsa_project(h, w_in)
    db = h.shape[0]
    past = cache_pool[page_table]
    past = past.reshape(db, past.shape[1] * past.shape[2], 4, NSA_KV_GROUPS, HEAD_DIM)
    p_len = past.shape[1]
    kv_all = jnp.concatenate([past, kv[:, :, :4]], axis=1)
    w_all = jnp.concatenate([win_buf, kv[:, :, 4:]], axis=1)
    o = nsa_attend(q, gates, kv_all, w_all, p_len, p_len - win_buf.shape[1],
                   cmp_pos, cmp_w1, cmp_w2, rel_bias)
    lw = w_all.shape[1]
    return o @ w_out, kv[:, :, :4], w_all[:, lw - min(WIN_A, lw):]


def dil_project(h, w_in):
    b, t = h.shape[:2]
    z = (h @ w_in).reshape(b, t, 3, N_HEADS, HEAD_DIM)
    return z[:, :, 0] * (HEAD_DIM ** -0.5), z[:, :, 1], z[:, :, 2]


def dil_group(q, k, v, q_pos0, k_pos0, window, dil, tbl):
    b, tq = q.shape[:2]
    offs = jnp.arange(window // dil + 1) * dil
    bias = tbl[rel_bucket(offs)].T
    qb = math.gcd(tq, Q_BLOCK)

    def block(i0):
        qblk = lax.dynamic_slice_in_dim(q, i0, qb, axis=1)
        qi = q_pos0 - k_pos0 + i0 + jnp.arange(qb)
        idx = qi[:, None] - offs[None, :]
        valid = idx >= 0
        ix = jnp.maximum(idx, 0)
        ks = k[:, ix]
        vs = v[:, ix]
        logits = jnp.einsum('bqhd,bqjhd->bhqj', qblk, ks) + bias[None, :, None, :]
        p, lse = masked_softmax(logits, valid)
        o = jnp.einsum('bhqj,bqjhd->bqhd', p.astype(vs.dtype), vs)
        return o, jnp.moveaxis(lse, 1, 2)

    o, lse = lax.map(block, jnp.arange(tq // qb) * qb)
    hg = q.shape[2]
    o = jnp.moveaxis(o, 0, 1).reshape(b, tq, hg, HEAD_DIM)
    lse = jnp.moveaxis(lse, 0, 1).reshape(b, tq, hg)
    return o, lse


def dil_combine(outs, lses, w_out):
    alpha = jax.nn.softmax(jnp.stack(lses, axis=0), axis=0)
    y = jnp.concatenate([o * alpha[g][..., None].astype(o.dtype) for g, o in enumerate(outs)], axis=2)
    b, t = y.shape[:2]
    return y.reshape(b, t, ATTN_DIM) @ w_out


def dil_prompt(h, w_in, w_out, rel_bias):
    q, k, v = dil_project(h, w_in)
    t = h.shape[1]
    outs, lses, states = [], [], []
    for g, (win, dil) in enumerate(DIL_PAIRS):
        hs = slice(g * DIL_HPG, (g + 1) * DIL_HPG)
        o, lse = dil_group(q[:, :, hs], k[:, :, hs], v[:, :, hs], 0, 0, win, dil, rel_bias[:, hs])
        outs.append(o)
        lses.append(lse)
        keep = min(win, t)
        states.append(jnp.stack([k[:, t - keep:, hs], v[:, t - keep:, hs]], axis=2))
    return dil_combine(outs, lses, w_out), states


def dil_sample(h, bufs, past_len, w_in, w_out, rel_bias):
    q, k, v = dil_project(h, w_in)
    outs, lses, states = [], [], []
    for g, (win, dil) in enumerate(DIL_PAIRS):
        hs = slice(g * DIL_HPG, (g + 1) * DIL_HPG)
        kv_all = jnp.concatenate([bufs[g], jnp.stack([k[:, :, hs], v[:, :, hs]], axis=2)], axis=1)
        lk = kv_all.shape[1]
        o, lse = dil_group(q[:, :, hs], kv_all[:, :, 0], kv_all[:, :, 1], past_len,
                           past_len - bufs[g].shape[1], win, dil, rel_bias[:, hs])
        outs.append(o)
        lses.append(lse)
        states.append(kv_all[:, lk - min(win, lk):])
    return dil_combine(outs, lses, w_out), states


def setup_inputs(seed: int = 0) -> dict:
    key = jax.random.key(seed)
    ks = jax.random.split(key, 20)
    n_pages = PAST_LEN // PAGE_SIZE
    n_used = DEC_BATCH * n_pages
    n_pool = n_used + max(1, n_used // 4)
    page_table = jax.random.permutation(ks[3], n_pool)[:n_used].reshape(DEC_BATCH, n_pages).astype(jnp.int32)
    nrm = jax.random.normal
    G = NSA_KV_GROUPS
    return {
        'x_prompt': nrm(ks[0], (BATCH, SEQ, D_MODEL), jnp.float32),
        'x_sample': nrm(ks[1], (DEC_BATCH, DEC_SEQ, D_MODEL), jnp.float32),
        'cache_nsa_kv': nrm(ks[2], (N_NSA_LAYERS, n_pool, PAGE_SIZE, 4, G, HEAD_DIM), jnp.float32),
        'page_table': page_table,
        'state_nsa_win': nrm(ks[4], (N_NSA_LAYERS, DEC_BATCH, min(WIN_A, PAST_LEN), 2, G, HEAD_DIM), jnp.float32),
        'state_dil_w128': nrm(ks[5], (N_DIL_LAYERS, DEC_BATCH, min(128, PAST_LEN), 2, DIL_HPG, HEAD_DIM), jnp.float32),
        'state_dil_w512': nrm(ks[6], (N_DIL_LAYERS, DEC_BATCH, min(512, PAST_LEN), 2, DIL_HPG, HEAD_DIM), jnp.float32),
        'state_dil_w2048': nrm(ks[7], (N_DIL_LAYERS, DEC_BATCH, min(2048, PAST_LEN), 2, DIL_HPG, HEAD_DIM), jnp.float32),
        'norm_g': 1.0 + 0.05 * nrm(ks[8], (DEPTH, 6, D_MODEL), jnp.float32),
        'ffn_w_in': nrm(ks[9], (DEPTH, 2, D_MODEL, 2 * D_FF), jnp.float32) * D_MODEL ** -0.5,
        'ffn_w_out': nrm(ks[10], (DEPTH, 2, D_FF, D_MODEL), jnp.float32) * D_FF ** -0.5,
        'nsa_w_in': nrm(ks[11], (N_NSA_LAYERS, D_MODEL, NSA_IN_DIM), jnp.float32) * D_MODEL ** -0.5,
        'nsa_w_out': nrm(ks[12], (N_NSA_LAYERS, ATTN_DIM, D_MODEL), jnp.float32) * ATTN_DIM ** -0.5,
        'nsa_cmp_pos': 0.2 * nrm(ks[13], (N_NSA_LAYERS, 2, CMP_BLOCK, HEAD_DIM), jnp.float32),
        'nsa_cmp_w1': nrm(ks[14], (N_NSA_LAYERS, 2, CMP_BLOCK * HEAD_DIM, CMP_HIDDEN), jnp.float32) * (CMP_BLOCK * HEAD_DIM) ** -0.5,
        'nsa_cmp_w2': nrm(ks[15], (N_NSA_LAYERS, 2, CMP_HIDDEN, HEAD_DIM), jnp.float32) * CMP_HIDDEN ** -0.5,
        'dil_w_in': nrm(ks[16], (N_DIL_LAYERS, D_MODEL, DIL_IN_DIM), jnp.float32) * D_MODEL ** -0.5,
        'dil_w_out': nrm(ks[17], (N_DIL_LAYERS, ATTN_DIM, D_MODEL), jnp.float32) * ATTN_DIM ** -0.5,
        'rel_bias': 0.5 * nrm(ks[18], (REL_BUCKETS, N_HEADS), jnp.float32),
    }


def reference(x_prompt, x_sample, cache_nsa_kv, page_table, state_nsa_win, state_dil_w128,
              state_dil_w512, state_dil_w2048, norm_g, ffn_w_in, ffn_w_out, nsa_w_in, nsa_w_out,
              nsa_cmp_pos, nsa_cmp_w1, nsa_cmp_w2, dil_w_in, dil_w_out, rel_bias):
    yp, ys = x_prompt, x_sample
    nsa_kv_p, nsa_win_p, nsa_kv_s, nsa_win_s = [], [], [], []
    dil_p = [[], [], []]
    dil_s = [[], [], []]
    for i in range(DEPTH):
        li = i // N_MIXERS
        g = norm_g[i]
        yp = half_ffn(yp, g[0], g[1], ffn_w_in[i, 0], ffn_w_out[i, 0])
        ys = half_ffn(ys, g[0], g[1], ffn_w_in[i, 0], ffn_w_out[i, 0])
        hp = rms_norm(yp, g[2])
        hsm = rms_norm(ys, g[2])
        if i % N_MIXERS == 0:
            mp, kvp, winp = nsa_prompt(hp, nsa_w_in[li], nsa_w_out[li], nsa_cmp_pos[li],
                                       nsa_cmp_w1[li], nsa_cmp_w2[li], rel_bias)
            ms, kvs, wins = nsa_sample(hsm, cache_nsa_kv[li], page_table, state_nsa_win[li],
                                       nsa_w_in[li], nsa_w_out[li], nsa_cmp_pos[li],
                                       nsa_cmp_w1[li], nsa_cmp_w2[li], rel_bias)
            nsa_kv_p.append(kvp)
            nsa_win_p.append(winp)
            nsa_kv_s.append(kvs)
            nsa_win_s.append(wins)
        else:
            mp, stp = dil_prompt(hp, dil_w_in[li], dil_w_out[li], rel_bias)
            bufs = (state_dil_w128[li], state_dil_w512[li], state_dil_w2048[li])
            ms, sts = dil_sample(hsm, bufs, PAST_LEN, dil_w_in[li], dil_w_out[li], rel_bias)
            for gi in range(len(DIL_PAIRS)):
                dil_p[gi].append(stp[gi])
                dil_s[gi].append(sts[gi])
        yp = yp + rms_norm(mp, g[3])
        ys = ys + rms_norm(ms, g[3])
        yp = half_ffn(yp, g[4], g[5], ffn_w_in[i, 1], ffn_w_out[i, 1])
        ys = half_ffn(ys, g[4], g[5], ffn_w_in[i, 1], ffn_w_out[i, 1])
    return (yp, ys,
            jnp.stack(nsa_kv_p), jnp.stack(nsa_win_p),
            jnp.stack(dil_p[0]), jnp.stack(dil_p[1]), jnp.stack(dil_p[2]),
            jnp.stack(nsa_kv_s), jnp.stack(nsa_win_s),
            jnp.stack(dil_s[0]), jnp.stack(dil_s[1]), jnp.stack(dil_s[2]))
```

```python
import functools
import math

import jax
import jax.numpy as jnp
from jax import lax
from jax.experimental import pallas as pl
from jax.experimental.pallas import tpu as pltpu

f32, bf16 = jnp.float32, jnp.bfloat16
SDS = jax.ShapeDtypeStruct

D = 1024
FF = 2816
FF_CHUNK = 256
H, DH = 12, 64
G, R = 4, 3
ATT = H * DH
EPS = 1e-6
NEG = -1e30
FORCE = 1e4
SEL_BLOCK, TOPK = 64, 16
WIN = 512
NSA_PAD = 2432
TQ = 256
PAGE = 128
CHUNK_PAGES = 8
DIL = ((128, 1), (512, 4), (2048, 16))
DT = 128
MIB = 1024 * 1024


def _cp(sem, vmem_mib):
    return pltpu.CompilerParams(dimension_semantics=sem, vmem_limit_bytes=vmem_mib * MIB)


def _rms(x, g):
    return x * lax.rsqrt(jnp.mean(x * x, axis=-1, keepdims=True) + EPS) * g


def _nt(a, b):
    return lax.dot_general(a, b, (((1,), (1,)), ((), ())), preferred_element_type=f32)


def _dot(a, b):
    return jnp.dot(a, b, preferred_element_type=f32)


def _resident(shape):
    nd = len(shape)
    return pl.BlockSpec(shape, lambda *_: (0,) * nd, pipeline_mode=pl.Buffered(1))


def _ffn_body(x_ref, g_ref, win_ref, wout_ref, o_ref):
    x = x_ref[...]
    h = _rms(x, g_ref[0:1, :]).astype(bf16)
    acc = jnp.zeros(x.shape, f32)
    for c in range(FF // FF_CHUNK):
        lo, hi = c * FF_CHUNK, (c + 1) * FF_CHUNK
        gate = _dot(h, win_ref[:, lo:hi])
        up = _dot(h, win_ref[:, FF + lo:FF + hi])
        act = (gate * jax.nn.sigmoid(gate) * up).astype(bf16)
        acc = acc + _dot(act, wout_ref[lo:hi, :])
    o_ref[...] = x + 0.5 * _rms(acc, g_ref[1:2, :])


def _half_ffn(x, g2, w_in, w_out):
    m = x.shape[0]
    tm = min(m, 512)
    return pl.pallas_call(
        _ffn_body, out_shape=SDS((m, D), f32), grid=(m // tm,),
        in_specs=[pl.BlockSpec((tm, D), lambda i: (i, 0)), _resident((2, D)),
                  _resident((D, 2 * FF)), _resident((FF, D))],
        out_specs=pl.BlockSpec((tm, D), lambda i: (i, 0)),
        compiler_params=_cp(("parallel",), 48), name="half_ffn")(x, g2, w_in, w_out)


def _proj_body(x_ref, g_ref, w_ref, o_ref):
    h = _rms(x_ref[...], g_ref[...]).astype(bf16)
    o_ref[...] = _dot(h, w_ref[...])


def _proj(x, g, w):
    m, n = x.shape[0], w.shape[1]
    return pl.pallas_call(
        _proj_body, out_shape=SDS((m, n), f32), grid=(1,),
        in_specs=[_resident((m, D)), _resident((1, D)), _resident((D, n))],
        out_specs=pl.BlockSpec((m, n), lambda i: (0, 0)),
        compiler_params=_cp(("arbitrary",), 32), name="proj_sample")(x, g, w)


def _nsa_proj_body(x_ref, g_ref, w_ref, q_ref, kv_ref, kwin_ref, ks_ref, vs_ref, kw_ref, vw_ref, gt_ref):
    h = _rms(x_ref[...], g_ref[...]).astype(bf16)
    zq = _dot(h, w_ref[:, 0:ATT]) * (DH ** -0.5)
    for hh in range(H):
        q_ref[0, hh] = zq[:, hh * DH:(hh + 1) * DH].astype(bf16)
    zc = _dot(h, w_ref[:, ATT:ATT + 512])
    kv_ref[:, 0:512] = zc
    zs = _dot(h, w_ref[:, ATT + 512:ATT + 1024])
    kv_ref[:, 512:1024] = zs
    zw = _dot(h, w_ref[:, ATT + 1024:ATT + 1536])
    kwin_ref[...] = zw
    for g in range(G):
        ks_ref[0, g] = zs[:, g * DH:(g + 1) * DH].astype(bf16)
        vs_ref[0, g] = zs[:, 256 + g * DH:256 + (g + 1) * DH].astype(bf16)
        kw_ref[0, g] = zw[:, g * DH:(g + 1) * DH].astype(bf16)
        vw_ref[0, g] = zw[:, 256 + g * DH:256 + (g + 1) * DH].astype(bf16)
    gt_ref[...] = _dot(h, w_ref[:, ATT + 1536:NSA_PAD])


def _nsa_proj_prompt(x, g, w, b, s):
    tm = 512
    nt = s // tm
    hm = lambda n: pl.BlockSpec((1, n, tm, DH), lambda i: (i // nt, 0, i % nt, 0))
    rows = lambda n: pl.BlockSpec((tm, n), lambda i: (i, 0))
    return pl.pallas_call(
        _nsa_proj_body,
        out_shape=(SDS((b, H, s, DH), bf16), SDS((b * s, 1024), f32), SDS((b * s, 512), f32),
                   SDS((b, G, s, DH), bf16), SDS((b, G, s, DH), bf16), SDS((b, G, s, DH), bf16),
                   SDS((b, G, s, DH), bf16), SDS((b * s, 128), f32)),
        grid=(b * s // tm,),
        in_specs=[rows(D), _resident((1, D)), _resident((D, NSA_PAD))],
        out_specs=(hm(H), rows(1024), rows(512), hm(G), hm(G), hm(G), hm(G), rows(128)),
        compiler_params=_cp(("parallel",), 48), name="nsa_proj_prompt")(x, g, w)


def _dil_proj_body(x_ref, g_ref, w_ref, o_ref):
    h = _rms(x_ref[...], g_ref[...]).astype(bf16)
    o_ref[:, 0:ATT] = _dot(h, w_ref[:, 0:ATT]) * (DH ** -0.5)
    o_ref[:, ATT:2 * ATT] = _dot(h, w_ref[:, ATT:2 * ATT])
    o_ref[:, 2 * ATT:3 * ATT] = _dot(h, w_ref[:, 2 * ATT:3 * ATT])


def _dil_proj(x, g, w):
    m = x.shape[0]
    tm = min(m, 512)
    return pl.pallas_call(
        _dil_proj_body, out_shape=SDS((m, 3 * ATT), f32), grid=(m // tm,),
        in_specs=[pl.BlockSpec((tm, D), lambda i: (i, 0)), _resident((1, D)), _resident((D, 3 * ATT))],
        out_specs=pl.BlockSpec((tm, 3 * ATT), lambda i: (i, 0)),
        compiler_params=_cp(("parallel",), 48), name="dil_proj")(x, g, w)


def _out_proj_body(x_ref, g_ref, w_ref, *refs):
    o_ref = refs[-1]
    o = refs[0][...]
    for r in refs[1:-1]:
        o = o + r[...]
    y = _dot(o.astype(bf16), w_ref[...])
    o_ref[...] = x_ref[...] + _rms(y, g_ref[...])


def _out_proj(x, g, w, *branches):
    m = x.shape[0]
    tm = min(m, 512)
    row = lambda n: pl.BlockSpec((tm, n), lambda i: (i, 0))
    return pl.pallas_call(
        _out_proj_body, out_shape=SDS((m, D), f32), grid=(m // tm,),
        in_specs=[row(D), _resident((1, D)), _resident((ATT, D))] + [row(ATT)] * len(branches),
        out_specs=row(D), compiler_params=_cp(("parallel",), 32), name="out_proj")(x, g, w, *branches)


def _compress_body(*refs, n_pages, rows, paged):
    if paged:
        refs = refs[1:]
    pages = refs[:n_pages]
    w1_ref, pos_ref, w2_ref, kc_ref, vc_ref, cs_ref, lhs_ref, aprev_ref = refs[n_pages:]
    nch = n_pages * rows // 16
    first = (pl.program_id(1) == 0) if paged else True

    for p in range(n_pages):
        for c in range(4):
            cs_ref[c, p * rows:(p + 1) * rows, :] = pages[p][0, :, c * 128:(c + 1) * 128]

    lane = lax.broadcasted_iota(jnp.int32, (nch, 128), 1)
    low = lane < DH
    for kind in range(2):
        for pair in range(2):
            c = kind * 2 + pair
            for r2 in range(8):
                va = cs_ref[c, pl.ds(2 * r2, nch, stride=16), :]
                vb = cs_ref[c, pl.ds(2 * r2 + 1, nch, stride=16), :]
                g_lo = jnp.where(low, va, pltpu.roll(vb, DH, 1))
                g_hi = jnp.where(low, pltpu.roll(va, DH, 1), vb)
                lhs_ref[kind, (2 * pair) * nch:(2 * pair + 1) * nch, r2 * 128:(r2 + 1) * 128] = g_lo.astype(bf16)
                lhs_ref[kind, (2 * pair + 1) * nch:(2 * pair + 2) * nch, r2 * 128:(r2 + 1) * 128] = g_hi.astype(bf16)

    if paged:
        @pl.when(first)
        def _():
            aprev_ref[...] = jnp.zeros(aprev_ref.shape, f32)

    row = lax.broadcasted_iota(jnp.int32, (nch, 256), 0)
    row_o = lax.broadcasted_iota(jnp.int32, (nch, DH), 0)
    for kind, out_ref in ((0, kc_ref), (1, vc_ref)):
        pp = _dot(lhs_ref[kind], w1_ref[kind])
        pos = _dot(pos_ref[kind].astype(bf16), w1_ref[kind])
        posterm = pos[0:1, 0:256] + pos[1:2, 256:512]
        for g in range(G):
            a = pp[g * nch:(g + 1) * nch, 0:256]
            bm = pp[g * nch:(g + 1) * nch, 256:512]
            shifted = pltpu.roll(a, 1, 0)
            if paged:
                carry = aprev_ref[kind, (g + 1) * nch - 1:(g + 1) * nch, :]
                shifted = jnp.where(row == 0, carry, shifted)
                aprev_ref[kind, g * nch:(g + 1) * nch, :] = a
            pre = shifted + bm + posterm
            hid = (pre * jax.nn.sigmoid(pre)).astype(bf16)
            out = _dot(hid, w2_ref[kind])
            out = jnp.where((row_o == 0) & first, 0.0, out)
            out_ref[0, g] = out.astype(bf16)


def _compress_prompt(kv4, w1cat, pos2, w2, b, s):
    nch = s // 16
    out = SDS((b, G, nch, DH), bf16)
    ospec = pl.BlockSpec((1, G, nch, DH), lambda i: (i, 0, 0, 0))
    return pl.pallas_call(
        functools.partial(_compress_body, n_pages=1, rows=s, paged=False),
        out_shape=(out, out), grid=(b,),
        in_specs=[pl.BlockSpec((1, s, 512), lambda i: (i, 0, 0)),
                  _resident((2, 1024, 512)), _resident((2, 2, 1024)), _resident((2, 256, DH))],
        out_specs=(ospec, ospec),
        scratch_shapes=[pltpu.VMEM((4, s, 128), f32), pltpu.VMEM((2, G * nch, 1024), bf16),
                        pltpu.VMEM((2, G * nch, 256), f32)],
        compiler_params=_cp(("parallel",), 48), name="compress_prompt")(kv4.reshape(b, s, 1024), w1cat, pos2, w2)


def _compress_sample(pool, page_table, w1cat, pos2, w2):
    db, n_pages = page_table.shape
    steps = n_pages // CHUNK_PAGES
    nch = CHUNK_PAGES * PAGE // 16
    out = SDS((db, G, n_pages * PAGE // 16, DH), bf16)
    ospec = pl.BlockSpec((1, G, nch, DH), lambda i, j, pt: (i, 0, j, 0))
    const = lambda shape: pl.BlockSpec(shape, lambda i, j, pt: (0,) * len(shape), pipeline_mode=pl.Buffered(1))
    page_specs = [pl.BlockSpec((1, PAGE, 512), lambda i, j, pt, k=k: (pt[i, j * CHUNK_PAGES + k], 0, 0))
                  for k in range(CHUNK_PAGES)]
    return pl.pallas_call(
        functools.partial(_compress_body, n_pages=CHUNK_PAGES, rows=PAGE, paged=True),
        out_shape=(out, out),
        grid_spec=pltpu.PrefetchScalarGridSpec(
            num_scalar_prefetch=1, grid=(db, steps),
            in_specs=page_specs + [const((2, 1024, 512)), const((2, 2, 1024)), const((2, 256, DH))],
            out_specs=(ospec, ospec),
            scratch_shapes=[pltpu.VMEM((4, CHUNK_PAGES * PAGE, 128), f32), pltpu.VMEM((2, G * nch, 1024), bf16),
                            pltpu.VMEM((2, G * nch, 256), f32)]),
        compiler_params=_cp(("arbitrary", "arbitrary"), 32), name="compress_sample")(
            page_table, *([pool] * CHUNK_PAGES), w1cat, pos2, w2)


def _block_scores(impt_ref, n_blocks):
    return (impt_ref[pl.ds(1, n_blocks, stride=4), :] + impt_ref[pl.ds(2, n_blocks, stride=4), :]
            + impt_ref[pl.ds(3, n_blocks, stride=4), :])


def _force_scores(v, q_pos):
    j = lax.broadcasted_iota(jnp.int32, v.shape, 0)
    cur = q_pos >> 6
    forced = (j == 0) | (j == cur) | (j == cur - 1)
    return jnp.where(forced, FORCE, jnp.where(j > cur, -FORCE, v)), j


def _cmp_sel_prompt_body(q_ref, kc_ref, vc_ref, gt_ref, o_ref, sel_ref, impt_ref, *, n_sel):
    qt = pl.program_id(1)
    sig = jax.nn.sigmoid(gt_ref[...])
    t = qt * TQ + lax.broadcasted_iota(jnp.int32, (TQ, 128), 0)
    mp = lax.broadcasted_iota(jnp.int32, (TQ, 128), 1)
    valid = (mp >= 1) & (16 * mp + 15 <= t)
    for g in range(G):
        kc, vc = kc_ref[0, g], vc_ref[0, g]
        imp = jnp.zeros((TQ, 128), f32)
        for r in range(R):
            hh = g * R + r
            s = jnp.where(valid, _nt(q_ref[0, hh], kc), NEG)
            m = jnp.max(s, axis=-1, keepdims=True)
            e = jnp.where(valid, jnp.exp(s - m), 0.0)
            p = e / jnp.maximum(jnp.sum(e, axis=-1, keepdims=True), 1e-30)
            imp = imp + p
            o_ref[0, :, hh * DH:(hh + 1) * DH] = _dot(p.astype(bf16), vc) * sig[:, hh:hh + 1]
        for half in range(TQ // 128):
            impt_ref[...] = imp[half * 128:(half + 1) * 128, :].T
            v = _block_scores(impt_ref, n_sel)
            q_pos = qt * TQ + half * 128 + lax.broadcasted_iota(jnp.int32, v.shape, 1)
            val, j = _force_scores(v, q_pos)
            rank = jnp.zeros(v.shape, jnp.int32)
            for k in range(n_sel):
                vk = val[k:k + 1, :]
                rank = rank + jnp.where((vk > val) | ((vk == val) & (j > k)), 1, 0)
            neg = jnp.where(rank < TOPK, 0.0, NEG)
            full = jnp.concatenate([neg, jnp.zeros((128 - n_sel, 128), f32)], axis=0)
            sel_ref[0, g, half * 128:(half + 1) * 128, :] = full.T.astype(bf16)


def _cmp_sel_prompt(q_hm, kc, vc, gates, b, s):
    nq = s // TQ
    return pl.pallas_call(
        functools.partial(_cmp_sel_prompt_body, n_sel=s // SEL_BLOCK),
        out_shape=(SDS((b, s, ATT), f32), SDS((b, G, s, 128), bf16)), grid=(b, nq),
        in_specs=[pl.BlockSpec((1, H, TQ, DH), lambda i, j: (i, 0, j, 0)),
                  pl.BlockSpec((1, G, 128, DH), lambda i, j: (i, 0, 0, 0)),
                  pl.BlockSpec((1, G, 128, DH), lambda i, j: (i, 0, 0, 0)),
                  pl.BlockSpec((TQ, 128), lambda i, j: (i * nq + j, 0))],
        out_specs=(pl.BlockSpec((1, TQ, ATT), lambda i, j: (i, j, 0)),
                   pl.BlockSpec((1, G, TQ, 128), lambda i, j: (i, 0, j, 0))),
        scratch_shapes=[pltpu.VMEM((128, 128), f32)],
        compiler_params=_cp(("parallel", "parallel"), 32), name="cmp_sel_prompt")(q_hm, kc, vc, gates)


def _online_update(s, v, m_ref, l_ref, acc_ref, idx):
    m_old = m_ref[idx]
    m_new = jnp.maximum(m_old, jnp.max(s, axis=-1, keepdims=True))
    alpha = jnp.exp(m_old - m_new)
    p = jnp.exp(s - m_new)
    l_ref[idx] = alpha * l_ref[idx] + jnp.sum(p, axis=-1, keepdims=True)
    acc_ref[idx] = alpha * acc_ref[idx] + _dot(p.astype(bf16), v)
    m_ref[idx] = m_new


def _flash_prompt_body(q_ref, ks_ref, vs_ref, kw_ref, vw_ref, bs_ref, bw_ref, sel_ref, e_ref, gt_ref, o_ref,
                       m_ref, l_ref, acc_ref):
    qt, kt = pl.program_id(1), pl.program_id(2)

    @pl.when(kt == 0)
    def _():
        m_ref[...] = jnp.full(m_ref.shape, NEG, f32)
        l_ref[...] = jnp.zeros(l_ref.shape, f32)
        acc_ref[...] = jnp.zeros(acc_ref.shape, f32)

    @pl.when(kt <= qt)
    def _():
        for g in range(G):
            negmask = _dot(sel_ref[0, g], e_ref[0])
            k, v = ks_ref[0, g], vs_ref[0, g]
            for r in range(R):
                hh = g * R + r
                s = _nt(q_ref[0, hh], k) + bs_ref[0, hh] + negmask
                _online_update(s, v, m_ref, l_ref, acc_ref, hh)

    @pl.when((kt <= qt) & (kt >= qt - (WIN // TQ)))
    def _():
        for g in range(G):
            k, v = kw_ref[0, g], vw_ref[0, g]
            for r in range(R):
                hh = g * R + r
                s = _nt(q_ref[0, hh], k) + bw_ref[0, hh]
                _online_update(s, v, m_ref, l_ref, acc_ref, H + hh)

    @pl.when(kt == qt)
    def _():
        sig = jax.nn.sigmoid(gt_ref[...])
        for hh in range(H):
            o_s = acc_ref[hh] / jnp.maximum(l_ref[hh], 1e-30)
            o_w = acc_ref[H + hh] / jnp.maximum(l_ref[H + hh], 1e-30)
            o_ref[0, :, hh * DH:(hh + 1) * DH] = (sig[:, H + hh:H + hh + 1] * o_s
                                                  + sig[:, 2 * H + hh:2 * H + hh + 1] * o_w)


def _flash_prompt(q_hm, ks, vs, kw, vw, bias_s, bias_w, selneg, e_tiles, gates, b, s):
    nq = s // TQ
    nw = WIN // TQ
    kv_s = pl.BlockSpec((1, G, TQ, DH), lambda i, j, k: (i, 0, jnp.minimum(k, j), 0))
    kv_w = pl.BlockSpec((1, G, TQ, DH), lambda i, j, k: (i, 0, jnp.clip(k, jnp.maximum(j - nw, 0), j), 0))
    return pl.pallas_call(
        _flash_prompt_body, out_shape=SDS((b, s, ATT), f32), grid=(b, nq, nq),
        in_specs=[pl.BlockSpec((1, H, TQ, DH), lambda i, j, k: (i, 0, j, 0)),
                  kv_s, kv_s, kv_w, kv_w,
                  pl.BlockSpec((1, H, TQ, TQ), lambda i, j, k: (jnp.maximum(j - k, 0), 0, 0, 0)),
                  pl.BlockSpec((1, H, TQ, TQ), lambda i, j, k: (jnp.clip(j - k, 0, nw), 0, 0, 0)),
                  pl.BlockSpec((1, G, TQ, 128), lambda i, j, k: (i, 0, j, 0)),
                  pl.BlockSpec((1, 128, TQ), lambda i, j, k: (jnp.minimum(k, j), 0, 0)),
                  pl.BlockSpec((TQ, 128), lambda i, j, k: (i * nq + j, 0))],
        out_specs=pl.BlockSpec((1, TQ, ATT), lambda i, j, k: (i, j, 0)),
        scratch_shapes=[pltpu.VMEM((2 * H, TQ, 1), f32), pltpu.VMEM((2 * H, TQ, 1), f32),
                        pltpu.VMEM((2 * H, TQ, DH), f32)],
        compiler_params=_cp(("parallel", "parallel", "arbitrary"), 48), name="flash_prompt")(
            q_hm, ks, vs, kw, vw, bias_s, bias_w, selneg, e_tiles, gates)


def _cmp_sel_sample_body(q_ref, kc_ref, vc_ref, gt_ref, e_ref, o_ref, nm_ref, impt_ref, val_ref, *, past, n_q):
    n_cmp = kc_ref.shape[2]
    n_sel = n_cmp // 4
    sig = jax.nn.sigmoid(gt_ref[0])
    rows = R * n_q
    mp = lax.broadcasted_iota(jnp.int32, (rows, n_cmp), 1)
    i_q = lax.broadcasted_iota(jnp.int32, (rows, n_cmp), 0) % n_q
    valid = (mp >= 1) & (16 * mp + 15 <= past + i_q)
    imps = []
    for g in range(G):
        s = jnp.where(valid, _nt(q_ref[0, g], kc_ref[0, g]), NEG)
        m = jnp.max(s, axis=-1, keepdims=True)
        e = jnp.where(valid, jnp.exp(s - m), 0.0)
        p = e / jnp.maximum(jnp.sum(e, axis=-1, keepdims=True), 1e-30)
        o = _dot(p.astype(bf16), vc_ref[0, g]) * sig[g * rows:(g + 1) * rows, 0:1]
        for r in range(R):
            hh = g * R + r
            o_ref[0, :, hh * DH:(hh + 1) * DH] = o[r * n_q:(r + 1) * n_q, :]
        imps.append(p[0:n_q] + p[n_q:2 * n_q] + p[2 * n_q:3 * n_q])
    imp = jnp.concatenate(imps + [jnp.zeros((128 - G * n_q, n_cmp), f32)], axis=0)
    impt_ref[...] = imp.T
    v = _block_scores(impt_ref, n_sel)
    q_pos = past + lax.broadcasted_iota(jnp.int32, v.shape, 1) % n_q
    val, j = _force_scores(v, q_pos)
    val_ref[...] = val

    def count(k, rank):
        vk = val_ref[pl.ds(k, 1), :]
        return rank + jnp.where((vk > val) | ((vk == val) & (j > k)), 1, 0)

    rank = lax.fori_loop(0, n_sel, count, jnp.zeros(v.shape, jnp.int32))
    rank = rank + jnp.where(FORCE > val, 1, 0)
    neg = jnp.where(rank < TOPK, 0.0, NEG).T
    nm_ref[0] = _dot(neg[0:G * n_q, :].astype(bf16), e_ref[...])


def _cmp_sel_sample(qg, kc, vc, gsw, e_full, past):
    db, n_q = qg.shape[0], qg.shape[2] // R
    n_cmp = kc.shape[2]
    return pl.pallas_call(
        functools.partial(_cmp_sel_sample_body, past=past, n_q=n_q),
        out_shape=(SDS((db, n_q, ATT), f32), SDS((db, G * n_q, past), f32)), grid=(db,),
        in_specs=[pl.BlockSpec((1, G, R * n_q, DH), lambda i: (i, 0, 0, 0)),
                  pl.BlockSpec((1, G, n_cmp, DH), lambda i: (i, 0, 0, 0)),
                  pl.BlockSpec((1, G, n_cmp, DH), lambda i: (i, 0, 0, 0)),
                  pl.BlockSpec((1, H * n_q, 128), lambda i: (i, 0, 0)),
                  _resident((n_cmp // 4, past))],
        out_specs=(pl.BlockSpec((1, n_q, ATT), lambda i: (i, 0, 0)),
                   pl.BlockSpec((1, G * n_q, past), lambda i: (i, 0, 0))),
        scratch_shapes=[pltpu.VMEM((n_cmp, 128), f32), pltpu.VMEM((n_cmp // 4, 128), f32)],
        compiler_params=_cp(("parallel",), 32), name="cmp_sel_sample")(qg, kc, vc, gsw, e_full)


def _flash_sample_body(*refs, n_q):
    refs = refs[1:]
    pages = refs[:CHUNK_PAGES]
    (q_ref, bias_ref, nm_ref, new_ref, bnew_ref, win_ref, wnew_ref, bwin_ref, gt_ref, o_ref,
     m_ref, l_ref, acc_ref, pad_ref, wall_ref) = refs[CHUNK_PAGES:]
    step = pl.program_id(1)
    q = q_ref[0]

    @pl.when(step == 0)
    def _():
        m_ref[...] = jnp.full(m_ref.shape, NEG, f32)
        l_ref[...] = jnp.zeros(l_ref.shape, f32)
        acc_ref[...] = jnp.zeros(acc_ref.shape, f32)

    nm = nm_ref[0]
    nm = jnp.concatenate([nm[g * n_q:(g + 1) * n_q] for g in range(G) for _ in range(R)], axis=0)
    s = jnp.concatenate([_nt(q, pages[p][0, :, 0:256].astype(bf16)) for p in range(CHUNK_PAGES)], axis=1)
    s = s + bias_ref[...] + nm
    m_old = m_ref[0]
    m_new = jnp.maximum(m_old, jnp.max(s, axis=-1, keepdims=True))
    alpha = jnp.exp(m_old - m_new)
    p = jnp.exp(s - m_new)
    l_ref[0] = alpha * l_ref[0] + jnp.sum(p, axis=-1, keepdims=True)
    p = p.astype(bf16)
    pv = acc_ref[0] * alpha
    for pg in range(CHUNK_PAGES):
        pv = pv + _dot(p[:, pg * PAGE:(pg + 1) * PAGE], pages[pg][0, :, 256:512].astype(bf16))
    acc_ref[0] = pv
    m_ref[0] = m_new

    @pl.when(step == pl.num_programs(1) - 1)
    def _():
        pad_ref[...] = jnp.zeros(pad_ref.shape, f32)
        pad_ref[0:n_q, :] = new_ref[0]
        _online_update(_nt(q, pad_ref[:, 0:256].astype(bf16)) + bnew_ref[...], pad_ref[:, 256:512].astype(bf16),
                       m_ref, l_ref, acc_ref, 0)
        wall_ref[...] = jnp.zeros(wall_ref.shape, f32)
        wall_ref[0:WIN, :] = win_ref[0, 0]
        wall_ref[WIN:WIN + n_q, :] = wnew_ref[0]
        _online_update(_nt(q, wall_ref[:, 0:256].astype(bf16)) + bwin_ref[...], wall_ref[:, 256:512].astype(bf16),
                       m_ref, l_ref, acc_ref, 1)
        sig = jax.nn.sigmoid(gt_ref[0])
        res = (sig[:, 1:2] * acc_ref[0] / jnp.maximum(l_ref[0], 1e-30)
               + sig[:, 2:3] * acc_ref[1] / jnp.maximum(l_ref[1], 1e-30))
        for hh in range(H):
            g = hh // R
            o_ref[0, :, hh * DH:(hh + 1) * DH] = res[hh * n_q:(hh + 1) * n_q, g * DH:(g + 1) * DH]


def _flash_sample(pool, page_table, qbd, bias_slc, negmask, new_slc, bias_new, win_states, li, win_new, bias_win,
                  gsw):
    db, n_pages = page_table.shape
    steps = n_pages // CHUNK_PAGES
    n_q = new_slc.shape[1]
    rows = H * n_q
    keys = CHUNK_PAGES * PAGE
    cm = lambda f: (lambda i, j, pt: f(i, j))
    const = lambda shape: pl.BlockSpec(shape, lambda i, j, pt: (0,) * len(shape), pipeline_mode=pl.Buffered(1))
    per_b = lambda shape: pl.BlockSpec((1,) + shape, lambda i, j, pt: (i,) + (0,) * len(shape))
    page_specs = [pl.BlockSpec((1, PAGE, 512), lambda i, j, pt, k=k: (pt[i, j * CHUNK_PAGES + k], 0, 1))
                  for k in range(CHUNK_PAGES)]
    return pl.pallas_call(
        functools.partial(_flash_sample_body, n_q=n_q),
        out_shape=SDS((db, n_q, ATT), f32),
        grid_spec=pltpu.PrefetchScalarGridSpec(
            num_scalar_prefetch=1, grid=(db, steps),
            in_specs=page_specs + [
                per_b((rows, 256)),
                pl.BlockSpec((rows, keys), cm(lambda i, j: (0, j))),
                pl.BlockSpec((1, G * n_q, keys), cm(lambda i, j: (i, 0, j))),
                per_b((n_q, 512)), const((rows, PAGE)),
                pl.BlockSpec((1, 1, WIN, 512), lambda i, j, pt: (li, i, 0, 0)),
                per_b((n_q, 512)), const((rows, WIN + PAGE)),
                per_b((rows, 128))],
            out_specs=per_b((n_q, ATT)),
            scratch_shapes=[pltpu.VMEM((2, rows, 1), f32), pltpu.VMEM((2, rows, 1), f32),
                            pltpu.VMEM((2, rows, 256), f32), pltpu.VMEM((PAGE, 512), f32),
                            pltpu.VMEM((WIN + PAGE, 512), f32)]),
        compiler_params=_cp(("arbitrary", "arbitrary"), 48), name="flash_sample")(
            page_table, *([pool] * CHUNK_PAGES), qbd, bias_slc, negmask, new_slc, bias_new,
            win_states, win_new, bias_win, gsw)


def _dil_tile(q, kc, vc, kp, vp, bias_ref, gi, prev_off):
    lane = lax.broadcasted_iota(jnp.int32, (DT, 128), 1)
    low = lane < DH
    kcb, vcb = kc.astype(bf16), vc.astype(bf16)
    outs, lses = [], []
    for e in range(2):
        qe = jnp.where(low if e == 0 else jnp.logical_not(low), q, 0.0).astype(bf16)
        s0 = _nt(qe, kcb) + bias_ref[gi, 0, e, 0]
        m = jnp.max(s0, axis=-1, keepdims=True)
        if kp is not None:
            s1 = _nt(qe, kp.astype(bf16)) + bias_ref[gi, 0, e, 1] + prev_off
            m = jnp.maximum(m, jnp.max(s1, axis=-1, keepdims=True))
        e0 = jnp.exp(s0 - m)
        l = jnp.sum(e0, axis=-1, keepdims=True)
        o = _dot(e0.astype(bf16), vcb)
        if kp is not None:
            e1 = jnp.exp(s1 - m)
            l = l + jnp.sum(e1, axis=-1, keepdims=True)
            o = o + _dot(e1.astype(bf16), vp.astype(bf16))
        l = jnp.maximum(l, 1e-30)
        outs.append(o / l)
        lses.append(m + jnp.log(l))
    return jnp.where(low, outs[0], outs[1]), jnp.where(low, lses[0], lses[1])


def _dil_combine(os_, ls_):
    mx = jnp.maximum(jnp.maximum(ls_[0], ls_[1]), ls_[2])
    ws = [jnp.exp(l - mx) for l in ls_]
    den = ws[0] + ws[1] + ws[2]
    return [o * w / den for o, w in zip(os_, ws)]


def _dil_prompt_body(q0, q1, q2, k0, k1, k2, v0, v1, v2, bias_ref, y_ref, osc, lsc, *, s):
    qs, ks, vs = (q0, q1, q2), (k0, k1, k2), (v0, v1, v2)
    for gi, (_, dil) in enumerate(DIL):
        n_t = s // dil // DT
        q_r, k_r, v_r = qs[gi], ks[gi], vs[gi]

        def tile(start, prev_start, prev_off, gi=gi, dil=dil, q_r=q_r, k_r=k_r, v_r=v_r, n_t=n_t):
            rows = lambda st: pl.ds(st, DT, stride=dil) if dil > 1 else pl.ds(pl.multiple_of(st, DT), DT)
            kp = k_r[0, rows(prev_start), :] if n_t > 1 else None
            vp = v_r[0, rows(prev_start), :] if n_t > 1 else None
            o, lse = _dil_tile(q_r[0, rows(start), :], k_r[0, rows(start), :], v_r[0, rows(start), :],
                               kp, vp, bias_ref, gi, prev_off)
            osc[gi, rows(start), :] = o
            lsc[gi, rows(start), :] = lse

        def body(it, carry, dil=dil, n_t=n_t, tile=tile):
            rho, t = it // n_t, it % n_t
            start = rho + t * (DT * dil)
            prev = rho + jnp.maximum(t - 1, 0) * (DT * dil)
            tile(start, prev, jnp.where(t == 0, NEG, 0.0))
            return carry

        lax.fori_loop(0, dil * n_t, body, 0)
    outs = _dil_combine([osc[0], osc[1], osc[2]], [lsc[0], lsc[1], lsc[2]])
    for pair in range(2):
        @pl.when(pl.program_id(1) == pair)
        def _(pair=pair):
            for gi, o in enumerate(outs):
                c = 2 * gi + pair
                y_ref[0, :, c * 128:(c + 1) * 128] = o


def _dil_prompt(zd, bias_tiles, b, s):
    col = lambda base: [pl.BlockSpec((1, s, 128), lambda i, j, c=base + 2 * gi: (i, 0, c + j)) for gi in range(3)]
    return pl.pallas_call(
        functools.partial(_dil_prompt_body, s=s),
        out_shape=SDS((b, s, ATT), f32), grid=(b, 2),
        in_specs=col(0) + col(6) + col(12) + [pl.BlockSpec((3, 1, 2, 2, DT, DT), lambda i, j: (0, j, 0, 0, 0, 0))],
        out_specs=pl.BlockSpec((1, s, ATT), lambda i, j: (i, 0, 0)),
        scratch_shapes=[pltpu.VMEM((3, s, 128), f32), pltpu.VMEM((3, s, 128), f32)],
        compiler_params=_cp(("parallel", "arbitrary"), 48), name="dil_prompt")(*([zd] * 9), bias_tiles)


def _dil_sample_body(q_ref, s0, s1, s2, new_ref, b0, b1, b2, y_ref, n0, n1, n2, a0, a1, a2, *, n_q):
    lane = lax.broadcasted_iota(jnp.int32, (n_q, 256), 1)
    os_, ls_ = [], []
    for gi, (st, bias, nst, alls) in enumerate(((s0, b0, n0, a0), (s1, b1, n1, a1), (s2, b2, n2, a2))):
        w = st.shape[2]
        alls[...] = jnp.zeros(alls.shape, f32)
        alls[0:w, :] = st[0, 0]
        alls[w:w + n_q, :] = new_ref[0, gi]
        nst[0] = alls[n_q:w + n_q, :]
        s = _nt(q_ref[0, gi], alls[:, 0:256].astype(bf16)) + bias[...]
        m = jnp.max(s, axis=-1, keepdims=True)
        e = jnp.exp(s - m)
        l = jnp.maximum(jnp.sum(e, axis=-1, keepdims=True), 1e-30)
        res = _dot(e.astype(bf16), alls[:, 256:512].astype(bf16)) / l
        lse = m + jnp.log(l)
        o = jnp.zeros((n_q, 256), f32)
        lm = jnp.zeros((n_q, 256), f32)
        for hg in range(4):
            inb = (lane >= hg * DH) & (lane < (hg + 1) * DH)
            o = jnp.where(inb, res[hg * n_q:(hg + 1) * n_q, :], o)
            lm = jnp.where(inb, lse[hg * n_q:(hg + 1) * n_q, :], lm)
        os_.append(o)
        ls_.append(lm)
    for gi, o in enumerate(_dil_combine(os_, ls_)):
        y_ref[0, :, gi * 256:(gi + 1) * 256] = o


def _dil_sample(qbd, states, li, new_kv, biases):
    db, n_q = new_kv.shape[0], new_kv.shape[2]
    per_b = lambda shape: pl.BlockSpec((1,) + shape, lambda i: (i,) + (0,) * len(shape))
    wins = [st.shape[2] for st in states]
    return pl.pallas_call(
        functools.partial(_dil_sample_body, n_q=n_q),
        out_shape=(SDS((db, n_q, ATT), f32),) + tuple(SDS((db, w, 512), f32) for w in wins), grid=(db,),
        in_specs=[per_b((3, 4 * n_q, 256))]
                 + [pl.BlockSpec((1, 1, w, 512), lambda i: (li, i, 0, 0)) for w in wins]
                 + [per_b((3, n_q, 512))] + [_resident(bb.shape) for bb in biases],
        out_specs=(per_b((n_q, ATT)),) + tuple(per_b((w, 512)) for w in wins),
        scratch_shapes=[pltpu.VMEM((w + PAGE, 512), f32) for w in wins],
        compiler_params=_cp(("parallel",), 48), name="dil_sample")(qbd, *states, new_kv, *biases)


def _rel_bucket(dist):
    dist = jnp.maximum(dist, 0)
    d32 = jnp.maximum(dist, 1).astype(f32)
    large = 16 + (jnp.log(d32 / 16) / math.log(2048 / 16) * 16).astype(jnp.int32)
    return jnp.where(dist < 16, dist, jnp.minimum(large, 31))


def _bias_by_dist(rel_bias, n):
    return rel_bias[_rel_bucket(jnp.arange(n))].T


def _toeplitz_tiles(bd, n_delta, t):
    hn = bd.shape[0]
    bdp = jnp.concatenate([jnp.full((hn, t - 1), NEG, f32), bd[:, :n_delta * t]], axis=1)
    f = jnp.stack([bdp[:, d * t:d * t + 2 * t - 1] for d in range(n_delta)])
    gq = jnp.pad(f[..., ::-1], ((0, 0), (0, 0), (0, 1)))
    x = jnp.broadcast_to(gq[:, :, None, :], (n_delta, hn, t, 2 * t)).reshape(n_delta, hn, t * 2 * t)
    y = x[..., :t * (2 * t - 1)].reshape(n_delta, hn, t, 2 * t - 1)
    return y[..., t - 1:]


def _rows_by_query(bd, n_q, n_keys, first_dist):
    return jnp.stack([bd[:, i + first_dist - n_keys + 1:i + first_dist + 1][:, ::-1] for i in range(n_q)], axis=1)


def _nsa_layer(yp, ys, g_pre, g_post, w_in, w_out, cmp_pos, cmp_w1, cmp_w2, rel_bias, pool, page_table, win_states,
               li, b, s, db, n_q, past):
    w_pad = jnp.pad(w_in, ((0, 0), (0, NSA_PAD - w_in.shape[1]))).astype(bf16)
    w_out = w_out.astype(bf16)
    w1cat = jnp.concatenate([cmp_w1[:, :1024], cmp_w1[:, 1024:]], axis=-1).astype(bf16)
    pos2 = cmp_pos.reshape(2, 2, 1024)
    w2 = cmp_w2.astype(bf16)
    bd = _bias_by_dist(rel_bias, past + n_q)

    q_hm, kv4, kwin, ks, vs, kw, vw, gates = _nsa_proj_prompt(yp, g_pre, w_pad, b, s)
    kc, vc = _compress_prompt(kv4, w1cat, pos2, w2, b, s)
    o_cmp, selneg = _cmp_sel_prompt(q_hm, kc, vc, gates, b, s)
    bias_s = _toeplitz_tiles(bd, s // TQ, TQ)
    bias_w = _toeplitz_tiles(jnp.where(jnp.arange(bd.shape[1]) < WIN, bd, NEG), WIN // TQ + 1, TQ)
    key_blk = jnp.arange(s // TQ)[:, None, None] * (TQ // SEL_BLOCK) + jnp.arange(TQ)[None, None, :] // SEL_BLOCK
    e_tiles = (jnp.arange(128)[None, :, None] == key_blk).astype(bf16)
    o_sw = _flash_prompt(q_hm, ks, vs, kw, vw, bias_s, bias_w, selneg, e_tiles, gates, b, s)
    yp = _out_proj(yp, g_post, w_out, o_cmp.reshape(b * s, ATT), o_sw.reshape(b * s, ATT))
    kv_p = kv4.reshape(b, s, 4, G, DH)
    win_p = kwin.reshape(b, s, 2, G, DH)[:, s - min(WIN, s):]

    z = _proj(ys, g_pre, w_pad)
    q = (z[:, :ATT] * (DH ** -0.5)).reshape(db, n_q, H, DH).transpose(0, 2, 1, 3)
    qg = q.reshape(db, G, R * n_q, DH).astype(bf16)
    eye = jnp.eye(G, dtype=f32)
    qbd = (q.reshape(db, G, R * n_q, 1, DH) * eye[None, :, None, :, None]).reshape(db, H * n_q, G * DH).astype(bf16)
    kv_new = z[:, ATT:ATT + 1024].reshape(db, n_q, 1024)
    win_new = z[:, ATT + 1024:ATT + 1536].reshape(db, n_q, 512)
    gt = z[:, ATT + 1536:ATT + 1536 + 3 * H].reshape(db, n_q, 3, H).transpose(0, 3, 1, 2)
    gsw = jnp.pad(gt.reshape(db, H * n_q, 3), ((0, 0), (0, 0), (0, 125)))
    kc_s, vc_s = _compress_sample(pool, page_table, w1cat, pos2, w2)
    n_blk = past // SEL_BLOCK
    e_full = (jnp.arange(n_blk)[:, None] == jnp.arange(past)[None, :] // SEL_BLOCK).astype(bf16)
    o_cmp_s, negmask = _cmp_sel_sample(qg, kc_s, vc_s, gsw, e_full, past)
    bias_slc = _rows_by_query(bd, n_q, past, past).reshape(H * n_q, past)
    tri = jnp.arange(n_q)[:, None] - jnp.arange(PAGE)[None, :]
    bias_new = jnp.where(tri >= 0, bd[:, jnp.maximum(tri, 0)], NEG).reshape(H * n_q, PAGE)
    bdw = jnp.where(jnp.arange(bd.shape[1]) < WIN, bd, NEG)
    bias_win = jnp.concatenate([_rows_by_query(bdw, n_q, WIN, WIN).reshape(H * n_q, WIN), bias_new], axis=1)
    o_sw_s = _flash_sample(pool, page_table, qbd, bias_slc, negmask, kv_new[:, :, 512:], bias_new,
                           win_states.reshape(win_states.shape[0], db, WIN, 512), li, win_new, bias_win, gsw)
    ys = _out_proj(ys, g_post, w_out, o_cmp_s.reshape(db * n_q, ATT), o_sw_s.reshape(db * n_q, ATT))
    kv_s = kv_new.reshape(db, n_q, 4, G, DH)
    win_s = jnp.concatenate([win_states[li, :, n_q:], win_new.reshape(db, n_q, 2, G, DH)], axis=1)
    return yp, ys, kv_p, win_p, kv_s, win_s


def _dil_layer(yp, ys, g_pre, g_post, w_in, w_out, rel_bias, states, li, b, s, db, n_q):
    w_in = w_in.astype(bf16)
    w_out = w_out.astype(bf16)

    zd = _dil_proj(yp, g_pre, w_in)
    sub = jnp.arange(2 * DT)
    tiles = []
    for gi, (win, dil) in enumerate(DIL):
        bd = rel_bias[_rel_bucket(sub * dil)].T[gi * 4:(gi + 1) * 4]
        bd = jnp.where(sub * dil <= win, bd, NEG)
        tiles.append(_toeplitz_tiles(bd, 2, DT))
    bias_tiles = jnp.stack(tiles).transpose(0, 2, 1, 3, 4).reshape(3, 2, 2, 2, DT, DT)
    y = _dil_prompt(zd.reshape(b, s, 3 * ATT), bias_tiles, b, s)
    yp = _out_proj(yp, g_post, w_out, y.reshape(b * s, ATT))
    z5 = zd.reshape(b, s, 3, H, DH)
    st_p = []
    for gi, (win, _) in enumerate(DIL):
        keep = min(win, s)
        st_p.append(jnp.stack([z5[:, s - keep:, 1, gi * 4:(gi + 1) * 4], z5[:, s - keep:, 2, gi * 4:(gi + 1) * 4]], axis=2))

    z = _proj(ys, g_pre, w_in).reshape(db, n_q, 3, 3, 4, DH)
    q = (z[:, :, 0] * (DH ** -0.5)).transpose(0, 2, 3, 1, 4)
    eye = jnp.eye(4, dtype=f32)
    qbd = (q[:, :, :, :, None, :] * eye[None, None, :, None, :, None]).reshape(db, 3, 4 * n_q, 256).astype(bf16)
    new_kv = jnp.concatenate([z[:, :, 1].transpose(0, 2, 1, 3, 4).reshape(db, 3, n_q, 256),
                              z[:, :, 2].transpose(0, 2, 1, 3, 4).reshape(db, 3, n_q, 256)], axis=-1)
    biases = []
    for gi, (win, dil) in enumerate(DIL):
        dist = jnp.arange(win + n_q)
        bd = rel_bias[_rel_bucket(dist)].T[gi * 4:(gi + 1) * 4]
        bd = jnp.where((dist % dil == 0) & (dist <= win), bd, NEG)
        tri = jnp.arange(n_q)[:, None] - jnp.arange(PAGE)[None, :]
        b_new = jnp.where(tri >= 0, bd[:, jnp.maximum(tri, 0)], NEG)
        b_all = jnp.concatenate([_rows_by_query(bd, n_q, win, win), b_new], axis=2)
        biases.append(b_all.reshape(4 * n_q, win + PAGE))
    flat = [st.reshape(st.shape[0], db, st.shape[2], 512) for st in states]
    y_s, n0, n1, n2 = _dil_sample(qbd, flat, li, new_kv, biases)
    ys = _out_proj(ys, g_post, w_out, y_s.reshape(db * n_q, ATT))
    st_s = [n.reshape(st.shape[1:]) for n, st in zip((n0, n1, n2), states)]
    return yp, ys, st_p, st_s


def kernel(x_prompt, x_sample, cache_nsa_kv, page_table, state_nsa_win, state_dil_w128, state_dil_w512,
           state_dil_w2048, norm_g, ffn_w_in, ffn_w_out, nsa_w_in, nsa_w_out, nsa_cmp_pos, nsa_cmp_w1,
           nsa_cmp_w2, dil_w_in, dil_w_out, rel_bias):
    b, s, _ = x_prompt.shape
    db, n_q, _ = x_sample.shape
    depth = norm_g.shape[0]
    past = page_table.shape[1] * PAGE
    yp = x_prompt.reshape(b * s, D)
    ys = x_sample.reshape(db * n_q, D)
    nsa_kv_p, nsa_win_p, nsa_kv_s, nsa_win_s = [], [], [], []
    dil_p, dil_s = [[], [], []], [[], [], []]
    for i in range(depth):
        li = i // 2
        g = norm_g[i]
        w_i, w_o = ffn_w_in[i, 0].astype(bf16), ffn_w_out[i, 0].astype(bf16)
        yp = _half_ffn(yp, g[0:2], w_i, w_o)
        ys = _half_ffn(ys, g[0:2], w_i, w_o)
        if i % 2 == 0:
            n_pool = cache_nsa_kv.shape[1]
            pool = cache_nsa_kv.reshape(cache_nsa_kv.shape[0] * n_pool, PAGE, 1024)
            yp, ys, kvp, winp, kvs, wins = _nsa_layer(
                yp, ys, g[2:3], g[3:4], nsa_w_in[li], nsa_w_out[li], nsa_cmp_pos[li], nsa_cmp_w1[li],
                nsa_cmp_w2[li], rel_bias, pool, page_table + li * n_pool, state_nsa_win, li, b, s, db, n_q, past)
            nsa_kv_p.append(kvp)
            nsa_win_p.append(winp)
            nsa_kv_s.append(kvs)
            nsa_win_s.append(wins)
        else:
            states = (state_dil_w128, state_dil_w512, state_dil_w2048)
            yp, ys, stp, sts = _dil_layer(yp, ys, g[2:3], g[3:4], dil_w_in[li], dil_w_out[li], rel_bias, states,
                                          li, b, s, db, n_q)
            for gi in range(3):
                dil_p[gi].append(stp[gi])
                dil_s[gi].append(sts[gi])
        w_i, w_o = ffn_w_in[i, 1].astype(bf16), ffn_w_out[i, 1].astype(bf16)
        yp = _half_ffn(yp, g[4:6], w_i, w_o)
        ys = _half_ffn(ys, g[4:6], w_i, w_o)
    return (yp.reshape(b, s, D), ys.reshape(db, n_q, D),
            jnp.stack(nsa_kv_p), jnp.stack(nsa_win_p),
            jnp.stack(dil_p[0]), jnp.stack(dil_p[1]), jnp.stack(dil_p[2]),
            jnp.stack(nsa_kv_s), jnp.stack(nsa_win_s),
            jnp.stack(dil_s[0]), jnp.stack(dil_s[1]), jnp.stack(dil_s[2]))
```

```python
import functools
import math

import jax
import jax.numpy as jnp
from jax import lax
from jax.experimental import pallas as pl
from jax.experimental.pallas import tpu as pltpu

f32, bf16 = jnp.float32, jnp.bfloat16
SDS = jax.ShapeDtypeStruct

D = 1024
FF = 2816
FF_CHUNK = 256
H, DH = 12, 64
G, R = 4, 3
ATT = H * DH
EPS = 1e-6
NEG = -1e30
LOG2E = math.log2(math.e)
FORCE = 1e4
SEL_BLOCK, TOPK = 64, 16
WIN = 512
NSA_PAD = 2432
TQ = 256
RC = 32
PAGE = 128
CHUNK_PAGES = 8
DIL = ((128, 1), (512, 4), (2048, 16))
DT = 128
MIB = 1024 * 1024


def _cp(sem, vmem_mib):
    return pltpu.CompilerParams(dimension_semantics=sem, vmem_limit_bytes=vmem_mib * MIB)


def _rms(x, g):
    return x * lax.rsqrt(jnp.mean(x * x, axis=-1, keepdims=True) + EPS) * g


def _nt(a, b):
    return lax.dot_general(a, b, (((1,), (1,)), ((), ())), preferred_element_type=f32)


def _dot(a, b):
    return jnp.dot(a, b, preferred_element_type=f32)


def _resident(shape):
    nd = len(shape)
    return pl.BlockSpec(shape, lambda *_: (0,) * nd, pipeline_mode=pl.Buffered(1))


def _ffn_body(x_ref, g_ref, win_ref, wout_ref, o_ref):
    x = x_ref[...]
    h = _rms(x, g_ref[0:1, :]).astype(bf16)
    acc = jnp.zeros(x.shape, f32)
    for c in range(FF // FF_CHUNK):
        lo, hi = c * FF_CHUNK, (c + 1) * FF_CHUNK
        gate = _dot(h, win_ref[:, lo:hi])
        up = _dot(h, win_ref[:, FF + lo:FF + hi])
        act = (gate * jax.nn.sigmoid(gate) * up).astype(bf16)
        acc = acc + _dot(act, wout_ref[lo:hi, :])
    o_ref[...] = x + 0.5 * _rms(acc, g_ref[1:2, :])


def _half_ffn(x, g2, w_in, w_out):
    m = x.shape[0]
    tm = min(m, 512)
    return pl.pallas_call(
        _ffn_body, out_shape=SDS((m, D), f32), grid=(m // tm,),
        in_specs=[pl.BlockSpec((tm, D), lambda i: (i, 0)), _resident((2, D)),
                  _resident((D, 2 * FF)), _resident((FF, D))],
        out_specs=pl.BlockSpec((tm, D), lambda i: (i, 0)),
        compiler_params=_cp(("parallel",), 48), name="half_ffn")(x, g2, w_in, w_out)


def _proj_body(x_ref, g_ref, w_ref, o_ref):
    h = _rms(x_ref[...], g_ref[...]).astype(bf16)
    o_ref[...] = _dot(h, w_ref[...])


def _proj(x, g, w):
    m, n = x.shape[0], w.shape[1]
    return pl.pallas_call(
        _proj_body, out_shape=SDS((m, n), f32), grid=(1,),
        in_specs=[_resident((m, D)), _resident((1, D)), _resident((D, n))],
        out_specs=pl.BlockSpec((m, n), lambda i: (0, 0)),
        compiler_params=_cp(("arbitrary",), 32), name="proj_sample")(x, g, w)


def _nsa_proj_body(x_ref, g_ref, w_ref, q_ref, q2_ref, kv_ref, kwin_ref, ks_ref, vs_ref, kw_ref, vw_ref, gt_ref):
    h = _rms(x_ref[...], g_ref[...]).astype(bf16)
    zq = _dot(h, w_ref[:, 0:ATT]) * (DH ** -0.5)
    zq2 = zq * LOG2E
    for hh in range(H):
        q_ref[0, hh] = zq[:, hh * DH:(hh + 1) * DH].astype(bf16)
        q2_ref[0, hh] = zq2[:, hh * DH:(hh + 1) * DH].astype(bf16)
    zc = _dot(h, w_ref[:, ATT:ATT + 512])
    kv_ref[:, 0:512] = zc
    zs = _dot(h, w_ref[:, ATT + 512:ATT + 1024])
    kv_ref[:, 512:1024] = zs
    zw = _dot(h, w_ref[:, ATT + 1024:ATT + 1536])
    kwin_ref[...] = zw
    ones = jnp.ones((zs.shape[0], DH), bf16)
    for g in range(G):
        ks_ref[0, g] = zs[:, g * DH:(g + 1) * DH].astype(bf16)
        vs_ref[0, g] = jnp.concatenate([zs[:, 256 + g * DH:256 + (g + 1) * DH].astype(bf16), ones], axis=1)
        kw_ref[0, g] = zw[:, g * DH:(g + 1) * DH].astype(bf16)
        vw_ref[0, g] = jnp.concatenate([zw[:, 256 + g * DH:256 + (g + 1) * DH].astype(bf16), ones], axis=1)
    gt_ref[...] = _dot(h, w_ref[:, ATT + 1536:NSA_PAD])


def _nsa_proj_prompt(x, g, w, b, s):
    tm = 512
    nt = s // tm
    hm = lambda n, w_: pl.BlockSpec((1, n, tm, w_), lambda i: (i // nt, 0, i % nt, 0))
    rows = lambda n: pl.BlockSpec((tm, n), lambda i: (i, 0))
    return pl.pallas_call(
        _nsa_proj_body,
        out_shape=(SDS((b, H, s, DH), bf16), SDS((b, H, s, DH), bf16), SDS((b * s, 1024), f32),
                   SDS((b * s, 512), f32), SDS((b, G, s, DH), bf16), SDS((b, G, s, 2 * DH), bf16),
                   SDS((b, G, s, DH), bf16), SDS((b, G, s, 2 * DH), bf16), SDS((b * s, 128), f32)),
        grid=(b * s // tm,),
        in_specs=[rows(D), _resident((1, D)), _resident((D, NSA_PAD))],
        out_specs=(hm(H, DH), hm(H, DH), rows(1024), rows(512), hm(G, DH), hm(G, 2 * DH), hm(G, DH),
                   hm(G, 2 * DH), rows(128)),
        compiler_params=_cp(("parallel",), 48), name="nsa_proj_prompt")(x, g, w)


def _dil_proj_body(x_ref, g_ref, w_ref, o_ref):
    h = _rms(x_ref[...], g_ref[...]).astype(bf16)
    o_ref[:, 0:ATT] = _dot(h, w_ref[:, 0:ATT]) * (DH ** -0.5)
    o_ref[:, ATT:2 * ATT] = _dot(h, w_ref[:, ATT:2 * ATT])
    o_ref[:, 2 * ATT:3 * ATT] = _dot(h, w_ref[:, 2 * ATT:3 * ATT])


def _dil_proj(x, g, w):
    m = x.shape[0]
    tm = min(m, 512)
    return pl.pallas_call(
        _dil_proj_body, out_shape=SDS((m, 3 * ATT), f32), grid=(m // tm,),
        in_specs=[pl.BlockSpec((tm, D), lambda i: (i, 0)), _resident((1, D)), _resident((D, 3 * ATT))],
        out_specs=pl.BlockSpec((tm, 3 * ATT), lambda i: (i, 0)),
        compiler_params=_cp(("parallel",), 48), name="dil_proj")(x, g, w)


def _out_proj_body(x_ref, g_ref, w_ref, *refs):
    o_ref = refs[-1]
    o = refs[0][...]
    for r in refs[1:-1]:
        o = o + r[...]
    y = _dot(o.astype(bf16), w_ref[...])
    o_ref[...] = x_ref[...] + _rms(y, g_ref[...])


def _out_proj(x, g, w, *branches):
    m = x.shape[0]
    tm = min(m, 512)
    row = lambda n: pl.BlockSpec((tm, n), lambda i: (i, 0))
    return pl.pallas_call(
        _out_proj_body, out_shape=SDS((m, D), f32), grid=(m // tm,),
        in_specs=[row(D), _resident((1, D)), _resident((ATT, D))] + [row(ATT)] * len(branches),
        out_specs=row(D), compiler_params=_cp(("parallel",), 32), name="out_proj")(x, g, w, *branches)


def _compress_body(*refs, n_pages, rows, paged):
    if paged:
        refs = refs[1:]
    pages = refs[:n_pages]
    w1_ref, pos_ref, w2_ref, kc_ref, vc_ref, cs_ref, lhs_ref, aprev_ref = refs[n_pages:]
    nch = n_pages * rows // 16
    first = (pl.program_id(1) == 0) if paged else True

    for p in range(n_pages):
        for c in range(4):
            cs_ref[c, p * rows:(p + 1) * rows, :] = pages[p][0, :, c * 128:(c + 1) * 128]

    lane = lax.broadcasted_iota(jnp.int32, (nch, 128), 1)
    low = lane < DH
    for kind in range(2):
        for pair in range(2):
            c = kind * 2 + pair
            for r2 in range(8):
                va = cs_ref[c, pl.ds(2 * r2, nch, stride=16), :]
                vb = cs_ref[c, pl.ds(2 * r2 + 1, nch, stride=16), :]
                g_lo = jnp.where(low, va, pltpu.roll(vb, DH, 1))
                g_hi = jnp.where(low, pltpu.roll(va, DH, 1), vb)
                lhs_ref[kind, (2 * pair) * nch:(2 * pair + 1) * nch, r2 * 128:(r2 + 1) * 128] = g_lo.astype(bf16)
                lhs_ref[kind, (2 * pair + 1) * nch:(2 * pair + 2) * nch, r2 * 128:(r2 + 1) * 128] = g_hi.astype(bf16)

    if paged:
        @pl.when(first)
        def _():
            aprev_ref[...] = jnp.zeros(aprev_ref.shape, f32)

    row = lax.broadcasted_iota(jnp.int32, (nch, 256), 0)
    row_o = lax.broadcasted_iota(jnp.int32, (nch, DH), 0)
    for kind, out_ref in ((0, kc_ref), (1, vc_ref)):
        pp = _dot(lhs_ref[kind], w1_ref[kind])
        pos = _dot(pos_ref[kind].astype(bf16), w1_ref[kind])
        posterm = pos[0:1, 0:256] + pos[1:2, 256:512]
        for g in range(G):
            a = pp[g * nch:(g + 1) * nch, 0:256]
            bm = pp[g * nch:(g + 1) * nch, 256:512]
            shifted = pltpu.roll(a, 1, 0)
            if paged:
                carry = aprev_ref[kind, (g + 1) * nch - 1:(g + 1) * nch, :]
                shifted = jnp.where(row == 0, carry, shifted)
                aprev_ref[kind, g * nch:(g + 1) * nch, :] = a
            pre = shifted + bm + posterm
            hid = (pre * jax.nn.sigmoid(pre)).astype(bf16)
            out = _dot(hid, w2_ref[kind])
            out = jnp.where((row_o == 0) & first, 0.0, out)
            out_ref[0, g] = out.astype(bf16)


def _compress_prompt(kv4, w1cat, pos2, w2, b, s):
    nch = s // 16
    out = SDS((b, G, nch, DH), bf16)
    ospec = pl.BlockSpec((1, G, nch, DH), lambda i: (i, 0, 0, 0))
    return pl.pallas_call(
        functools.partial(_compress_body, n_pages=1, rows=s, paged=False),
        out_shape=(out, out), grid=(b,),
        in_specs=[pl.BlockSpec((1, s, 512), lambda i: (i, 0, 0)),
                  _resident((2, 1024, 512)), _resident((2, 2, 1024)), _resident((2, 256, DH))],
        out_specs=(ospec, ospec),
        scratch_shapes=[pltpu.VMEM((4, s, 128), f32), pltpu.VMEM((2, G * nch, 1024), bf16),
                        pltpu.VMEM((2, G * nch, 256), f32)],
        compiler_params=_cp(("parallel",), 48), name="compress_prompt")(kv4.reshape(b, s, 1024), w1cat, pos2, w2)


def _compress_sample(pool, page_table, w1cat, pos2, w2):
    db, n_pages = page_table.shape
    steps = n_pages // CHUNK_PAGES
    nch = CHUNK_PAGES * PAGE // 16
    out = SDS((db, G, n_pages * PAGE // 16, DH), bf16)
    ospec = pl.BlockSpec((1, G, nch, DH), lambda i, j, pt: (i, 0, j, 0))
    const = lambda shape: pl.BlockSpec(shape, lambda i, j, pt: (0,) * len(shape), pipeline_mode=pl.Buffered(1))
    page_specs = [pl.BlockSpec((1, PAGE, 512), lambda i, j, pt, k=k: (pt[i, j * CHUNK_PAGES + k], 0, 0))
                  for k in range(CHUNK_PAGES)]
    return pl.pallas_call(
        functools.partial(_compress_body, n_pages=CHUNK_PAGES, rows=PAGE, paged=True),
        out_shape=(out, out),
        grid_spec=pltpu.PrefetchScalarGridSpec(
            num_scalar_prefetch=1, grid=(db, steps),
            in_specs=page_specs + [const((2, 1024, 512)), const((2, 2, 1024)), const((2, 256, DH))],
            out_specs=(ospec, ospec),
            scratch_shapes=[pltpu.VMEM((4, CHUNK_PAGES * PAGE, 128), f32), pltpu.VMEM((2, G * nch, 1024), bf16),
                            pltpu.VMEM((2, G * nch, 256), f32)]),
        compiler_params=_cp(("arbitrary", "arbitrary"), 32), name="compress_sample")(
            page_table, *([pool] * CHUNK_PAGES), w1cat, pos2, w2)


def _block_scores(impt_ref, n_blocks):
    return (impt_ref[pl.ds(1, n_blocks, stride=4), :] + impt_ref[pl.ds(2, n_blocks, stride=4), :]
            + impt_ref[pl.ds(3, n_blocks, stride=4), :])


def _force_scores(v, q_pos):
    j = lax.broadcasted_iota(jnp.int32, v.shape, 0)
    cur = q_pos >> 6
    forced = (j == 0) | (j == cur) | (j == cur - 1)
    return jnp.where(forced, FORCE, jnp.where(j > cur, -FORCE, v)), j


def _cmp_sel_prompt_body(q_ref, kc_ref, vc_ref, gt_ref, o_ref, sel_ref, impt_ref, *, n_sel):
    qt = pl.program_id(1)
    sig = jax.nn.sigmoid(gt_ref[...])
    t = qt * TQ + lax.broadcasted_iota(jnp.int32, (TQ, 128), 0)
    mp = lax.broadcasted_iota(jnp.int32, (TQ, 128), 1)
    valid = (mp >= 1) & (16 * mp + 15 <= t)
    for g in range(G):
        kc, vc = kc_ref[0, g], vc_ref[0, g]
        imp = jnp.zeros((TQ, 128), f32)
        for r in range(R):
            hh = g * R + r
            s = jnp.where(valid, _nt(q_ref[0, hh], kc), NEG)
            m = jnp.max(s, axis=-1, keepdims=True)
            e = jnp.where(valid, jnp.exp(s - m), 0.0)
            p = e / jnp.maximum(jnp.sum(e, axis=-1, keepdims=True), 1e-30)
            imp = imp + p
            o_ref[0, :, hh * DH:(hh + 1) * DH] = _dot(p.astype(bf16), vc) * sig[:, hh:hh + 1]
        for half in range(TQ // 128):
            impt_ref[...] = imp[half * 128:(half + 1) * 128, :].T
            v = _block_scores(impt_ref, n_sel)
            q_pos = qt * TQ + half * 128 + lax.broadcasted_iota(jnp.int32, v.shape, 1)
            val, j = _force_scores(v, q_pos)
            rank = jnp.zeros(v.shape, jnp.int32)
            for k in range(n_sel):
                vk = val[k:k + 1, :]
                rank = rank + jnp.where((vk > val) | ((vk == val) & (j > k)), 1, 0)
            neg = jnp.where(rank < TOPK, 0.0, NEG)
            full = jnp.concatenate([neg, jnp.zeros((128 - n_sel, 128), f32)], axis=0)
            sel_ref[0, g, half * 128:(half + 1) * 128, :] = full.T.astype(bf16)


def _cmp_sel_prompt(q_hm, kc, vc, gates, b, s):
    nq = s // TQ
    return pl.pallas_call(
        functools.partial(_cmp_sel_prompt_body, n_sel=s // SEL_BLOCK),
        out_shape=(SDS((b, s, ATT), f32), SDS((b, G, s, 128), bf16)), grid=(b, nq),
        in_specs=[pl.BlockSpec((1, H, TQ, DH), lambda i, j: (i, 0, j, 0)),
                  pl.BlockSpec((1, G, 128, DH), lambda i, j: (i, 0, 0, 0)),
                  pl.BlockSpec((1, G, 128, DH), lambda i, j: (i, 0, 0, 0)),
                  pl.BlockSpec((TQ, 128), lambda i, j: (i * nq + j, 0))],
        out_specs=(pl.BlockSpec((1, TQ, ATT), lambda i, j: (i, j, 0)),
                   pl.BlockSpec((1, G, TQ, 128), lambda i, j: (i, 0, j, 0))),
        scratch_shapes=[pltpu.VMEM((128, 128), f32)],
        compiler_params=_cp(("parallel", "parallel"), 32), name="cmp_sel_prompt")(q_hm, kc, vc, gates)


def _online_update(s, v, m_ref, l_ref, acc_ref, idx):
    m_old = m_ref[idx]
    m_new = jnp.maximum(m_old, jnp.max(s, axis=-1, keepdims=True))
    alpha = jnp.exp(m_old - m_new)
    p = jnp.exp(s - m_new)
    l_ref[idx] = alpha * l_ref[idx] + jnp.sum(p, axis=-1, keepdims=True)
    acc_ref[idx] = alpha * acc_ref[idx] + _dot(p.astype(bf16), v)
    m_ref[idx] = m_new


def _group_update(qa, ka, va, bias_ref, delta, h0, m_ref, acc_ref, idx0, s_ref, p_ref, a_ref):
    s_ref[...] = _nt(qa, ka)
    for r in range(R):
        for c in range(TQ // RC):
            lo = c * RC
            s = s_ref[r * TQ + lo:r * TQ + lo + RC, :] + bias_ref[delta, h0 + r, lo:lo + RC, :]
            m_old = m_ref[idx0 + r, lo:lo + RC, :]
            m_new = jnp.maximum(m_old, jnp.max(s, axis=-1, keepdims=True))
            p_ref[r * TQ + lo:r * TQ + lo + RC, :] = jnp.exp2(
                s - jnp.concatenate([m_new] * (TQ // 128), axis=1)).astype(bf16)
            m_ref[idx0 + r, lo:lo + RC, :] = m_new
            a_ref[r * TQ + lo:r * TQ + lo + RC, :] = jnp.exp2(m_old - m_new)
    pv = _dot(p_ref[...], va)
    for r in range(R):
        acc_ref[idx0 + r] = a_ref[r * TQ:(r + 1) * TQ, :] * acc_ref[idx0 + r] + pv[r * TQ:(r + 1) * TQ, :]


def _flash_prompt_body(q_ref, ks_ref, vs_ref, kw_ref, vw_ref, bs_ref, bw_ref, sel_ref, et_ref, gt_ref, o_ref,
                       m_ref, acc_ref, s_ref, p_ref, a_ref):
    qt, kt = pl.program_id(1), pl.program_id(2)
    delta = qt - kt
    scratch = (s_ref, p_ref, a_ref)

    @pl.when(kt == 0)
    def _():
        m_ref[...] = jnp.full(m_ref.shape, NEG, f32)
        acc_ref[...] = jnp.zeros(acc_ref.shape, f32)

    @pl.when(kt <= qt)
    def _():
        for g in range(G):
            q3 = q_ref[0, g * R:(g + 1) * R].reshape(R * TQ, DH)
            qa = jnp.concatenate([jnp.concatenate([sel_ref[0, g]] * R, axis=0), q3], axis=1)
            ka = jnp.concatenate([et_ref[0], ks_ref[0, g]], axis=1)
            _group_update(qa, ka, vs_ref[0, g], bs_ref, delta, g * R, m_ref, acc_ref, g * R, *scratch)

    @pl.when((kt <= qt) & (kt >= qt - (WIN // TQ)))
    def _():
        for g in range(G):
            q3 = q_ref[0, g * R:(g + 1) * R].reshape(R * TQ, DH)
            _group_update(q3, kw_ref[0, g], vw_ref[0, g], bw_ref, delta, g * R, m_ref, acc_ref, H + g * R, *scratch)

    @pl.when(kt == qt)
    def _():
        sig = jax.nn.sigmoid(gt_ref[...])
        for hh in range(H):
            a_s, a_w = acc_ref[hh], acc_ref[H + hh]
            o_s = a_s / jnp.maximum(pltpu.roll(a_s, DH, 1), 1e-30)
            o_w = a_w / jnp.maximum(pltpu.roll(a_w, DH, 1), 1e-30)
            o = sig[:, H + hh:H + hh + 1] * o_s + sig[:, 2 * H + hh:2 * H + hh + 1] * o_w
            o_ref[0, :, hh * DH:(hh + 1) * DH] = o[:, 0:DH]


def _flash_prompt(q2, ks, vs, kw, vw, bias_s, bias_w, selneg, et_tiles, gates, b, s):
    nq = s // TQ
    nw = WIN // TQ
    kv = lambda w_, f: pl.BlockSpec((1, G, TQ, w_), lambda i, j, k: (i, 0, f(j, k), 0))
    sel_kt = lambda j, k: jnp.minimum(k, j)
    win_kt = lambda j, k: jnp.clip(k, jnp.maximum(j - nw, 0), j)
    return pl.pallas_call(
        _flash_prompt_body, out_shape=SDS((b, s, ATT), f32), grid=(b, nq, nq),
        in_specs=[pl.BlockSpec((1, H, TQ, DH), lambda i, j, k: (i, 0, j, 0)),
                  kv(DH, sel_kt), kv(2 * DH, sel_kt), kv(DH, win_kt), kv(2 * DH, win_kt),
                  _resident(bias_s.shape), _resident(bias_w.shape),
                  pl.BlockSpec((1, G, TQ, 128), lambda i, j, k: (i, 0, j, 0)),
                  pl.BlockSpec((1, TQ, 128), lambda i, j, k: (jnp.minimum(k, j), 0, 0)),
                  pl.BlockSpec((TQ, 128), lambda i, j, k: (i * nq + j, 0))],
        out_specs=pl.BlockSpec((1, TQ, ATT), lambda i, j, k: (i, j, 0)),
        scratch_shapes=[pltpu.VMEM((2 * H, TQ, 128), f32), pltpu.VMEM((2 * H, TQ, 128), f32),
                        pltpu.VMEM((R * TQ, TQ), f32), pltpu.VMEM((R * TQ, TQ), bf16), pltpu.VMEM((R * TQ, 128), f32)],
        compiler_params=_cp(("parallel", "parallel", "arbitrary"), 58), name="flash_prompt")(
            q2, ks, vs, kw, vw, bias_s, bias_w, selneg, et_tiles, gates)


def _cmp_sel_sample_body(q_ref, kc_ref, vc_ref, gt_ref, e_ref, o_ref, nm_ref, impt_ref, val_ref, *, past, n_q):
    n_cmp = kc_ref.shape[2]
    n_sel = n_cmp // 4
    sig = jax.nn.sigmoid(gt_ref[0])
    rows = R * n_q
    mp = lax.broadcasted_iota(jnp.int32, (rows, n_cmp), 1)
    i_q = lax.broadcasted_iota(jnp.int32, (rows, n_cmp), 0) % n_q
    valid = (mp >= 1) & (16 * mp + 15 <= past + i_q)
    imps = []
    for g in range(G):
        s = jnp.where(valid, _nt(q_ref[0, g], kc_ref[0, g]), NEG)
        m = jnp.max(s, axis=-1, keepdims=True)
        e = jnp.where(valid, jnp.exp(s - m), 0.0)
        p = e / jnp.maximum(jnp.sum(e, axis=-1, keepdims=True), 1e-30)
        o = _dot(p.astype(bf16), vc_ref[0, g]) * sig[g * rows:(g + 1) * rows, 0:1]
        for r in range(R):
            hh = g * R + r
            o_ref[0, :, hh * DH:(hh + 1) * DH] = o[r * n_q:(r + 1) * n_q, :]
        imps.append(p[0:n_q] + p[n_q:2 * n_q] + p[2 * n_q:3 * n_q])
    imp = jnp.concatenate(imps + [jnp.zeros((128 - G * n_q, n_cmp), f32)], axis=0)
    impt_ref[...] = imp.T
    v = _block_scores(impt_ref, n_sel)
    q_pos = past + lax.broadcasted_iota(jnp.int32, v.shape, 1) % n_q
    val, j = _force_scores(v, q_pos)
    val_ref[...] = val

    def count(k, rank):
        vk = val_ref[pl.ds(k, 1), :]
        return rank + jnp.where((vk > val) | ((vk == val) & (j > k)), 1, 0)

    rank = lax.fori_loop(0, n_sel, count, jnp.zeros(v.shape, jnp.int32))
    rank = rank + jnp.where(FORCE > val, 1, 0)
    neg = jnp.where(rank < TOPK, 0.0, NEG).T
    nm_ref[0] = _dot(neg[0:G * n_q, :].astype(bf16), e_ref[...])


def _cmp_sel_sample(qg, kc, vc, gsw, e_full, past):
    db, n_q = qg.shape[0], qg.shape[2] // R
    n_cmp = kc.shape[2]
    return pl.pallas_call(
        functools.partial(_cmp_sel_sample_body, past=past, n_q=n_q),
        out_shape=(SDS((db, n_q, ATT), f32), SDS((db, G * n_q, past), f32)), grid=(db,),
        in_specs=[pl.BlockSpec((1, G, R * n_q, DH), lambda i: (i, 0, 0, 0)),
                  pl.BlockSpec((1, G, n_cmp, DH), lambda i: (i, 0, 0, 0)),
                  pl.BlockSpec((1, G, n_cmp, DH), lambda i: (i, 0, 0, 0)),
                  pl.BlockSpec((1, H * n_q, 128), lambda i: (i, 0, 0)),
                  _resident((n_cmp // 4, past))],
        out_specs=(pl.BlockSpec((1, n_q, ATT), lambda i: (i, 0, 0)),
                   pl.BlockSpec((1, G * n_q, past), lambda i: (i, 0, 0))),
        scratch_shapes=[pltpu.VMEM((n_cmp, 128), f32), pltpu.VMEM((n_cmp // 4, 128), f32)],
        compiler_params=_cp(("parallel",), 32), name="cmp_sel_sample")(qg, kc, vc, gsw, e_full)


def _flash_sample_body(*refs, n_q):
    refs = refs[1:]
    pages = refs[:CHUNK_PAGES]
    (q_ref, bias_ref, nm_ref, new_ref, bnew_ref, win_ref, wnew_ref, bwin_ref, gt_ref, o_ref,
     m_ref, l_ref, acc_ref, pad_ref, wall_ref) = refs[CHUNK_PAGES:]
    step = pl.program_id(1)
    q = q_ref[0]

    @pl.when(step == 0)
    def _():
        m_ref[...] = jnp.full(m_ref.shape, NEG, f32)
        l_ref[...] = jnp.zeros(l_ref.shape, f32)
        acc_ref[...] = jnp.zeros(acc_ref.shape, f32)

    nm = nm_ref[0]
    nm = jnp.concatenate([nm[g * n_q:(g + 1) * n_q] for g in range(G) for _ in range(R)], axis=0)
    s = jnp.concatenate([_nt(q, pages[p][0, :, 0:256].astype(bf16)) for p in range(CHUNK_PAGES)], axis=1)
    s = s + bias_ref[...] + nm
    m_old = m_ref[0]
    m_new = jnp.maximum(m_old, jnp.max(s, axis=-1, keepdims=True))
    alpha = jnp.exp(m_old - m_new)
    p = jnp.exp(s - m_new)
    l_ref[0] = alpha * l_ref[0] + jnp.sum(p, axis=-1, keepdims=True)
    p = p.astype(bf16)
    pv = acc_ref[0] * alpha
    for pg in range(CHUNK_PAGES):
        pv = pv + _dot(p[:, pg * PAGE:(pg + 1) * PAGE], pages[pg][0, :, 256:512].astype(bf16))
    acc_ref[0] = pv
    m_ref[0] = m_new

    @pl.when(step == pl.num_programs(1) - 1)
    def _():
        pad_ref[...] = jnp.zeros(pad_ref.shape, f32)
        pad_ref[0:n_q, :] = new_ref[0]
        _online_update(_nt(q, pad_ref[:, 0:256].astype(bf16)) + bnew_ref[...], pad_ref[:, 256:512].astype(bf16),
                       m_ref, l_ref, acc_ref, 0)
        wall_ref[...] = jnp.zeros(wall_ref.shape, f32)
        wall_ref[0:WIN, :] = win_ref[0, 0]
        wall_ref[WIN:WIN + n_q, :] = wnew_ref[0]
        _online_update(_nt(q, wall_ref[:, 0:256].astype(bf16)) + bwin_ref[...], wall_ref[:, 256:512].astype(bf16),
                       m_ref, l_ref, acc_ref, 1)
        sig = jax.nn.sigmoid(gt_ref[0])
        res = (sig[:, 1:2] * acc_ref[0] / jnp.maximum(l_ref[0], 1e-30)
               + sig[:, 2:3] * acc_ref[1] / jnp.maximum(l_ref[1], 1e-30))
        for hh in range(H):
            g = hh // R
            o_ref[0, :, hh * DH:(hh + 1) * DH] = res[hh * n_q:(hh + 1) * n_q, g * DH:(g + 1) * DH]


def _flash_sample(pool, page_table, qbd, bias_slc, negmask, new_slc, bias_new, win_states, li, win_new, bias_win,
                  gsw):
    db, n_pages = page_table.shape
    steps = n_pages // CHUNK_PAGES
    n_q = new_slc.shape[1]
    rows = H * n_q
    keys = CHUNK_PAGES * PAGE
    cm = lambda f: (lambda i, j, pt: f(i, j))
    const = lambda shape: pl.BlockSpec(shape, lambda i, j, pt: (0,) * len(shape), pipeline_mode=pl.Buffered(1))
    per_b = lambda shape: pl.BlockSpec((1,) + shape, lambda i, j, pt: (i,) + (0,) * len(shape))
    page_specs = [pl.BlockSpec((1, PAGE, 512), lambda i, j, pt, k=k: (pt[i, j * CHUNK_PAGES + k], 0, 1))
                  for k in range(CHUNK_PAGES)]
    return pl.pallas_call(
        functools.partial(_flash_sample_body, n_q=n_q),
        out_shape=SDS((db, n_q, ATT), f32),
        grid_spec=pltpu.PrefetchScalarGridSpec(
            num_scalar_prefetch=1, grid=(db, steps),
            in_specs=page_specs + [
                per_b((rows, 256)),
                pl.BlockSpec((rows, keys), cm(lambda i, j: (0, j))),
                pl.BlockSpec((1, G * n_q, keys), cm(lambda i, j: (i, 0, j))),
                per_b((n_q, 512)), const((rows, PAGE)),
                pl.BlockSpec((1, 1, WIN, 512), lambda i, j, pt: (li, i, 0, 0)),
                per_b((n_q, 512)), const((rows, WIN + PAGE)),
                per_b((rows, 128))],
            out_specs=per_b((n_q, ATT)),
            scratch_shapes=[pltpu.VMEM((2, rows, 1), f32), pltpu.VMEM((2, rows, 1), f32),
                            pltpu.VMEM((2, rows, 256), f32), pltpu.VMEM((PAGE, 512), f32),
                            pltpu.VMEM((WIN + PAGE, 512), f32)]),
        compiler_params=_cp(("arbitrary", "arbitrary"), 48), name="flash_sample")(
            page_table, *([pool] * CHUNK_PAGES), qbd, bias_slc, negmask, new_slc, bias_new,
            win_states, win_new, bias_win, gsw)


def _dil_tile(q, kc, vc, kp, vp, bias_ref, gi, prev_off):
    lane = lax.broadcasted_iota(jnp.int32, (DT, 128), 1)
    low = lane < DH
    kcb, vcb = kc.astype(bf16), vc.astype(bf16)
    outs, lses = [], []
    for e in range(2):
        qe = jnp.where(low if e == 0 else jnp.logical_not(low), q, 0.0).astype(bf16)
        s0 = _nt(qe, kcb) + bias_ref[gi, 0, e, 0]
        m = jnp.max(s0, axis=-1, keepdims=True)
        if kp is not None:
            s1 = _nt(qe, kp.astype(bf16)) + bias_ref[gi, 0, e, 1] + prev_off
            m = jnp.maximum(m, jnp.max(s1, axis=-1, keepdims=True))
        e0 = jnp.exp(s0 - m)
        l = jnp.sum(e0, axis=-1, keepdims=True)
        o = _dot(e0.astype(bf16), vcb)
        if kp is not None:
            e1 = jnp.exp(s1 - m)
            l = l + jnp.sum(e1, axis=-1, keepdims=True)
            o = o + _dot(e1.astype(bf16), vp.astype(bf16))
        l = jnp.maximum(l, 1e-30)
        outs.append(o / l)
        lses.append(m + jnp.log(l))
    return jnp.where(low, outs[0], outs[1]), jnp.where(low, lses[0], lses[1])


def _dil_combine(os_, ls_):
    mx = jnp.maximum(jnp.maximum(ls_[0], ls_[1]), ls_[2])
    ws = [jnp.exp(l - mx) for l in ls_]
    den = ws[0] + ws[1] + ws[2]
    return [o * w / den for o, w in zip(os_, ws)]


def _dil_prompt_body(q0, q1, q2, k0, k1, k2, v0, v1, v2, bias_ref, y_ref, osc, lsc, *, s):
    qs, ks, vs = (q0, q1, q2), (k0, k1, k2), (v0, v1, v2)
    for gi, (_, dil) in enumerate(DIL):
        n_t = s // dil // DT
        q_r, k_r, v_r = qs[gi], ks[gi], vs[gi]

        def tile(start, prev_start, prev_off, gi=gi, dil=dil, q_r=q_r, k_r=k_r, v_r=v_r, n_t=n_t):
            rows = lambda st: pl.ds(st, DT, stride=dil) if dil > 1 else pl.ds(pl.multiple_of(st, DT), DT)
            kp = k_r[0, rows(prev_start), :] if n_t > 1 else None
            vp = v_r[0, rows(prev_start), :] if n_t > 1 else None
            o, lse = _dil_tile(q_r[0, rows(start), :], k_r[0, rows(start), :], v_r[0, rows(start), :],
                               kp, vp, bias_ref, gi, prev_off)
            osc[gi, rows(start), :] = o
            lsc[gi, rows(start), :] = lse

        def body(it, carry, dil=dil, n_t=n_t, tile=tile):
            rho, t = it // n_t, it % n_t
            start = rho + t * (DT * dil)
            prev = rho + jnp.maximum(t - 1, 0) * (DT * dil)
            tile(start, prev, jnp.where(t == 0, NEG, 0.0))
            return carry

        lax.fori_loop(0, dil * n_t, body, 0)
    outs = _dil_combine([osc[0], osc[1], osc[2]], [lsc[0], lsc[1], lsc[2]])
    for pair in range(2):
        @pl.when(pl.program_id(1) == pair)
        def _(pair=pair):
            for gi, o in enumerate(outs):
                c = 2 * gi + pair
                y_ref[0, :, c * 128:(c + 1) * 128] = o


def _dil_prompt(zd, bias_tiles, b, s):
    col = lambda base: [pl.BlockSpec((1, s, 128), lambda i, j, c=base + 2 * gi: (i, 0, c + j)) for gi in range(3)]
    return pl.pallas_call(
        functools.partial(_dil_prompt_body, s=s),
        out_shape=SDS((b, s, ATT), f32), grid=(b, 2),
        in_specs=col(0) + col(6) + col(12) + [pl.BlockSpec((3, 1, 2, 2, DT, DT), lambda i, j: (0, j, 0, 0, 0, 0))],
        out_specs=pl.BlockSpec((1, s, ATT), lambda i, j: (i, 0, 0)),
        scratch_shapes=[pltpu.VMEM((3, s, 128), f32), pltpu.VMEM((3, s, 128), f32)],
        compiler_params=_cp(("parallel", "arbitrary"), 48), name="dil_prompt")(*([zd] * 9), bias_tiles)


def _dil_sample_body(q_ref, s0, s1, s2, new_ref, b0, b1, b2, y_ref, n0, n1, n2, a0, a1, a2, *, n_q):
    lane = lax.broadcasted_iota(jnp.int32, (n_q, 256), 1)
    os_, ls_ = [], []
    for gi, (st, bias, nst, alls) in enumerate(((s0, b0, n0, a0), (s1, b1, n1, a1), (s2, b2, n2, a2))):
        w = st.shape[2]
        alls[...] = jnp.zeros(alls.shape, f32)
        alls[0:w, :] = st[0, 0]
        alls[w:w + n_q, :] = new_ref[0, gi]
        nst[0] = alls[n_q:w + n_q, :]
        s = _nt(q_ref[0, gi], alls[:, 0:256].astype(bf16)) + bias[...]
        m = jnp.max(s, axis=-1, keepdims=True)
        e = jnp.exp(s - m)
        l = jnp.maximum(jnp.sum(e, axis=-1, keepdims=True), 1e-30)
        res = _dot(e.astype(bf16), alls[:, 256:512].astype(bf16)) / l
        lse = m + jnp.log(l)
        o = jnp.zeros((n_q, 256), f32)
        lm = jnp.zeros((n_q, 256), f32)
        for hg in range(4):
            inb = (lane >= hg * DH) & (lane < (hg + 1) * DH)
            o = jnp.where(inb, res[hg * n_q:(hg + 1) * n_q, :], o)
            lm = jnp.where(inb, lse[hg * n_q:(hg + 1) * n_q, :], lm)
        os_.append(o)
        ls_.append(lm)
    for gi, o in enumerate(_dil_combine(os_, ls_)):
        y_ref[0, :, gi * 256:(gi + 1) * 256] = o


def _dil_sample(qbd, states, li, new_kv, biases):
    db, n_q = new_kv.shape[0], new_kv.shape[2]
    per_b = lambda shape: pl.BlockSpec((1,) + shape, lambda i: (i,) + (0,) * len(shape))
    wins = [st.shape[2] for st in states]
    return pl.pallas_call(
        functools.partial(_dil_sample_body, n_q=n_q),
        out_shape=(SDS((db, n_q, ATT), f32),) + tuple(SDS((db, w, 512), f32) for w in wins), grid=(db,),
        in_specs=[per_b((3, 4 * n_q, 256))]
                 + [pl.BlockSpec((1, 1, w, 512), lambda i: (li, i, 0, 0)) for w in wins]
                 + [per_b((3, n_q, 512))] + [_resident(bb.shape) for bb in biases],
        out_specs=(per_b((n_q, ATT)),) + tuple(per_b((w, 512)) for w in wins),
        scratch_shapes=[pltpu.VMEM((w + PAGE, 512), f32) for w in wins],
        compiler_params=_cp(("parallel",), 48), name="dil_sample")(qbd, *states, new_kv, *biases)


def _rel_bucket(dist):
    dist = jnp.maximum(dist, 0)
    d32 = jnp.maximum(dist, 1).astype(f32)
    large = 16 + (jnp.log(d32 / 16) / math.log(2048 / 16) * 16).astype(jnp.int32)
    return jnp.where(dist < 16, dist, jnp.minimum(large, 31))


def _bias_by_dist(rel_bias, n):
    return rel_bias[_rel_bucket(jnp.arange(n))].T


def _toeplitz_tiles(bd, n_delta, t):
    hn = bd.shape[0]
    bdp = jnp.concatenate([jnp.full((hn, t - 1), NEG, f32), bd[:, :n_delta * t]], axis=1)
    f = jnp.stack([bdp[:, d * t:d * t + 2 * t - 1] for d in range(n_delta)])
    gq = jnp.pad(f[..., ::-1], ((0, 0), (0, 0), (0, 1)))
    x = jnp.broadcast_to(gq[:, :, None, :], (n_delta, hn, t, 2 * t)).reshape(n_delta, hn, t * 2 * t)
    y = x[..., :t * (2 * t - 1)].reshape(n_delta, hn, t, 2 * t - 1)
    return y[..., t - 1:]


def _rows_by_query(bd, n_q, n_keys, first_dist):
    return jnp.stack([bd[:, i + first_dist - n_keys + 1:i + first_dist + 1][:, ::-1] for i in range(n_q)], axis=1)


def _nsa_tables(rel_bias, s, n_q, past):
    bd = _bias_by_dist(rel_bias, past + n_q)
    bdw = jnp.where(jnp.arange(bd.shape[1]) < WIN, bd, NEG)
    key_blk = jnp.arange(s // TQ)[:, None, None] * (TQ // SEL_BLOCK) + jnp.arange(TQ)[None, :, None] // SEL_BLOCK
    tri = jnp.arange(n_q)[:, None] - jnp.arange(PAGE)[None, :]
    bias_new = jnp.where(tri >= 0, bd[:, jnp.maximum(tri, 0)], NEG).reshape(H * n_q, PAGE)
    return dict(
        bias_s=_toeplitz_tiles(bd, s // TQ, TQ) * LOG2E,
        bias_w=_toeplitz_tiles(bdw, WIN // TQ + 1, TQ) * LOG2E,
        et_tiles=(key_blk == jnp.arange(128)[None, None, :]).astype(bf16),
        e_full=(jnp.arange(past // SEL_BLOCK)[:, None] == jnp.arange(past)[None, :] // SEL_BLOCK).astype(bf16),
        bias_slc=_rows_by_query(bd, n_q, past, past).reshape(H * n_q, past),
        bias_new=bias_new,
        bias_win=jnp.concatenate([_rows_by_query(bdw, n_q, WIN, WIN).reshape(H * n_q, WIN), bias_new], axis=1))


def _nsa_layer(yp, ys, g_pre, g_post, w_in, w_out, cmp_pos, cmp_w1, cmp_w2, tb, pool, page_table, win_states,
               li, b, s, db, n_q, past):
    w_pad = jnp.pad(w_in, ((0, 0), (0, NSA_PAD - w_in.shape[1]))).astype(bf16)
    w_out = w_out.astype(bf16)
    w1cat = jnp.concatenate([cmp_w1[:, :1024], cmp_w1[:, 1024:]], axis=-1).astype(bf16)
    pos2 = cmp_pos.reshape(2, 2, 1024)
    w2 = cmp_w2.astype(bf16)

    q_hm, q2_hm, kv4, kwin, ks, vs, kw, vw, gates = _nsa_proj_prompt(yp, g_pre, w_pad, b, s)
    kc, vc = _compress_prompt(kv4, w1cat, pos2, w2, b, s)
    o_cmp, selneg = _cmp_sel_prompt(q_hm, kc, vc, gates, b, s)
    o_sw = _flash_prompt(q2_hm, ks, vs, kw, vw, tb["bias_s"], tb["bias_w"], selneg, tb["et_tiles"], gates, b, s)
    yp = _out_proj(yp, g_post, w_out, o_cmp.reshape(b * s, ATT), o_sw.reshape(b * s, ATT))
    kv_p = kv4.reshape(b, s, 4, G, DH)
    win_p = kwin.reshape(b, s, 2, G, DH)[:, s - min(WIN, s):]

    z = _proj(ys, g_pre, w_pad)
    q = (z[:, :ATT] * (DH ** -0.5)).reshape(db, n_q, H, DH).transpose(0, 2, 1, 3)
    qg = q.reshape(db, G, R * n_q, DH).astype(bf16)
    eye = jnp.eye(G, dtype=f32)
    qbd = (q.reshape(db, G, R * n_q, 1, DH) * eye[None, :, None, :, None]).reshape(db, H * n_q, G * DH).astype(bf16)
    kv_new = z[:, ATT:ATT + 1024].reshape(db, n_q, 1024)
    win_new = z[:, ATT + 1024:ATT + 1536].reshape(db, n_q, 512)
    gt = z[:, ATT + 1536:ATT + 1536 + 3 * H].reshape(db, n_q, 3, H).transpose(0, 3, 1, 2)
    gsw = jnp.pad(gt.reshape(db, H * n_q, 3), ((0, 0), (0, 0), (0, 125)))
    kc_s, vc_s = _compress_sample(pool, page_table, w1cat, pos2, w2)
    o_cmp_s, negmask = _cmp_sel_sample(qg, kc_s, vc_s, gsw, tb["e_full"], past)
    o_sw_s = _flash_sample(pool, page_table, qbd, tb["bias_slc"], negmask, kv_new[:, :, 512:], tb["bias_new"],
                           win_states.reshape(win_states.shape[0], db, WIN, 512), li, win_new, tb["bias_win"], gsw)
    ys = _out_proj(ys, g_post, w_out, o_cmp_s.reshape(db * n_q, ATT), o_sw_s.reshape(db * n_q, ATT))
    kv_s = kv_new.reshape(db, n_q, 4, G, DH)
    win_s = jnp.concatenate([win_states[li, :, n_q:], win_new.reshape(db, n_q, 2, G, DH)], axis=1)
    return yp, ys, kv_p, win_p, kv_s, win_s


def _dil_tables(rel_bias, n_q):
    sub = jnp.arange(2 * DT)
    tri = jnp.arange(n_q)[:, None] - jnp.arange(PAGE)[None, :]
    tiles, biases = [], []
    for gi, (win, dil) in enumerate(DIL):
        bd = rel_bias[_rel_bucket(sub * dil)].T[gi * 4:(gi + 1) * 4]
        tiles.append(_toeplitz_tiles(jnp.where(sub * dil <= win, bd, NEG), 2, DT))
        dist = jnp.arange(win + n_q)
        bd = rel_bias[_rel_bucket(dist)].T[gi * 4:(gi + 1) * 4]
        bd = jnp.where((dist % dil == 0) & (dist <= win), bd, NEG)
        b_new = jnp.where(tri >= 0, bd[:, jnp.maximum(tri, 0)], NEG)
        b_all = jnp.concatenate([_rows_by_query(bd, n_q, win, win), b_new], axis=2)
        biases.append(b_all.reshape(4 * n_q, win + PAGE))
    return dict(tiles=jnp.stack(tiles).transpose(0, 2, 1, 3, 4).reshape(3, 2, 2, 2, DT, DT), sample=biases)


def _dil_layer(yp, ys, g_pre, g_post, w_in, w_out, tb, states, li, b, s, db, n_q):
    w_in = w_in.astype(bf16)
    w_out = w_out.astype(bf16)

    zd = _dil_proj(yp, g_pre, w_in)
    y = _dil_prompt(zd.reshape(b, s, 3 * ATT), tb["tiles"], b, s)
    yp = _out_proj(yp, g_post, w_out, y.reshape(b * s, ATT))
    z5 = zd.reshape(b, s, 3, H, DH)
    st_p = []
    for gi, (win, _) in enumerate(DIL):
        keep = min(win, s)
        st_p.append(jnp.stack([z5[:, s - keep:, 1, gi * 4:(gi + 1) * 4], z5[:, s - keep:, 2, gi * 4:(gi + 1) * 4]], axis=2))

    z = _proj(ys, g_pre, w_in).reshape(db, n_q, 3, 3, 4, DH)
    q = (z[:, :, 0] * (DH ** -0.5)).transpose(0, 2, 3, 1, 4)
    eye = jnp.eye(4, dtype=f32)
    qbd = (q[:, :, :, :, None, :] * eye[None, None, :, None, :, None]).reshape(db, 3, 4 * n_q, 256).astype(bf16)
    new_kv = jnp.concatenate([z[:, :, 1].transpose(0, 2, 1, 3, 4).reshape(db, 3, n_q, 256),
                              z[:, :, 2].transpose(0, 2, 1, 3, 4).reshape(db, 3, n_q, 256)], axis=-1)
    flat = [st.reshape(st.shape[0], db, st.shape[2], 512) for st in states]
    y_s, n0, n1, n2 = _dil_sample(qbd, flat, li, new_kv, tb["sample"])
    ys = _out_proj(ys, g_post, w_out, y_s.reshape(db * n_q, ATT))
    st_s = [n.reshape(st.shape[1:]) for n, st in zip((n0, n1, n2), states)]
    return yp, ys, st_p, st_s


def kernel(x_prompt, x_sample, cache_nsa_kv, page_table, state_nsa_win, state_dil_w128, state_dil_w512,
           state_dil_w2048, norm_g, ffn_w_in, ffn_w_out, nsa_w_in, nsa_w_out, nsa_cmp_pos, nsa_cmp_w1,
           nsa_cmp_w2, dil_w_in, dil_w_out, rel_bias):
    b, s, _ = x_prompt.shape
    db, n_q, _ = x_sample.shape
    depth = norm_g.shape[0]
    past = page_table.shape[1] * PAGE
    yp = x_prompt.reshape(b * s, D)
    ys = x_sample.reshape(db * n_q, D)
    nsa_kv_p, nsa_win_p, nsa_kv_s, nsa_win_s = [], [], [], []
    dil_p, dil_s = [[], [], []], [[], [], []]
    nsa_tb = _nsa_tables(rel_bias, s, n_q, past)
    dil_tb = _dil_tables(rel_bias, n_q)
    for i in range(depth):
        li = i // 2
        g = norm_g[i]
        w_i, w_o = ffn_w_in[i, 0].astype(bf16), ffn_w_out[i, 0].astype(bf16)
        yp = _half_ffn(yp, g[0:2], w_i, w_o)
        ys = _half_ffn(ys, g[0:2], w_i, w_o)
        if i % 2 == 0:
            n_pool = cache_nsa_kv.shape[1]
            pool = cache_nsa_kv.reshape(cache_nsa_kv.shape[0] * n_pool, PAGE, 1024)
            yp, ys, kvp, winp, kvs, wins = _nsa_layer(
                yp, ys, g[2:3], g[3:4], nsa_w_in[li], nsa_w_out[li], nsa_cmp_pos[li], nsa_cmp_w1[li],
                nsa_cmp_w2[li], nsa_tb, pool, page_table + li * n_pool, state_nsa_win, li, b, s, db, n_q, past)
            nsa_kv_p.append(kvp)
            nsa_win_p.append(winp)
            nsa_kv_s.append(kvs)
            nsa_win_s.append(wins)
        else:
            states = (state_dil_w128, state_dil_w512, state_dil_w2048)
            yp, ys, stp, sts = _dil_layer(yp, ys, g[2:3], g[3:4], dil_w_in[li], dil_w_out[li], dil_tb, states,
                                          li, b, s, db, n_q)
            for gi in range(3):
                dil_p[gi].append(stp[gi])
                dil_s[gi].append(sts[gi])
        w_i, w_o = ffn_w_in[i, 1].astype(bf16), ffn_w_out[i, 1].astype(bf16)
        yp = _half_ffn(yp, g[4:6], w_i, w_o)
        ys = _half_ffn(ys, g[4:6], w_i, w_o)
    return (yp.reshape(b, s, D), ys.reshape(db, n_q, D),
            jnp.stack(nsa_kv_p), jnp.stack(nsa_win_p),
            jnp.stack(dil_p[0]), jnp.stack(dil_p[1]), jnp.stack(dil_p[2]),
            jnp.stack(nsa_kv_s), jnp.stack(nsa_win_s),
            jnp.stack(dil_s[0]), jnp.stack(dil_s[1]), jnp.stack(dil_s[2]))
```

```python
import functools
import math

import jax
import jax.numpy as jnp
from jax import lax
from jax.experimental import pallas as pl
from jax.experimental.pallas import tpu as pltpu

f32, bf16 = jnp.float32, jnp.bfloat16
SDS = jax.ShapeDtypeStruct

D = 1024
FF = 2816
FF_CHUNK = 256
H, DH = 12, 64
G, R = 4, 3
ATT = H * DH
EPS = 1e-6
NEG = -1e30
LOG2E = math.log2(math.e)
FORCE = 1e4
SEL_BLOCK, TOPK = 64, 16
WIN = 512
NSA_PAD = 2432
TQ = 256
RC = 32
PAGE = 128
CHUNK_PAGES = 8
CMP_GROUP = 4
DIL = ((128, 1), (512, 4), (2048, 16))
DT = 128
DIL_UNROLL = 4
MIB = 1024 * 1024


def _cp(sem, vmem_mib):
    return pltpu.CompilerParams(dimension_semantics=sem, vmem_limit_bytes=vmem_mib * MIB)


def _rms(x, g):
    return x * lax.rsqrt(jnp.mean(x * x, axis=-1, keepdims=True) + EPS) * g


def _nt(a, b):
    return lax.dot_general(a, b, (((1,), (1,)), ((), ())), preferred_element_type=f32)


def _dot(a, b):
    return jnp.dot(a, b, preferred_element_type=f32)


def _resident(shape):
    nd = len(shape)
    return pl.BlockSpec(shape, lambda *_: (0,) * nd, pipeline_mode=pl.Buffered(1))


def _ffn_body(x_ref, g_ref, win_ref, wout_ref, o_ref):
    x = x_ref[...]
    h = _rms(x, g_ref[0:1, :]).astype(bf16)
    acc = jnp.zeros(x.shape, f32)
    for c in range(FF // FF_CHUNK):
        lo, hi = c * FF_CHUNK, (c + 1) * FF_CHUNK
        gate = _dot(h, win_ref[:, lo:hi])
        up = _dot(h, win_ref[:, FF + lo:FF + hi])
        act = (gate * jax.nn.sigmoid(gate) * up).astype(bf16)
        acc = acc + _dot(act, wout_ref[lo:hi, :])
    o_ref[...] = x + 0.5 * _rms(acc, g_ref[1:2, :])


def _half_ffn(x, g2, w_in, w_out):
    m = x.shape[0]
    tm = min(m, 512)
    return pl.pallas_call(
        _ffn_body, out_shape=SDS((m, D), f32), grid=(m // tm,),
        in_specs=[pl.BlockSpec((tm, D), lambda i: (i, 0)), _resident((2, D)),
                  _resident((D, 2 * FF)), _resident((FF, D))],
        out_specs=pl.BlockSpec((tm, D), lambda i: (i, 0)),
        compiler_params=_cp(("parallel",), 48), name="half_ffn")(x, g2, w_in, w_out)


def _proj_body(x_ref, g_ref, w_ref, o_ref):
    h = _rms(x_ref[...], g_ref[...]).astype(bf16)
    o_ref[...] = _dot(h, w_ref[...])


def _proj(x, g, w):
    m, n = x.shape[0], w.shape[1]
    return pl.pallas_call(
        _proj_body, out_shape=SDS((m, n), f32), grid=(1,),
        in_specs=[_resident((m, D)), _resident((1, D)), _resident((D, n))],
        out_specs=pl.BlockSpec((m, n), lambda i: (0, 0)),
        compiler_params=_cp(("arbitrary",), 32), name="proj_sample")(x, g, w)


def _nsa_proj_body(x_ref, g_ref, w_ref, q_ref, q2_ref, kv_ref, kwin_ref, ks_ref, vs_ref, kw_ref, vw_ref, gt_ref):
    h = _rms(x_ref[...], g_ref[...]).astype(bf16)
    zq = _dot(h, w_ref[:, 0:ATT]) * (DH ** -0.5)
    zq2 = zq * LOG2E
    for hh in range(H):
        q_ref[0, hh] = zq[:, hh * DH:(hh + 1) * DH].astype(bf16)
        q2_ref[0, hh] = zq2[:, hh * DH:(hh + 1) * DH].astype(bf16)
    zc = _dot(h, w_ref[:, ATT:ATT + 512])
    kv_ref[:, 0:512] = zc
    zs = _dot(h, w_ref[:, ATT + 512:ATT + 1024])
    kv_ref[:, 512:1024] = zs
    zw = _dot(h, w_ref[:, ATT + 1024:ATT + 1536])
    kwin_ref[...] = zw
    ones = jnp.ones((zs.shape[0], DH), bf16)
    for g in range(G):
        ks_ref[0, g] = zs[:, g * DH:(g + 1) * DH].astype(bf16)
        vs_ref[0, g] = jnp.concatenate([zs[:, 256 + g * DH:256 + (g + 1) * DH].astype(bf16), ones], axis=1)
        kw_ref[0, g] = zw[:, g * DH:(g + 1) * DH].astype(bf16)
        vw_ref[0, g] = jnp.concatenate([zw[:, 256 + g * DH:256 + (g + 1) * DH].astype(bf16), ones], axis=1)
    gt_ref[...] = _dot(h, w_ref[:, ATT + 1536:NSA_PAD])


def _nsa_proj_prompt(x, g, w, b, s):
    tm = 512
    nt = s // tm
    hm = lambda n, w_: pl.BlockSpec((1, n, tm, w_), lambda i: (i // nt, 0, i % nt, 0))
    rows = lambda n: pl.BlockSpec((tm, n), lambda i: (i, 0))
    return pl.pallas_call(
        _nsa_proj_body,
        out_shape=(SDS((b, H, s, DH), bf16), SDS((b, H, s, DH), bf16), SDS((b * s, 1024), f32),
                   SDS((b * s, 512), f32), SDS((b, G, s, DH), bf16), SDS((b, G, s, 2 * DH), bf16),
                   SDS((b, G, s, DH), bf16), SDS((b, G, s, 2 * DH), bf16), SDS((b * s, 128), f32)),
        grid=(b * s // tm,),
        in_specs=[rows(D), _resident((1, D)), _resident((D, NSA_PAD))],
        out_specs=(hm(H, DH), hm(H, DH), rows(1024), rows(512), hm(G, DH), hm(G, 2 * DH), hm(G, DH),
                   hm(G, 2 * DH), rows(128)),
        compiler_params=_cp(("parallel",), 48), name="nsa_proj_prompt")(x, g, w)


def _dil_proj_body(x_ref, g_ref, w_ref, o_ref):
    h = _rms(x_ref[...], g_ref[...]).astype(bf16)
    o_ref[:, 0:ATT] = _dot(h, w_ref[:, 0:ATT]) * (DH ** -0.5)
    o_ref[:, ATT:2 * ATT] = _dot(h, w_ref[:, ATT:2 * ATT])
    o_ref[:, 2 * ATT:3 * ATT] = _dot(h, w_ref[:, 2 * ATT:3 * ATT])


def _dil_proj(x, g, w):
    m = x.shape[0]
    tm = min(m, 512)
    return pl.pallas_call(
        _dil_proj_body, out_shape=SDS((m, 3 * ATT), f32), grid=(m // tm,),
        in_specs=[pl.BlockSpec((tm, D), lambda i: (i, 0)), _resident((1, D)), _resident((D, 3 * ATT))],
        out_specs=pl.BlockSpec((tm, 3 * ATT), lambda i: (i, 0)),
        compiler_params=_cp(("parallel",), 48), name="dil_proj")(x, g, w)


def _out_proj_body(x_ref, g_ref, w_ref, *refs):
    o_ref = refs[-1]
    o = refs[0][...]
    for r in refs[1:-1]:
        o = o + r[...]
    y = _dot(o.astype(bf16), w_ref[...])
    o_ref[...] = x_ref[...] + _rms(y, g_ref[...])


def _out_proj(x, g, w, *branches):
    m = x.shape[0]
    tm = min(m, 512)
    row = lambda n: pl.BlockSpec((tm, n), lambda i: (i, 0))
    return pl.pallas_call(
        _out_proj_body, out_shape=SDS((m, D), f32), grid=(m // tm,),
        in_specs=[row(D), _resident((1, D)), _resident((ATT, D))] + [row(ATT)] * len(branches),
        out_specs=row(D), compiler_params=_cp(("parallel",), 32), name="out_proj")(x, g, w, *branches)


def _compress_body(*refs, n_pages, rows, paged, group=1):
    if paged:
        refs = refs[1:]
    pages = refs[:n_pages]
    w1_ref, pos_ref, w2_ref, kc_ref, vc_ref, cs_ref, lhs_ref, aprev_ref = refs[n_pages:]
    nch = n_pages * rows // 16
    gn = group * nch
    step = pl.program_id(1) if paged else 0
    sub = step % group if paged else 0

    for p in range(n_pages):
        for c in range(4):
            if paged:
                cs_ref[c, p * rows:(p + 1) * rows, :] = pages[p][0, 2 * c:2 * c + 2].reshape(128, 128).T
            else:
                cs_ref[c, p * rows:(p + 1) * rows, :] = pages[p][0, :, c * 128:(c + 1) * 128]

    lane = lax.broadcasted_iota(jnp.int32, (nch, 128), 1)
    low = lane < DH
    for kind in range(2):
        for pair in range(2):
            c = kind * 2 + pair
            for half in range(2):
                start = (2 * pair + half) * gn + sub * nch
                dst = pl.ds(pl.multiple_of(start, nch), nch) if paged else pl.ds(start, nch)
                for r2 in range(8):
                    va = cs_ref[c, pl.ds(2 * r2, nch, stride=16), :]
                    vb = cs_ref[c, pl.ds(2 * r2 + 1, nch, stride=16), :]
                    piece = (jnp.where(low, va, pltpu.roll(vb, DH, 1)) if half == 0
                             else jnp.where(low, pltpu.roll(va, DH, 1), vb))
                    lhs_ref[kind, dst, r2 * 128:(r2 + 1) * 128] = piece.astype(bf16)

    def mlp():
        first = (step < group) if paged else True
        if paged:
            @pl.when(step == group - 1)
            def _():
                aprev_ref[...] = jnp.zeros(aprev_ref.shape, f32)

        row = lax.broadcasted_iota(jnp.int32, (gn, 256), 0)
        row_o = lax.broadcasted_iota(jnp.int32, (gn, DH), 0)
        for kind, out_ref in ((0, kc_ref), (1, vc_ref)):
            pp = _dot(lhs_ref[kind], w1_ref[kind])
            pos = _dot(pos_ref[kind].astype(bf16), w1_ref[kind])
            posterm = pos[0:1, 0:256] + pos[1:2, 256:512]
            for g in range(G):
                a = pp[g * gn:(g + 1) * gn, 0:256]
                bm = pp[g * gn:(g + 1) * gn, 256:512]
                shifted = pltpu.roll(a, 1, 0)
                if paged:
                    carry = aprev_ref[kind, (g + 1) * gn - 1:(g + 1) * gn, :]
                    shifted = jnp.where(row == 0, carry, shifted)
                    aprev_ref[kind, g * gn:(g + 1) * gn, :] = a
                pre = shifted + bm + posterm
                hid = (pre * jax.nn.sigmoid(pre)).astype(bf16)
                out = _dot(hid, w2_ref[kind])
                out = jnp.where((row_o == 0) & first, 0.0, out)
                out_ref[0, g] = out.astype(bf16)

    if paged:
        pl.when(sub == group - 1)(mlp)
    else:
        mlp()


def _compress_prompt(kv4, w1cat, pos2, w2, b, s):
    nch = s // 16
    out = SDS((b, G, nch, DH), bf16)
    ospec = pl.BlockSpec((1, G, nch, DH), lambda i: (i, 0, 0, 0))
    return pl.pallas_call(
        functools.partial(_compress_body, n_pages=1, rows=s, paged=False),
        out_shape=(out, out), grid=(b,),
        in_specs=[pl.BlockSpec((1, s, 512), lambda i: (i, 0, 0)),
                  _resident((2, 1024, 512)), _resident((2, 2, 1024)), _resident((2, 256, DH))],
        out_specs=(ospec, ospec),
        scratch_shapes=[pltpu.VMEM((4, s, 128), f32), pltpu.VMEM((2, G * nch, 1024), bf16),
                        pltpu.VMEM((2, G * nch, 256), f32)],
        compiler_params=_cp(("parallel",), 48), name="compress_prompt")(kv4.reshape(b, s, 1024), w1cat, pos2, w2)


def _compress_sample(pool_t, page_table, w1cat, pos2, w2):
    db, n_pages = page_table.shape
    steps = n_pages // CHUNK_PAGES
    gn = CMP_GROUP * CHUNK_PAGES * PAGE // 16
    out = SDS((db, G, n_pages * PAGE // 16, DH), bf16)
    ospec = pl.BlockSpec((1, G, gn, DH), lambda i, j, pt: (i, 0, j // CMP_GROUP, 0))
    const = lambda shape: pl.BlockSpec(shape, lambda i, j, pt: (0,) * len(shape), pipeline_mode=pl.Buffered(1))
    page_specs = [pl.BlockSpec((1, 8, DH, PAGE), lambda i, j, pt, k=k: (pt[i, j * CHUNK_PAGES + k], 0, 0, 0))
                  for k in range(CHUNK_PAGES)]
    return pl.pallas_call(
        functools.partial(_compress_body, n_pages=CHUNK_PAGES, rows=PAGE, paged=True, group=CMP_GROUP),
        out_shape=(out, out),
        grid_spec=pltpu.PrefetchScalarGridSpec(
            num_scalar_prefetch=1, grid=(db, steps),
            in_specs=page_specs + [const((2, 1024, 512)), const((2, 2, 1024)), const((2, 256, DH))],
            out_specs=(ospec, ospec),
            scratch_shapes=[pltpu.VMEM((4, CHUNK_PAGES * PAGE, 128), f32), pltpu.VMEM((2, G * gn, 1024), bf16),
                            pltpu.VMEM((2, G * gn, 256), f32)]),
        compiler_params=_cp(("arbitrary", "arbitrary"), 40), name="compress_sample")(
            page_table, *([pool_t] * CHUNK_PAGES), w1cat, pos2, w2)


def _block_scores(impt_ref, n_blocks):
    return (impt_ref[pl.ds(1, n_blocks, stride=4), :] + impt_ref[pl.ds(2, n_blocks, stride=4), :]
            + impt_ref[pl.ds(3, n_blocks, stride=4), :])


def _force_scores(v, q_pos):
    j = lax.broadcasted_iota(jnp.int32, v.shape, 0)
    cur = q_pos >> 6
    forced = (j == 0) | (j == cur) | (j == cur - 1)
    return jnp.where(forced, FORCE, jnp.where(j > cur, -FORCE, v)), j


def _cmp_sel_prompt_body(q_ref, kc_ref, vc_ref, gt_ref, o_ref, sel_ref, impt_ref, *, n_sel):
    qt = pl.program_id(1)
    sig = jax.nn.sigmoid(gt_ref[...])
    t = qt * TQ + lax.broadcasted_iota(jnp.int32, (TQ, 128), 0)
    mp = lax.broadcasted_iota(jnp.int32, (TQ, 128), 1)
    valid = (mp >= 1) & (16 * mp + 15 <= t)
    for g in range(G):
        kc, vc = kc_ref[0, g], vc_ref[0, g]
        imp = jnp.zeros((TQ, 128), f32)
        for r in range(R):
            hh = g * R + r
            s = jnp.where(valid, _nt(q_ref[0, hh], kc), NEG)
            m = jnp.max(s, axis=-1, keepdims=True)
            e = jnp.where(valid, jnp.exp(s - m), 0.0)
            p = e / jnp.maximum(jnp.sum(e, axis=-1, keepdims=True), 1e-30)
            imp = imp + p
            o_ref[0, :, hh * DH:(hh + 1) * DH] = _dot(p.astype(bf16), vc) * sig[:, hh:hh + 1]
        for half in range(TQ // 128):
            impt_ref[...] = imp[half * 128:(half + 1) * 128, :].T
            v = _block_scores(impt_ref, n_sel)
            q_pos = qt * TQ + half * 128 + lax.broadcasted_iota(jnp.int32, v.shape, 1)
            val, j = _force_scores(v, q_pos)
            rank = jnp.zeros(v.shape, jnp.int32)
            for k in range(n_sel):
                vk = val[k:k + 1, :]
                rank = rank + jnp.where((vk > val) | ((vk == val) & (j > k)), 1, 0)
            neg = jnp.where(rank < TOPK, 0.0, NEG)
            full = jnp.concatenate([neg, jnp.zeros((128 - n_sel, 128), f32)], axis=0)
            sel_ref[0, g, half * 128:(half + 1) * 128, :] = full.T.astype(bf16)


def _cmp_sel_prompt(q_hm, kc, vc, gates, b, s):
    nq = s // TQ
    return pl.pallas_call(
        functools.partial(_cmp_sel_prompt_body, n_sel=s // SEL_BLOCK),
        out_shape=(SDS((b, s, ATT), f32), SDS((b, G, s, 128), bf16)), grid=(b, nq),
        in_specs=[pl.BlockSpec((1, H, TQ, DH), lambda i, j: (i, 0, j, 0)),
                  pl.BlockSpec((1, G, 128, DH), lambda i, j: (i, 0, 0, 0)),
                  pl.BlockSpec((1, G, 128, DH), lambda i, j: (i, 0, 0, 0)),
                  pl.BlockSpec((TQ, 128), lambda i, j: (i * nq + j, 0))],
        out_specs=(pl.BlockSpec((1, TQ, ATT), lambda i, j: (i, j, 0)),
                   pl.BlockSpec((1, G, TQ, 128), lambda i, j: (i, 0, j, 0))),
        scratch_shapes=[pltpu.VMEM((128, 128), f32)],
        compiler_params=_cp(("parallel", "parallel"), 32), name="cmp_sel_prompt")(q_hm, kc, vc, gates)


def _online_update(s, v, m_ref, l_ref, acc_ref, idx):
    m_old = m_ref[idx]
    m_new = jnp.maximum(m_old, jnp.max(s, axis=-1, keepdims=True))
    alpha = jnp.exp(m_old - m_new)
    p = jnp.exp(s - m_new)
    l_ref[idx] = alpha * l_ref[idx] + jnp.sum(p, axis=-1, keepdims=True)
    acc_ref[idx] = alpha * acc_ref[idx] + _dot(p.astype(bf16), v)
    m_ref[idx] = m_new


def _group_update(qa, ka, va, bias_ref, delta, h0, m_ref, acc_ref, idx0, s_ref, p_ref, a_ref):
    s_ref[...] = _nt(qa, ka)
    for r in range(R):
        for c in range(TQ // RC):
            lo = c * RC
            s = s_ref[r * TQ + lo:r * TQ + lo + RC, :] + bias_ref[delta, h0 + r, lo:lo + RC, :]
            m_old = m_ref[idx0 + r, lo:lo + RC, :]
            m_new = jnp.maximum(m_old, jnp.max(s, axis=-1, keepdims=True))
            p_ref[r * TQ + lo:r * TQ + lo + RC, :] = jnp.exp2(
                s - jnp.concatenate([m_new] * (TQ // 128), axis=1)).astype(bf16)
            m_ref[idx0 + r, lo:lo + RC, :] = m_new
            a_ref[r * TQ + lo:r * TQ + lo + RC, :] = jnp.exp2(m_old - m_new)
    pv = _dot(p_ref[...], va)
    for r in range(R):
        acc_ref[idx0 + r] = a_ref[r * TQ:(r + 1) * TQ, :] * acc_ref[idx0 + r] + pv[r * TQ:(r + 1) * TQ, :]


def _flash_prompt_body(q_ref, ks_ref, vs_ref, kw_ref, vw_ref, bs_ref, bw_ref, sel_ref, et_ref, gt_ref, o_ref,
                       m_ref, acc_ref, s_ref, p_ref, a_ref):
    qt, kt = pl.program_id(1), pl.program_id(2)
    delta = qt - kt
    scratch = (s_ref, p_ref, a_ref)

    @pl.when(kt == 0)
    def _():
        m_ref[...] = jnp.full(m_ref.shape, NEG, f32)
        acc_ref[...] = jnp.zeros(acc_ref.shape, f32)

    @pl.when(kt <= qt)
    def _():
        for g in range(G):
            q3 = q_ref[0, g * R:(g + 1) * R].reshape(R * TQ, DH)
            qa = jnp.concatenate([jnp.concatenate([sel_ref[0, g]] * R, axis=0), q3], axis=1)
            ka = jnp.concatenate([et_ref[0], ks_ref[0, g]], axis=1)
            _group_update(qa, ka, vs_ref[0, g], bs_ref, delta, g * R, m_ref, acc_ref, g * R, *scratch)

    @pl.when((kt <= qt) & (kt >= qt - (WIN // TQ)))
    def _():
        for g in range(G):
            q3 = q_ref[0, g * R:(g + 1) * R].reshape(R * TQ, DH)
            _group_update(q3, kw_ref[0, g], vw_ref[0, g], bw_ref, delta, g * R, m_ref, acc_ref, H + g * R, *scratch)

    @pl.when(kt == qt)
    def _():
        sig = jax.nn.sigmoid(gt_ref[...])
        for hh in range(H):
            a_s, a_w = acc_ref[hh], acc_ref[H + hh]
            o_s = a_s / jnp.maximum(pltpu.roll(a_s, DH, 1), 1e-30)
            o_w = a_w / jnp.maximum(pltpu.roll(a_w, DH, 1), 1e-30)
            o = sig[:, H + hh:H + hh + 1] * o_s + sig[:, 2 * H + hh:2 * H + hh + 1] * o_w
            o_ref[0, :, hh * DH:(hh + 1) * DH] = o[:, 0:DH]


def _flash_prompt(q2, ks, vs, kw, vw, bias_s, bias_w, selneg, et_tiles, gates, b, s):
    nq = s // TQ
    nw = WIN // TQ
    kv = lambda w_, f: pl.BlockSpec((1, G, TQ, w_), lambda i, j, k: (i, 0, f(j, k), 0))
    sel_kt = lambda j, k: jnp.minimum(k, j)
    win_kt = lambda j, k: jnp.clip(k, jnp.maximum(j - nw, 0), j)
    return pl.pallas_call(
        _flash_prompt_body, out_shape=SDS((b, s, ATT), f32), grid=(b, nq, nq),
        in_specs=[pl.BlockSpec((1, H, TQ, DH), lambda i, j, k: (i, 0, j, 0)),
                  kv(DH, sel_kt), kv(2 * DH, sel_kt), kv(DH, win_kt), kv(2 * DH, win_kt),
                  _resident(bias_s.shape), _resident(bias_w.shape),
                  pl.BlockSpec((1, G, TQ, 128), lambda i, j, k: (i, 0, j, 0)),
                  pl.BlockSpec((1, TQ, 128), lambda i, j, k: (jnp.minimum(k, j), 0, 0)),
                  pl.BlockSpec((TQ, 128), lambda i, j, k: (i * nq + j, 0))],
        out_specs=pl.BlockSpec((1, TQ, ATT), lambda i, j, k: (i, j, 0)),
        scratch_shapes=[pltpu.VMEM((2 * H, TQ, 128), f32), pltpu.VMEM((2 * H, TQ, 128), f32),
                        pltpu.VMEM((R * TQ, TQ), f32), pltpu.VMEM((R * TQ, TQ), bf16), pltpu.VMEM((R * TQ, 128), f32)],
        compiler_params=_cp(("parallel", "parallel", "arbitrary"), 58), name="flash_prompt")(
            q2, ks, vs, kw, vw, bias_s, bias_w, selneg, et_tiles, gates)


def _cmp_sel_sample_body(q_ref, kc_ref, vc_ref, gt_ref, e_ref, o_ref, nm_ref, impt_ref, val_ref, *, past, n_q):
    n_cmp = kc_ref.shape[2]
    n_sel = n_cmp // 4
    sig = jax.nn.sigmoid(gt_ref[0])
    rows = R * n_q
    mp = lax.broadcasted_iota(jnp.int32, (rows, n_cmp), 1)
    i_q = lax.broadcasted_iota(jnp.int32, (rows, n_cmp), 0) % n_q
    valid = (mp >= 1) & (16 * mp + 15 <= past + i_q)
    imps = []
    for g in range(G):
        s = jnp.where(valid, _nt(q_ref[0, g], kc_ref[0, g]), NEG)
        m = jnp.max(s, axis=-1, keepdims=True)
        e = jnp.where(valid, jnp.exp(s - m), 0.0)
        p = e / jnp.maximum(jnp.sum(e, axis=-1, keepdims=True), 1e-30)
        o = _dot(p.astype(bf16), vc_ref[0, g]) * sig[g * rows:(g + 1) * rows, 0:1]
        for r in range(R):
            hh = g * R + r
            o_ref[0, :, hh * DH:(hh + 1) * DH] = o[r * n_q:(r + 1) * n_q, :]
        imps.append(p[0:n_q] + p[n_q:2 * n_q] + p[2 * n_q:3 * n_q])
    imp = jnp.concatenate(imps + [jnp.zeros((128 - G * n_q, n_cmp), f32)], axis=0)
    impt_ref[...] = imp.T
    v = _block_scores(impt_ref, n_sel)
    q_pos = past + lax.broadcasted_iota(jnp.int32, v.shape, 1) % n_q
    val, j = _force_scores(v, q_pos)
    val_ref[...] = val

    def count(k, rank):
        vk = val_ref[pl.ds(k, 1), :]
        return rank + jnp.where((vk > val) | ((vk == val) & (j > k)), 1, 0)

    rank = lax.fori_loop(0, n_sel, count, jnp.zeros(v.shape, jnp.int32))
    rank = rank + jnp.where(FORCE > val, 1, 0)
    neg = jnp.where(rank < TOPK, 0.0, NEG).T
    nm_ref[0] = _dot(neg[0:G * n_q, :].astype(bf16), e_ref[...])


def _cmp_sel_sample(qg, kc, vc, gsw, e_full, past):
    db, n_q = qg.shape[0], qg.shape[2] // R
    n_cmp = kc.shape[2]
    return pl.pallas_call(
        functools.partial(_cmp_sel_sample_body, past=past, n_q=n_q),
        out_shape=(SDS((db, n_q, ATT), f32), SDS((db, G * n_q, past), f32)), grid=(db,),
        in_specs=[pl.BlockSpec((1, G, R * n_q, DH), lambda i: (i, 0, 0, 0)),
                  pl.BlockSpec((1, G, n_cmp, DH), lambda i: (i, 0, 0, 0)),
                  pl.BlockSpec((1, G, n_cmp, DH), lambda i: (i, 0, 0, 0)),
                  pl.BlockSpec((1, H * n_q, 128), lambda i: (i, 0, 0)),
                  _resident((n_cmp // 4, past))],
        out_specs=(pl.BlockSpec((1, n_q, ATT), lambda i: (i, 0, 0)),
                   pl.BlockSpec((1, G * n_q, past), lambda i: (i, 0, 0))),
        scratch_shapes=[pltpu.VMEM((n_cmp, 128), f32), pltpu.VMEM((n_cmp // 4, 128), f32)],
        compiler_params=_cp(("parallel",), 32), name="cmp_sel_sample")(qg, kc, vc, gsw, e_full)


def _flash_sample_body(*refs, n_q):
    refs = refs[1:]
    pages = refs[:CHUNK_PAGES]
    (q_ref, bias_ref, nm_ref, new_ref, bnew_ref, win_ref, wnew_ref, bwin_ref, gt_ref, o_ref,
     m_ref, l_ref, acc_ref, pad_ref, wall_ref) = refs[CHUNK_PAGES:]
    step = pl.program_id(1)
    q = q_ref[0]

    @pl.when(step == 0)
    def _():
        m_ref[...] = jnp.full(m_ref.shape, NEG, f32)
        l_ref[...] = jnp.zeros(l_ref.shape, f32)
        acc_ref[...] = jnp.zeros(acc_ref.shape, f32)

    nm = nm_ref[0]
    nm = jnp.concatenate([nm[g * n_q:(g + 1) * n_q] for g in range(G) for _ in range(R)], axis=0)
    s = jnp.concatenate([_dot(q, pages[p][0, 0:G].reshape(G * DH, PAGE).astype(bf16)) for p in range(CHUNK_PAGES)],
                        axis=1)
    s = s + bias_ref[...] + nm
    m_old = m_ref[0]
    m_new = jnp.maximum(m_old, jnp.max(s, axis=-1, keepdims=True))
    alpha = jnp.exp(m_old - m_new)
    p = jnp.exp(s - m_new)
    l_ref[0] = alpha * l_ref[0] + jnp.sum(p, axis=-1, keepdims=True)
    p = p.astype(bf16)
    pv = acc_ref[0] * alpha
    for pg in range(CHUNK_PAGES):
        pv = pv + _nt(p[:, pg * PAGE:(pg + 1) * PAGE], pages[pg][0, G:2 * G].reshape(G * DH, PAGE).astype(bf16))
    acc_ref[0] = pv
    m_ref[0] = m_new

    @pl.when(step == pl.num_programs(1) - 1)
    def _():
        pad_ref[...] = jnp.zeros(pad_ref.shape, f32)
        pad_ref[0:n_q, :] = new_ref[0]
        _online_update(_nt(q, pad_ref[:, 0:256].astype(bf16)) + bnew_ref[...], pad_ref[:, 256:512].astype(bf16),
                       m_ref, l_ref, acc_ref, 0)
        wall_ref[...] = jnp.zeros(wall_ref.shape, f32)
        wall_ref[0:WIN, :] = win_ref[0, 0]
        wall_ref[WIN:WIN + n_q, :] = wnew_ref[0]
        _online_update(_nt(q, wall_ref[:, 0:256].astype(bf16)) + bwin_ref[...], wall_ref[:, 256:512].astype(bf16),
                       m_ref, l_ref, acc_ref, 1)
        sig = jax.nn.sigmoid(gt_ref[0])
        res = (sig[:, 1:2] * acc_ref[0] / jnp.maximum(l_ref[0], 1e-30)
               + sig[:, 2:3] * acc_ref[1] / jnp.maximum(l_ref[1], 1e-30))
        for hh in range(H):
            g = hh // R
            o_ref[0, :, hh * DH:(hh + 1) * DH] = res[hh * n_q:(hh + 1) * n_q, g * DH:(g + 1) * DH]


def _flash_sample(pool, page_table, qbd, bias_slc, negmask, new_slc, bias_new, win_states, li, win_new, bias_win,
                  gsw):
    db, n_pages = page_table.shape
    steps = n_pages // CHUNK_PAGES
    n_q = new_slc.shape[1]
    rows = H * n_q
    keys = CHUNK_PAGES * PAGE
    cm = lambda f: (lambda i, j, pt: f(i, j))
    const = lambda shape: pl.BlockSpec(shape, lambda i, j, pt: (0,) * len(shape), pipeline_mode=pl.Buffered(1))
    per_b = lambda shape: pl.BlockSpec((1,) + shape, lambda i, j, pt: (i,) + (0,) * len(shape))
    page_specs = [pl.BlockSpec((1, 8, DH, PAGE), lambda i, j, pt, k=k: (pt[i, j * CHUNK_PAGES + k], 1, 0, 0))
                  for k in range(CHUNK_PAGES)]
    return pl.pallas_call(
        functools.partial(_flash_sample_body, n_q=n_q),
        out_shape=SDS((db, n_q, ATT), f32),
        grid_spec=pltpu.PrefetchScalarGridSpec(
            num_scalar_prefetch=1, grid=(db, steps),
            in_specs=page_specs + [
                per_b((rows, 256)),
                pl.BlockSpec((rows, keys), cm(lambda i, j: (0, j))),
                pl.BlockSpec((1, G * n_q, keys), cm(lambda i, j: (i, 0, j))),
                per_b((n_q, 512)), const((rows, PAGE)),
                pl.BlockSpec((1, 1, WIN, 512), lambda i, j, pt: (li, i, 0, 0)),
                per_b((n_q, 512)), const((rows, WIN + PAGE)),
                per_b((rows, 128))],
            out_specs=per_b((n_q, ATT)),
            scratch_shapes=[pltpu.VMEM((2, rows, 1), f32), pltpu.VMEM((2, rows, 1), f32),
                            pltpu.VMEM((2, rows, 256), f32), pltpu.VMEM((PAGE, 512), f32),
                            pltpu.VMEM((WIN + PAGE, 512), f32)]),
        compiler_params=_cp(("arbitrary", "arbitrary"), 48), name="flash_sample")(
            page_table, *([pool] * CHUNK_PAGES), qbd, bias_slc, negmask, new_slc, bias_new,
            win_states, win_new, bias_win, gsw)


def _dil_tile(q, kc, vc, kp, vp, bias_ref, gi, prev_off):
    lane = lax.broadcasted_iota(jnp.int32, (DT, 128), 1)
    low = lane < DH
    kcb, vcb = kc.astype(bf16), vc.astype(bf16)
    outs, lses = [], []
    for e in range(2):
        qe = jnp.where(low if e == 0 else jnp.logical_not(low), q, 0.0).astype(bf16)
        s0 = _nt(qe, kcb) + bias_ref[gi, 0, e, 0]
        m = jnp.max(s0, axis=-1, keepdims=True)
        if kp is not None:
            s1 = _nt(qe, kp.astype(bf16)) + bias_ref[gi, 0, e, 1] + prev_off
            m = jnp.maximum(m, jnp.max(s1, axis=-1, keepdims=True))
        e0 = jnp.exp(s0 - m)
        l = jnp.sum(e0, axis=-1, keepdims=True)
        o = _dot(e0.astype(bf16), vcb)
        if kp is not None:
            e1 = jnp.exp(s1 - m)
            l = l + jnp.sum(e1, axis=-1, keepdims=True)
            o = o + _dot(e1.astype(bf16), vp.astype(bf16))
        l = jnp.maximum(l, 1e-30)
        outs.append(o / l)
        lses.append(m + jnp.log(l))
    return jnp.where(low, outs[0], outs[1]), jnp.where(low, lses[0], lses[1])


def _dil_combine(os_, ls_):
    mx = jnp.maximum(jnp.maximum(ls_[0], ls_[1]), ls_[2])
    ws = [jnp.exp(l - mx) for l in ls_]
    den = ws[0] + ws[1] + ws[2]
    return [o * w / den for o, w in zip(os_, ws)]


def _dil_prompt_body(q0, q1, q2, k0, k1, k2, v0, v1, v2, bias_ref, y_ref, osc, lsc, *, s):
    qs, ks, vs = (q0, q1, q2), (k0, k1, k2), (v0, v1, v2)
    for gi, (_, dil) in enumerate(DIL):
        n_t = s // dil // DT
        q_r, k_r, v_r = qs[gi], ks[gi], vs[gi]

        def tile(start, prev_start, prev_off, gi=gi, dil=dil, q_r=q_r, k_r=k_r, v_r=v_r, n_t=n_t):
            rows = lambda st: pl.ds(st, DT, stride=dil) if dil > 1 else pl.ds(pl.multiple_of(st, DT), DT)
            kp = k_r[0, rows(prev_start), :] if n_t > 1 else None
            vp = v_r[0, rows(prev_start), :] if n_t > 1 else None
            o, lse = _dil_tile(q_r[0, rows(start), :], k_r[0, rows(start), :], v_r[0, rows(start), :],
                               kp, vp, bias_ref, gi, prev_off)
            osc[gi, rows(start), :] = o
            lsc[gi, rows(start), :] = lse

        def body(it, carry, dil=dil, n_t=n_t, tile=tile):
            rho, t = it // n_t, it % n_t
            start = rho + t * (DT * dil)
            prev = rho + jnp.maximum(t - 1, 0) * (DT * dil)
            tile(start, prev, jnp.where(t == 0, NEG, 0.0))
            return carry

        lax.fori_loop(0, dil * n_t, body, 0, unroll=DIL_UNROLL)
    outs = _dil_combine([osc[0], osc[1], osc[2]], [lsc[0], lsc[1], lsc[2]])
    for pair in range(2):
        @pl.when(pl.program_id(1) == pair)
        def _(pair=pair):
            for gi, o in enumerate(outs):
                c = 2 * gi + pair
                y_ref[0, :, c * 128:(c + 1) * 128] = o


def _dil_prompt(zd, bias_tiles, b, s):
    col = lambda base: [pl.BlockSpec((1, s, 128), lambda i, j, c=base + 2 * gi: (i, 0, c + j)) for gi in range(3)]
    return pl.pallas_call(
        functools.partial(_dil_prompt_body, s=s),
        out_shape=SDS((b, s, ATT), f32), grid=(b, 2),
        in_specs=col(0) + col(6) + col(12) + [pl.BlockSpec((3, 1, 2, 2, DT, DT), lambda i, j: (0, j, 0, 0, 0, 0))],
        out_specs=pl.BlockSpec((1, s, ATT), lambda i, j: (i, 0, 0)),
        scratch_shapes=[pltpu.VMEM((3, s, 128), f32), pltpu.VMEM((3, s, 128), f32)],
        compiler_params=_cp(("parallel", "arbitrary"), 48), name="dil_prompt")(*([zd] * 9), bias_tiles)


def _dil_sample_body(q_ref, s0, s1, s2, new_ref, b0, b1, b2, y_ref, n0, n1, n2, a0, a1, a2, *, n_q):
    lane = lax.broadcasted_iota(jnp.int32, (n_q, 256), 1)
    os_, ls_ = [], []
    for gi, (st, bias, nst, alls) in enumerate(((s0, b0, n0, a0), (s1, b1, n1, a1), (s2, b2, n2, a2))):
        w = st.shape[2]
        alls[...] = jnp.zeros(alls.shape, f32)
        alls[0:w, :] = st[0, 0]
        alls[w:w + n_q, :] = new_ref[0, gi]
        nst[0] = alls[n_q:w + n_q, :]
        s = _nt(q_ref[0, gi], alls[:, 0:256].astype(bf16)) + bias[...]
        m = jnp.max(s, axis=-1, keepdims=True)
        e = jnp.exp(s - m)
        l = jnp.maximum(jnp.sum(e, axis=-1, keepdims=True), 1e-30)
        res = _dot(e.astype(bf16), alls[:, 256:512].astype(bf16)) / l
        lse = m + jnp.log(l)
        o = jnp.zeros((n_q, 256), f32)
        lm = jnp.zeros((n_q, 256), f32)
        for hg in range(4):
            inb = (lane >= hg * DH) & (lane < (hg + 1) * DH)
            o = jnp.where(inb, res[hg * n_q:(hg + 1) * n_q, :], o)
            lm = jnp.where(inb, lse[hg * n_q:(hg + 1) * n_q, :], lm)
        os_.append(o)
        ls_.append(lm)
    for gi, o in enumerate(_dil_combine(os_, ls_)):
        y_ref[0, :, gi * 256:(gi + 1) * 256] = o


def _dil_sample(qbd, states, li, new_kv, biases):
    db, n_q = new_kv.shape[0], new_kv.shape[2]
    per_b = lambda shape: pl.BlockSpec((1,) + shape, lambda i: (i,) + (0,) * len(shape))
    wins = [st.shape[2] for st in states]
    return pl.pallas_call(
        functools.partial(_dil_sample_body, n_q=n_q),
        out_shape=(SDS((db, n_q, ATT), f32),) + tuple(SDS((db, w, 512), f32) for w in wins), grid=(db,),
        in_specs=[per_b((3, 4 * n_q, 256))]
                 + [pl.BlockSpec((1, 1, w, 512), lambda i: (li, i, 0, 0)) for w in wins]
                 + [per_b((3, n_q, 512))] + [_resident(bb.shape) for bb in biases],
        out_specs=(per_b((n_q, ATT)),) + tuple(per_b((w, 512)) for w in wins),
        scratch_shapes=[pltpu.VMEM((w + PAGE, 512), f32) for w in wins],
        compiler_params=_cp(("parallel",), 48), name="dil_sample")(qbd, *states, new_kv, *biases)


def _rel_bucket(dist):
    dist = jnp.maximum(dist, 0)
    d32 = jnp.maximum(dist, 1).astype(f32)
    large = 16 + (jnp.log(d32 / 16) / math.log(2048 / 16) * 16).astype(jnp.int32)
    return jnp.where(dist < 16, dist, jnp.minimum(large, 31))


def _bias_rev(tbl, n, keep, scale=1):
    dist = (n - 1 - jnp.arange(n)) * scale
    return jnp.where(keep(dist), tbl[_rel_bucket(dist)].T, NEG)


def _toeplitz_tiles(bdr, n_delta, t):
    hn = bdr.shape[0]
    bdp = jnp.concatenate([bdr, jnp.full((hn, t - 1), NEG, f32)], axis=1)
    lp = bdp.shape[1]
    gq = jnp.stack([bdp[:, lp + 1 - d * t - 2 * t:lp - d * t] for d in range(n_delta)])
    gq = jnp.pad(gq, ((0, 0), (0, 0), (0, 1)))
    x = jnp.broadcast_to(gq[:, :, None, :], (n_delta, hn, t, 2 * t)).reshape(n_delta, hn, t * 2 * t)
    y = x[..., :t * (2 * t - 1)].reshape(n_delta, hn, t, 2 * t - 1)
    return y[..., t - 1:]


def _rows_by_query(bdr, n_q, n_keys, first_dist):
    last = bdr.shape[1] - 1
    return jnp.stack([bdr[:, last - first_dist - i:last - first_dist - i + n_keys] for i in range(n_q)], axis=1)


def _new_token_bias(tbl, n_q, keep):
    tri = jnp.arange(n_q)[:, None] - jnp.arange(PAGE)[None, :]
    ok = (tri >= 0) & keep(tri)
    return jnp.where(ok[None], jnp.transpose(tbl[_rel_bucket(tri)], (2, 0, 1)), NEG)


def _nsa_tables(rel_bias, s, n_q, past):
    everywhere = lambda d: d >= 0
    in_window = lambda d: d < WIN
    n = past + n_q
    key_blk = jnp.arange(s // TQ)[:, None, None] * (TQ // SEL_BLOCK) + jnp.arange(TQ)[None, :, None] // SEL_BLOCK
    bias_new = _new_token_bias(rel_bias, n_q, everywhere).reshape(H * n_q, PAGE)
    bias_win = _rows_by_query(_bias_rev(rel_bias, n, in_window), n_q, WIN, WIN).reshape(H * n_q, WIN)
    return dict(
        bias_s=_toeplitz_tiles(_bias_rev(rel_bias, s, everywhere), s // TQ, TQ) * LOG2E,
        bias_w=_toeplitz_tiles(_bias_rev(rel_bias, WIN + TQ, in_window), WIN // TQ + 1, TQ) * LOG2E,
        et_tiles=(key_blk == jnp.arange(128)[None, None, :]).astype(bf16),
        e_full=(jnp.arange(past // SEL_BLOCK)[:, None] == jnp.arange(past)[None, :] // SEL_BLOCK).astype(bf16),
        bias_slc=_rows_by_query(_bias_rev(rel_bias, n, everywhere), n_q, past, past).reshape(H * n_q, past),
        bias_new=bias_new,
        bias_win=jnp.concatenate([bias_win, bias_new], axis=1))


def _nsa_layer(yp, ys, g_pre, g_post, w_in, w_out, cmp_pos, cmp_w1, cmp_w2, tb, pool, page_table, win_states,
               li, b, s, db, n_q, past):
    w_pad = jnp.pad(w_in, ((0, 0), (0, NSA_PAD - w_in.shape[1]))).astype(bf16)
    w_out = w_out.astype(bf16)
    w1cat = jnp.concatenate([cmp_w1[:, :1024], cmp_w1[:, 1024:]], axis=-1).astype(bf16)
    pos2 = cmp_pos.reshape(2, 2, 1024)
    w2 = cmp_w2.astype(bf16)

    q_hm, q2_hm, kv4, kwin, ks, vs, kw, vw, gates = _nsa_proj_prompt(yp, g_pre, w_pad, b, s)
    kc, vc = _compress_prompt(kv4, w1cat, pos2, w2, b, s)
    o_cmp, selneg = _cmp_sel_prompt(q_hm, kc, vc, gates, b, s)
    o_sw = _flash_prompt(q2_hm, ks, vs, kw, vw, tb["bias_s"], tb["bias_w"], selneg, tb["et_tiles"], gates, b, s)
    yp = _out_proj(yp, g_post, w_out, o_cmp.reshape(b * s, ATT), o_sw.reshape(b * s, ATT))
    kv_p = kv4.reshape(b, s, 4, G, DH)
    win_p = kwin.reshape(b, s, 2, G, DH)[:, s - min(WIN, s):]

    z = _proj(ys, g_pre, w_pad)
    q = (z[:, :ATT] * (DH ** -0.5)).reshape(db, n_q, H, DH).transpose(0, 2, 1, 3)
    qg = q.reshape(db, G, R * n_q, DH).astype(bf16)
    eye = jnp.eye(G, dtype=f32)
    qbd = (q.reshape(db, G, R * n_q, 1, DH) * eye[None, :, None, :, None]).reshape(db, H * n_q, G * DH).astype(bf16)
    kv_new = z[:, ATT:ATT + 1024].reshape(db, n_q, 1024)
    win_new = z[:, ATT + 1024:ATT + 1536].reshape(db, n_q, 512)
    gt = z[:, ATT + 1536:ATT + 1536 + 3 * H].reshape(db, n_q, 3, H).transpose(0, 3, 1, 2)
    gsw = jnp.pad(gt.reshape(db, H * n_q, 3), ((0, 0), (0, 0), (0, 125)))
    kc_s, vc_s = _compress_sample(pool, page_table, w1cat, pos2, w2)
    o_cmp_s, negmask = _cmp_sel_sample(qg, kc_s, vc_s, gsw, tb["e_full"], past)
    o_sw_s = _flash_sample(pool, page_table, qbd, tb["bias_slc"], negmask, kv_new[:, :, 512:], tb["bias_new"],
                           win_states.reshape(win_states.shape[0], db, WIN, 512), li, win_new, tb["bias_win"], gsw)
    ys = _out_proj(ys, g_post, w_out, o_cmp_s.reshape(db * n_q, ATT), o_sw_s.reshape(db * n_q, ATT))
    kv_s = kv_new.reshape(db, n_q, 4, G, DH)
    win_s = jnp.concatenate([win_states[li, :, n_q:], win_new.reshape(db, n_q, 2, G, DH)], axis=1)
    return yp, ys, kv_p, win_p, kv_s, win_s


def _dil_tables(rel_bias, n_q):
    tiles, biases = [], []
    for gi, (win, dil) in enumerate(DIL):
        tbl = rel_bias[:, gi * 4:(gi + 1) * 4]
        on_grid = lambda d, win=win, dil=dil: (d % dil == 0) & (d <= win)
        tiles.append(_toeplitz_tiles(_bias_rev(tbl, 2 * DT, on_grid, scale=dil), 2, DT))
        b_all = jnp.concatenate([_rows_by_query(_bias_rev(tbl, win + n_q, on_grid), n_q, win, win),
                                 _new_token_bias(tbl, n_q, on_grid)], axis=2)
        biases.append(b_all.reshape(4 * n_q, win + PAGE))
    return dict(tiles=jnp.stack(tiles).transpose(0, 2, 1, 3, 4).reshape(3, 2, 2, 2, DT, DT), sample=biases)


def _dil_layer(yp, ys, g_pre, g_post, w_in, w_out, tb, states, li, b, s, db, n_q):
    w_in = w_in.astype(bf16)
    w_out = w_out.astype(bf16)

    zd = _dil_proj(yp, g_pre, w_in)
    y = _dil_prompt(zd.reshape(b, s, 3 * ATT), tb["tiles"], b, s)
    yp = _out_proj(yp, g_post, w_out, y.reshape(b * s, ATT))
    z5 = zd.reshape(b, s, 3, H, DH)
    st_p = []
    for gi, (win, _) in enumerate(DIL):
        keep = min(win, s)
        st_p.append(jnp.stack([z5[:, s - keep:, 1, gi * 4:(gi + 1) * 4], z5[:, s - keep:, 2, gi * 4:(gi + 1) * 4]], axis=2))

    z = _proj(ys, g_pre, w_in).reshape(db, n_q, 3, 3, 4, DH)
    q = (z[:, :, 0] * (DH ** -0.5)).transpose(0, 2, 3, 1, 4)
    eye = jnp.eye(4, dtype=f32)
    qbd = (q[:, :, :, :, None, :] * eye[None, None, :, None, :, None]).reshape(db, 3, 4 * n_q, 256).astype(bf16)
    new_kv = jnp.concatenate([z[:, :, 1].transpose(0, 2, 1, 3, 4).reshape(db, 3, n_q, 256),
                              z[:, :, 2].transpose(0, 2, 1, 3, 4).reshape(db, 3, n_q, 256)], axis=-1)
    flat = [st.reshape(st.shape[0], db, st.shape[2], 512) for st in states]
    y_s, n0, n1, n2 = _dil_sample(qbd, flat, li, new_kv, tb["sample"])
    ys = _out_proj(ys, g_post, w_out, y_s.reshape(db * n_q, ATT))
    st_s = [n.reshape(st.shape[1:]) for n, st in zip((n0, n1, n2), states)]
    return yp, ys, st_p, st_s


def kernel(x_prompt, x_sample, cache_nsa_kv, page_table, state_nsa_win, state_dil_w128, state_dil_w512,
           state_dil_w2048, norm_g, ffn_w_in, ffn_w_out, nsa_w_in, nsa_w_out, nsa_cmp_pos, nsa_cmp_w1,
           nsa_cmp_w2, dil_w_in, dil_w_out, rel_bias):
    b, s, _ = x_prompt.shape
    db, n_q, _ = x_sample.shape
    depth = norm_g.shape[0]
    past = page_table.shape[1] * PAGE
    yp = x_prompt.reshape(b * s, D)
    ys = x_sample.reshape(db * n_q, D)
    nsa_kv_p, nsa_win_p, nsa_kv_s, nsa_win_s = [], [], [], []
    dil_p, dil_s = [[], [], []], [[], [], []]
    nsa_tb = _nsa_tables(rel_bias, s, n_q, past)
    dil_tb = _dil_tables(rel_bias, n_q)
    for i in range(depth):
        li = i // 2
        g = norm_g[i]
        w_i, w_o = ffn_w_in[i, 0].astype(bf16), ffn_w_out[i, 0].astype(bf16)
        yp = _half_ffn(yp, g[0:2], w_i, w_o)
        ys = _half_ffn(ys, g[0:2], w_i, w_o)
        if i % 2 == 0:
            n_pool = cache_nsa_kv.shape[1]
            pool = jnp.transpose(cache_nsa_kv, (0, 1, 3, 4, 5, 2)).reshape(
                cache_nsa_kv.shape[0] * n_pool, 4 * G, DH, PAGE)
            yp, ys, kvp, winp, kvs, wins = _nsa_layer(
                yp, ys, g[2:3], g[3:4], nsa_w_in[li], nsa_w_out[li], nsa_cmp_pos[li], nsa_cmp_w1[li],
                nsa_cmp_w2[li], nsa_tb, pool, page_table + li * n_pool, state_nsa_win, li, b, s, db, n_q, past)
            nsa_kv_p.append(kvp)
            nsa_win_p.append(winp)
            nsa_kv_s.append(kvs)
            nsa_win_s.append(wins)
        else:
            states = (state_dil_w128, state_dil_w512, state_dil_w2048)
            yp, ys, stp, sts = _dil_layer(yp, ys, g[2:3], g[3:4], dil_w_in[li], dil_w_out[li], dil_tb, states,
                                          li, b, s, db, n_q)
            for gi in range(3):
                dil_p[gi].append(stp[gi])
                dil_s[gi].append(sts[gi])
        w_i, w_o = ffn_w_in[i, 1].astype(bf16), ffn_w_out[i, 1].astype(bf16)
        yp = _half_ffn(yp, g[4:6], w_i, w_o)
        ys = _half_ffn(ys, g[4:6], w_i, w_o)
    return (yp.reshape(b, s, D), ys.reshape(db, n_q, D),
            jnp.stack(nsa_kv_p), jnp.stack(nsa_win_p),
            jnp.stack(dil_p[0]), jnp.stack(dil_p[1]), jnp.stack(dil_p[2]),
            jnp.stack(nsa_kv_s), jnp.stack(nsa_win_s),
            jnp.stack(dil_s[0]), jnp.stack(dil_s[1]), jnp.stack(dil_s[2]))
```

```python
import functools
import math

import jax
import jax.numpy as jnp
from jax import lax
from jax.experimental import pallas as pl
from jax.experimental.pallas import tpu as pltpu

f32, bf16 = jnp.float32, jnp.bfloat16
SDS = jax.ShapeDtypeStruct

D = 1024
FF = 2816
FF_CHUNK = 256
H, DH = 12, 64
G, R = 4, 3
ATT = H * DH
EPS = 1e-6
NEG = -1e30
LOG2E = math.log2(math.e)
FORCE = 1e4
SEL_BLOCK, TOPK = 64, 16
WIN = 512
NSA_PAD = 2432
TQ = 256
RC = 32
PAGE = 128
CHUNK_PAGES = 8
CMP_GROUP = 4
DIL = ((128, 1), (512, 4), (2048, 16))
DT = 128
DIL_UNROLL = 4
MIB = 1024 * 1024


def _cp(sem, vmem_mib):
    return pltpu.CompilerParams(dimension_semantics=sem, vmem_limit_bytes=vmem_mib * MIB)


def _rms(x, g):
    return x * lax.rsqrt(jnp.mean(x * x, axis=-1, keepdims=True) + EPS) * g


def _nt(a, b):
    return lax.dot_general(a, b, (((1,), (1,)), ((), ())), preferred_element_type=f32)


def _dot(a, b):
    return jnp.dot(a, b, preferred_element_type=f32)


def _resident(shape):
    nd = len(shape)
    return pl.BlockSpec(shape, lambda *_: (0,) * nd, pipeline_mode=pl.Buffered(1))


def _ffn_body(x_ref, g_ref, win_ref, wout_ref, o_ref):
    x = x_ref[...]
    h = _rms(x, g_ref[0:1, :]).astype(bf16)
    acc = jnp.zeros(x.shape, f32)
    for c in range(FF // FF_CHUNK):
        lo, hi = c * FF_CHUNK, (c + 1) * FF_CHUNK
        gate = _dot(h, win_ref[:, lo:hi])
        up = _dot(h, win_ref[:, FF + lo:FF + hi])
        act = (gate * jax.nn.sigmoid(gate) * up).astype(bf16)
        acc = acc + _dot(act, wout_ref[lo:hi, :])
    o_ref[...] = x + 0.5 * _rms(acc, g_ref[1:2, :])


def _half_ffn(x, g2, w_in, w_out):
    m = x.shape[0]
    tm = min(m, 512)
    return pl.pallas_call(
        _ffn_body, out_shape=SDS((m, D), f32), grid=(m // tm,),
        in_specs=[pl.BlockSpec((tm, D), lambda i: (i, 0)), _resident((2, D)),
                  _resident((D, 2 * FF)), _resident((FF, D))],
        out_specs=pl.BlockSpec((tm, D), lambda i: (i, 0)),
        compiler_params=_cp(("parallel",), 48), name="half_ffn")(x, g2, w_in, w_out)


def _proj_body(x_ref, g_ref, w_ref, o_ref):
    h = _rms(x_ref[...], g_ref[...]).astype(bf16)
    o_ref[...] = _dot(h, w_ref[...])


def _proj(x, g, w):
    m, n = x.shape[0], w.shape[1]
    return pl.pallas_call(
        _proj_body, out_shape=SDS((m, n), f32), grid=(1,),
        in_specs=[_resident((m, D)), _resident((1, D)), _resident((D, n))],
        out_specs=pl.BlockSpec((m, n), lambda i: (0, 0)),
        compiler_params=_cp(("arbitrary",), 32), name="proj_sample")(x, g, w)


def _nsa_proj_body(x_ref, g_ref, w_ref, q_ref, q2_ref, kv_ref, kwin_ref, ks_ref, vs_ref, kw_ref, vw_ref, gt_ref):
    h = _rms(x_ref[...], g_ref[...]).astype(bf16)
    zq = _dot(h, w_ref[:, 0:ATT]) * (DH ** -0.5)
    zq2 = zq * LOG2E
    for hh in range(H):
        q_ref[0, hh] = zq[:, hh * DH:(hh + 1) * DH].astype(bf16)
        q2_ref[0, hh] = zq2[:, hh * DH:(hh + 1) * DH].astype(bf16)
    zc = _dot(h, w_ref[:, ATT:ATT + 512])
    kv_ref[0, 0:2 * G] = zc.T.reshape(2 * G, DH, zc.shape[0])
    zs = _dot(h, w_ref[:, ATT + 512:ATT + 1024])
    kv_ref[0, 2 * G:4 * G] = zs.T.reshape(2 * G, DH, zs.shape[0])
    zw = _dot(h, w_ref[:, ATT + 1024:ATT + 1536])
    kwin_ref[...] = zw
    ones = jnp.ones((zs.shape[0], DH), bf16)
    for g in range(G):
        ks_ref[0, g] = zs[:, g * DH:(g + 1) * DH].astype(bf16)
        vs_ref[0, g] = jnp.concatenate([zs[:, 256 + g * DH:256 + (g + 1) * DH].astype(bf16), ones], axis=1)
        kw_ref[0, g] = zw[:, g * DH:(g + 1) * DH].astype(bf16)
        vw_ref[0, g] = jnp.concatenate([zw[:, 256 + g * DH:256 + (g + 1) * DH].astype(bf16), ones], axis=1)
    gt_ref[...] = _dot(h, w_ref[:, ATT + 1536:NSA_PAD])


def _nsa_proj_prompt(x, g, w, b, s):
    tm = 512
    nt = s // tm
    hm = lambda n, w_: pl.BlockSpec((1, n, tm, w_), lambda i: (i // nt, 0, i % nt, 0))
    rows = lambda n: pl.BlockSpec((tm, n), lambda i: (i, 0))
    return pl.pallas_call(
        _nsa_proj_body,
        out_shape=(SDS((b, H, s, DH), bf16), SDS((b, H, s, DH), bf16), SDS((b, 4 * G, DH, s), f32),
                   SDS((b * s, 512), f32), SDS((b, G, s, DH), bf16), SDS((b, G, s, 2 * DH), bf16),
                   SDS((b, G, s, DH), bf16), SDS((b, G, s, 2 * DH), bf16), SDS((b * s, 128), f32)),
        grid=(b * s // tm,),
        in_specs=[rows(D), _resident((1, D)), _resident((D, NSA_PAD))],
        out_specs=(hm(H, DH), hm(H, DH), pl.BlockSpec((1, 4 * G, DH, tm), lambda i: (i // nt, 0, 0, i % nt)),
                   rows(512), hm(G, DH), hm(G, 2 * DH), hm(G, DH), hm(G, 2 * DH), rows(128)),
        compiler_params=_cp(("parallel",), 48), name="nsa_proj_prompt")(x, g, w)


def _dil_proj_body(x_ref, g_ref, w_ref, o_ref, kvt_ref):
    h = _rms(x_ref[...], g_ref[...]).astype(bf16)
    o_ref[:, 0:ATT] = _dot(h, w_ref[:, 0:ATT]) * (DH ** -0.5)
    for part in (1, 2):
        z = _dot(h, w_ref[:, part * ATT:(part + 1) * ATT])
        o_ref[:, part * ATT:(part + 1) * ATT] = z
        kvt_ref[0, (part - 1) * H:part * H] = z.T.reshape(H, DH, z.shape[0])


def _dil_proj(x, g, w, b, s):
    tm = 512
    nt = s // tm
    return pl.pallas_call(
        _dil_proj_body, out_shape=(SDS((b * s, 3 * ATT), f32), SDS((b, 2 * H, DH, s), f32)), grid=(b * s // tm,),
        in_specs=[pl.BlockSpec((tm, D), lambda i: (i, 0)), _resident((1, D)), _resident((D, 3 * ATT))],
        out_specs=(pl.BlockSpec((tm, 3 * ATT), lambda i: (i, 0)),
                   pl.BlockSpec((1, 2 * H, DH, tm), lambda i: (i // nt, 0, 0, i % nt))),
        compiler_params=_cp(("parallel",), 48), name="dil_proj")(x, g, w)


def _out_proj_body(x_ref, g_ref, w_ref, *refs):
    o_ref = refs[-1]
    o = refs[0][...]
    for r in refs[1:-1]:
        o = o + r[...]
    y = _dot(o.astype(bf16), w_ref[...])
    o_ref[...] = x_ref[...] + _rms(y, g_ref[...])


def _out_proj(x, g, w, *branches):
    m = x.shape[0]
    tm = min(m, 512)
    row = lambda n: pl.BlockSpec((tm, n), lambda i: (i, 0))
    return pl.pallas_call(
        _out_proj_body, out_shape=SDS((m, D), f32), grid=(m // tm,),
        in_specs=[row(D), _resident((1, D)), _resident((ATT, D))] + [row(ATT)] * len(branches),
        out_specs=row(D), compiler_params=_cp(("parallel",), 32), name="out_proj")(x, g, w, *branches)


def _compress_body(*refs, n_src, n_pages, paged, group=1):
    if paged:
        refs = refs[1:]
    srcs = refs[:n_src]
    perm_ref, w1_ref, pos_ref, w2_ref, kc_ref, vc_ref, lhs_ref, aprev_ref = refs[n_src:]
    nch = n_pages * PAGE // 16
    gn = group * nch
    step = pl.program_id(1) if paged else 0
    sub = step % group if paged else 0

    for p in range(n_pages):
        mats = srcs[p][0] if paged else srcs[0][0, :, :, p * PAGE:(p + 1) * PAGE]
        both = jnp.concatenate([jnp.concatenate([mats[2 * c], mats[2 * c + 1]], axis=1) for c in range(4)], axis=0)
        out = _dot(both.astype(bf16), perm_ref[...])
        for c in range(4):
            kind, pair = c // 2, c % 2
            o = out[c * DH:(c + 1) * DH]
            zt = jnp.concatenate([o[:, 0:128], o[:, 128:256]], axis=0).T
            for gl in range(2):
                start = (2 * pair + gl) * gn + sub * nch + p * 8
                dst = pl.ds(pl.multiple_of(start, 8), 8) if paged else pl.ds(start, 8)
                for r2 in range(8):
                    lhs_ref[kind, dst, r2 * 128:(r2 + 1) * 128] = zt[(gl * 8 + r2) * 8:(gl * 8 + r2 + 1) * 8, :]

    def mlp():
        first = (step < group) if paged else True
        if paged:
            @pl.when(step == group - 1)
            def _():
                aprev_ref[...] = jnp.zeros(aprev_ref.shape, f32)

        row = lax.broadcasted_iota(jnp.int32, (gn, 256), 0)
        row_o = lax.broadcasted_iota(jnp.int32, (gn, DH), 0)
        for kind, out_ref in ((0, kc_ref), (1, vc_ref)):
            pp = _dot(lhs_ref[kind].astype(bf16), w1_ref[kind])
            pos = _dot(pos_ref[kind].astype(bf16), w1_ref[kind])
            posterm = pos[0:1, 0:256] + pos[1:2, 256:512]
            for g in range(G):
                a = pp[g * gn:(g + 1) * gn, 0:256]
                bm = pp[g * gn:(g + 1) * gn, 256:512]
                shifted = pltpu.roll(a, 1, 0)
                if paged:
                    carry = aprev_ref[kind, (g + 1) * gn - 1:(g + 1) * gn, :]
                    shifted = jnp.where(row == 0, carry, shifted)
                    aprev_ref[kind, g * gn:(g + 1) * gn, :] = a
                pre = shifted + bm + posterm
                hid = (pre * jax.nn.sigmoid(pre)).astype(bf16)
                out = _dot(hid, w2_ref[kind])
                out = jnp.where((row_o == 0) & first, 0.0, out)
                out_ref[0, g] = out.astype(bf16)

    if paged:
        pl.when(sub == group - 1)(mlp)
    else:
        mlp()


def _chunk_perm():
    t = jnp.arange(PAGE)
    col = (t % 2) * 128 + ((t % 16) // 2) * 8 + t // 16
    src = jnp.concatenate([col, col + DH])
    return (src[:, None] == jnp.arange(256)[None, :]).astype(bf16)


def _compress_prompt(kvt, perm, w1cat, pos2, w2, b, s):
    nch = s // 16
    out = SDS((b, G, nch, DH), bf16)
    ospec = pl.BlockSpec((1, G, nch, DH), lambda i: (i, 0, 0, 0))
    return pl.pallas_call(
        functools.partial(_compress_body, n_src=1, n_pages=s // PAGE, paged=False),
        out_shape=(out, out), grid=(b,),
        in_specs=[pl.BlockSpec((1, 8, DH, s), lambda i: (i, 0, 0, 0)), _resident((256, 256)),
                  _resident((2, 1024, 512)), _resident((2, 2, 1024)), _resident((2, 256, DH))],
        out_specs=(ospec, ospec),
        scratch_shapes=[pltpu.VMEM((2, G * nch, 1024), f32), pltpu.VMEM((2, G * nch, 256), f32)],
        compiler_params=_cp(("parallel",), 48), name="compress_prompt")(kvt, perm, w1cat, pos2, w2)


def _compress_sample(pool_t, page_table, perm, w1cat, pos2, w2):
    db, n_pages = page_table.shape
    steps = n_pages // CHUNK_PAGES
    gn = CMP_GROUP * CHUNK_PAGES * PAGE // 16
    out = SDS((db, G, n_pages * PAGE // 16, DH), bf16)
    ospec = pl.BlockSpec((1, G, gn, DH), lambda i, j, pt: (i, 0, j // CMP_GROUP, 0))
    const = lambda shape: pl.BlockSpec(shape, lambda i, j, pt: (0,) * len(shape), pipeline_mode=pl.Buffered(1))
    page_specs = [pl.BlockSpec((1, 8, DH, PAGE), lambda i, j, pt, k=k: (pt[i, j * CHUNK_PAGES + k], 0, 0, 0))
                  for k in range(CHUNK_PAGES)]
    return pl.pallas_call(
        functools.partial(_compress_body, n_src=CHUNK_PAGES, n_pages=CHUNK_PAGES, paged=True, group=CMP_GROUP),
        out_shape=(out, out),
        grid_spec=pltpu.PrefetchScalarGridSpec(
            num_scalar_prefetch=1, grid=(db, steps),
            in_specs=page_specs + [const((256, 256)), const((2, 1024, 512)), const((2, 2, 1024)),
                                   const((2, 256, DH))],
            out_specs=(ospec, ospec),
            scratch_shapes=[pltpu.VMEM((2, G * gn, 1024), f32), pltpu.VMEM((2, G * gn, 256), f32)]),
        compiler_params=_cp(("arbitrary", "arbitrary"), 40), name="compress_sample")(
            page_table, *([pool_t] * CHUNK_PAGES), perm, w1cat, pos2, w2)


def _block_scores(impt_ref, n_blocks):
    return (impt_ref[pl.ds(1, n_blocks, stride=4), :] + impt_ref[pl.ds(2, n_blocks, stride=4), :]
            + impt_ref[pl.ds(3, n_blocks, stride=4), :])


def _force_scores(v, q_pos):
    j = lax.broadcasted_iota(jnp.int32, v.shape, 0)
    cur = q_pos >> 6
    forced = (j == 0) | (j == cur) | (j == cur - 1)
    return jnp.where(forced, FORCE, jnp.where(j > cur, -FORCE, v)), j


def _cmp_sel_prompt_body(q_ref, kc_ref, vc_ref, gt_ref, o_ref, sel_ref, impt_ref, *, n_sel):
    qt = pl.program_id(1)
    sig = jax.nn.sigmoid(gt_ref[...])
    t = qt * TQ + lax.broadcasted_iota(jnp.int32, (TQ, 128), 0)
    mp = lax.broadcasted_iota(jnp.int32, (TQ, 128), 1)
    valid = (mp >= 1) & (16 * mp + 15 <= t)
    for g in range(G):
        kc, vc = kc_ref[0, g], vc_ref[0, g]
        imp = jnp.zeros((TQ, 128), f32)
        for r in range(R):
            hh = g * R + r
            s = jnp.where(valid, _nt(q_ref[0, hh], kc), NEG)
            m = jnp.max(s, axis=-1, keepdims=True)
            e = jnp.where(valid, jnp.exp(s - m), 0.0)
            p = e / jnp.maximum(jnp.sum(e, axis=-1, keepdims=True), 1e-30)
            imp = imp + p
            o_ref[0, :, hh * DH:(hh + 1) * DH] = _dot(p.astype(bf16), vc) * sig[:, hh:hh + 1]
        for half in range(TQ // 128):
            impt_ref[...] = imp[half * 128:(half + 1) * 128, :].T
            v = _block_scores(impt_ref, n_sel)
            q_pos = qt * TQ + half * 128 + lax.broadcasted_iota(jnp.int32, v.shape, 1)
            val, j = _force_scores(v, q_pos)
            rank = jnp.zeros(v.shape, jnp.int32)
            for k in range(n_sel):
                vk = val[k:k + 1, :]
                rank = rank + jnp.where((vk > val) | ((vk == val) & (j > k)), 1, 0)
            neg = jnp.where(rank < TOPK, 0.0, NEG)
            full = jnp.concatenate([neg, jnp.zeros((128 - n_sel, 128), f32)], axis=0)
            sel_ref[0, g, half * 128:(half + 1) * 128, :] = full.T.astype(bf16)


def _cmp_sel_prompt(q_hm, kc, vc, gates, b, s):
    nq = s // TQ
    return pl.pallas_call(
        functools.partial(_cmp_sel_prompt_body, n_sel=s // SEL_BLOCK),
        out_shape=(SDS((b, s, ATT), f32), SDS((b, G, s, 128), bf16)), grid=(b, nq),
        in_specs=[pl.BlockSpec((1, H, TQ, DH), lambda i, j: (i, 0, j, 0)),
                  pl.BlockSpec((1, G, 128, DH), lambda i, j: (i, 0, 0, 0)),
                  pl.BlockSpec((1, G, 128, DH), lambda i, j: (i, 0, 0, 0)),
                  pl.BlockSpec((TQ, 128), lambda i, j: (i * nq + j, 0))],
        out_specs=(pl.BlockSpec((1, TQ, ATT), lambda i, j: (i, j, 0)),
                   pl.BlockSpec((1, G, TQ, 128), lambda i, j: (i, 0, j, 0))),
        scratch_shapes=[pltpu.VMEM((128, 128), f32)],
        compiler_params=_cp(("parallel", "parallel"), 32), name="cmp_sel_prompt")(q_hm, kc, vc, gates)


def _online_update(s, v, m_ref, l_ref, acc_ref, idx):
    m_old = m_ref[idx]
    m_new = jnp.maximum(m_old, jnp.max(s, axis=-1, keepdims=True))
    alpha = jnp.exp(m_old - m_new)
    p = jnp.exp(s - m_new)
    l_ref[idx] = alpha * l_ref[idx] + jnp.sum(p, axis=-1, keepdims=True)
    acc_ref[idx] = alpha * acc_ref[idx] + _dot(p.astype(bf16), v)
    m_ref[idx] = m_new


def _group_update(qa, ka, va, bias_ref, delta, h0, m_ref, acc_ref, idx0, s_ref, p_ref, a_ref):
    s_ref[...] = _nt(qa, ka)
    for r in range(R):
        for c in range(TQ // RC):
            lo = c * RC
            s = s_ref[r * TQ + lo:r * TQ + lo + RC, :] + bias_ref[delta, h0 + r, lo:lo + RC, :]
            m_old = m_ref[idx0 + r, lo:lo + RC, :]
            m_new = jnp.maximum(m_old, jnp.max(s, axis=-1, keepdims=True))
            p_ref[r * TQ + lo:r * TQ + lo + RC, :] = jnp.exp2(
                s - jnp.concatenate([m_new] * (TQ // 128), axis=1)).astype(bf16)
            m_ref[idx0 + r, lo:lo + RC, :] = m_new
            a_ref[r * TQ + lo:r * TQ + lo + RC, :] = jnp.exp2(m_old - m_new)
    pv = _dot(p_ref[...], va)
    for r in range(R):
        acc_ref[idx0 + r] = a_ref[r * TQ:(r + 1) * TQ, :] * acc_ref[idx0 + r] + pv[r * TQ:(r + 1) * TQ, :]


def _flash_prompt_body(q_ref, ks_ref, vs_ref, kw_ref, vw_ref, bs_ref, bw_ref, sel_ref, et_ref, gt_ref, o_ref,
                       m_ref, acc_ref, s_ref, p_ref, a_ref):
    qt, kt = pl.program_id(1), pl.program_id(2)
    delta = qt - kt
    scratch = (s_ref, p_ref, a_ref)

    @pl.when(kt == 0)
    def _():
        m_ref[...] = jnp.full(m_ref.shape, NEG, f32)
        acc_ref[...] = jnp.zeros(acc_ref.shape, f32)

    @pl.when(kt <= qt)
    def _():
        for g in range(G):
            q3 = q_ref[0, g * R:(g + 1) * R].reshape(R * TQ, DH)
            qa = jnp.concatenate([jnp.concatenate([sel_ref[0, g]] * R, axis=0), q3], axis=1)
            ka = jnp.concatenate([et_ref[0], ks_ref[0, g]], axis=1)
            _group_update(qa, ka, vs_ref[0, g], bs_ref, delta, g * R, m_ref, acc_ref, g * R, *scratch)

    @pl.when((kt <= qt) & (kt >= qt - (WIN // TQ)))
    def _():
        for g in range(G):
            q3 = q_ref[0, g * R:(g + 1) * R].reshape(R * TQ, DH)
            _group_update(q3, kw_ref[0, g], vw_ref[0, g], bw_ref, delta, g * R, m_ref, acc_ref, H + g * R, *scratch)

    @pl.when(kt == qt)
    def _():
        sig = jax.nn.sigmoid(gt_ref[...])
        for hh in range(H):
            a_s, a_w = acc_ref[hh], acc_ref[H + hh]
            o_s = a_s / jnp.maximum(pltpu.roll(a_s, DH, 1), 1e-30)
            o_w = a_w / jnp.maximum(pltpu.roll(a_w, DH, 1), 1e-30)
            o = sig[:, H + hh:H + hh + 1] * o_s + sig[:, 2 * H + hh:2 * H + hh + 1] * o_w
            o_ref[0, :, hh * DH:(hh + 1) * DH] = o[:, 0:DH]


def _flash_prompt(q2, ks, vs, kw, vw, bias_s, bias_w, selneg, et_tiles, gates, b, s):
    nq = s // TQ
    nw = WIN // TQ
    kv = lambda w_, f: pl.BlockSpec((1, G, TQ, w_), lambda i, j, k: (i, 0, f(j, k), 0))
    sel_kt = lambda j, k: jnp.minimum(k, j)
    win_kt = lambda j, k: jnp.clip(k, jnp.maximum(j - nw, 0), j)
    return pl.pallas_call(
        _flash_prompt_body, out_shape=SDS((b, s, ATT), f32), grid=(b, nq, nq),
        in_specs=[pl.BlockSpec((1, H, TQ, DH), lambda i, j, k: (i, 0, j, 0)),
                  kv(DH, sel_kt), kv(2 * DH, sel_kt), kv(DH, win_kt), kv(2 * DH, win_kt),
                  _resident(bias_s.shape), _resident(bias_w.shape),
                  pl.BlockSpec((1, G, TQ, 128), lambda i, j, k: (i, 0, j, 0)),
                  pl.BlockSpec((1, TQ, 128), lambda i, j, k: (jnp.minimum(k, j), 0, 0)),
                  pl.BlockSpec((TQ, 128), lambda i, j, k: (i * nq + j, 0))],
        out_specs=pl.BlockSpec((1, TQ, ATT), lambda i, j, k: (i, j, 0)),
        scratch_shapes=[pltpu.VMEM((2 * H, TQ, 128), f32), pltpu.VMEM((2 * H, TQ, 128), f32),
                        pltpu.VMEM((R * TQ, TQ), f32), pltpu.VMEM((R * TQ, TQ), bf16), pltpu.VMEM((R * TQ, 128), f32)],
        compiler_params=_cp(("parallel", "parallel", "arbitrary"), 58), name="flash_prompt")(
            q2, ks, vs, kw, vw, bias_s, bias_w, selneg, et_tiles, gates)


def _cmp_sel_sample_body(q_ref, kc_ref, vc_ref, gt_ref, e_ref, o_ref, nm_ref, impt_ref, val_ref, *, past, n_q):
    n_cmp = kc_ref.shape[2]
    n_sel = n_cmp // 4
    sig = jax.nn.sigmoid(gt_ref[0])
    rows = R * n_q
    mp = lax.broadcasted_iota(jnp.int32, (rows, n_cmp), 1)
    i_q = lax.broadcasted_iota(jnp.int32, (rows, n_cmp), 0) % n_q
    valid = (mp >= 1) & (16 * mp + 15 <= past + i_q)
    imps = []
    for g in range(G):
        s = jnp.where(valid, _nt(q_ref[0, g], kc_ref[0, g]), NEG)
        m = jnp.max(s, axis=-1, keepdims=True)
        e = jnp.where(valid, jnp.exp(s - m), 0.0)
        p = e / jnp.maximum(jnp.sum(e, axis=-1, keepdims=True), 1e-30)
        o = _dot(p.astype(bf16), vc_ref[0, g]) * sig[g * rows:(g + 1) * rows, 0:1]
        for r in range(R):
            hh = g * R + r
            o_ref[0, :, hh * DH:(hh + 1) * DH] = o[r * n_q:(r + 1) * n_q, :]
        imps.append(p[0:n_q] + p[n_q:2 * n_q] + p[2 * n_q:3 * n_q])
    imp = jnp.concatenate(imps + [jnp.zeros((128 - G * n_q, n_cmp), f32)], axis=0)
    impt_ref[...] = imp.T
    v = _block_scores(impt_ref, n_sel)
    q_pos = past + lax.broadcasted_iota(jnp.int32, v.shape, 1) % n_q
    val, j = _force_scores(v, q_pos)
    val_ref[...] = val

    def count(k, rank):
        vk = val_ref[pl.ds(k, 1), :]
        return rank + jnp.where((vk > val) | ((vk == val) & (j > k)), 1, 0)

    rank = lax.fori_loop(0, n_sel, count, jnp.zeros(v.shape, jnp.int32))
    rank = rank + jnp.where(FORCE > val, 1, 0)
    neg = jnp.where(rank < TOPK, 0.0, NEG).T
    nm_ref[0] = _dot(neg[0:G * n_q, :].astype(bf16), e_ref[...])


def _cmp_sel_sample(qg, kc, vc, gsw, e_full, past):
    db, n_q = qg.shape[0], qg.shape[2] // R
    n_cmp = kc.shape[2]
    return pl.pallas_call(
        functools.partial(_cmp_sel_sample_body, past=past, n_q=n_q),
        out_shape=(SDS((db, n_q, ATT), f32), SDS((db, G * n_q, past), f32)), grid=(db,),
        in_specs=[pl.BlockSpec((1, G, R * n_q, DH), lambda i: (i, 0, 0, 0)),
                  pl.BlockSpec((1, G, n_cmp, DH), lambda i: (i, 0, 0, 0)),
                  pl.BlockSpec((1, G, n_cmp, DH), lambda i: (i, 0, 0, 0)),
                  pl.BlockSpec((1, H * n_q, 128), lambda i: (i, 0, 0)),
                  _resident((n_cmp // 4, past))],
        out_specs=(pl.BlockSpec((1, n_q, ATT), lambda i: (i, 0, 0)),
                   pl.BlockSpec((1, G * n_q, past), lambda i: (i, 0, 0))),
        scratch_shapes=[pltpu.VMEM((n_cmp, 128), f32), pltpu.VMEM((n_cmp // 4, 128), f32)],
        compiler_params=_cp(("parallel",), 32), name="cmp_sel_sample")(qg, kc, vc, gsw, e_full)


def _flash_sample_body(*refs, n_q):
    refs = refs[1:]
    pages = refs[:CHUNK_PAGES]
    (q_ref, bias_ref, nm_ref, new_ref, bnew_ref, win_ref, wnew_ref, bwin_ref, gt_ref, o_ref,
     m_ref, l_ref, acc_ref, pad_ref, wall_ref) = refs[CHUNK_PAGES:]
    step = pl.program_id(1)
    q = q_ref[0]

    @pl.when(step == 0)
    def _():
        m_ref[...] = jnp.full(m_ref.shape, NEG, f32)
        l_ref[...] = jnp.zeros(l_ref.shape, f32)
        acc_ref[...] = jnp.zeros(acc_ref.shape, f32)

    nm = nm_ref[0]
    nm = jnp.concatenate([nm[g * n_q:(g + 1) * n_q] for g in range(G) for _ in range(R)], axis=0)
    s = jnp.concatenate([_dot(q, pages[p][0, 0:G].reshape(G * DH, PAGE).astype(bf16)) for p in range(CHUNK_PAGES)],
                        axis=1)
    s = s + bias_ref[...] + nm
    m_old = m_ref[0]
    m_new = jnp.maximum(m_old, jnp.max(s, axis=-1, keepdims=True))
    alpha = jnp.exp(m_old - m_new)
    p = jnp.exp(s - m_new)
    l_ref[0] = alpha * l_ref[0] + jnp.sum(p, axis=-1, keepdims=True)
    p = p.astype(bf16)
    pv = acc_ref[0] * alpha
    for pg in range(CHUNK_PAGES):
        pv = pv + _nt(p[:, pg * PAGE:(pg + 1) * PAGE], pages[pg][0, G:2 * G].reshape(G * DH, PAGE).astype(bf16))
    acc_ref[0] = pv
    m_ref[0] = m_new

    @pl.when(step == pl.num_programs(1) - 1)
    def _():
        pad_ref[...] = jnp.zeros(pad_ref.shape, f32)
        pad_ref[0:n_q, :] = new_ref[0]
        _online_update(_nt(q, pad_ref[:, 0:256].astype(bf16)) + bnew_ref[...], pad_ref[:, 256:512].astype(bf16),
                       m_ref, l_ref, acc_ref, 0)
        wall_ref[...] = jnp.zeros(wall_ref.shape, f32)
        wall_ref[0:WIN, :] = win_ref[0, 0]
        wall_ref[WIN:WIN + n_q, :] = wnew_ref[0]
        _online_update(_nt(q, wall_ref[:, 0:256].astype(bf16)) + bwin_ref[...], wall_ref[:, 256:512].astype(bf16),
                       m_ref, l_ref, acc_ref, 1)
        sig = jax.nn.sigmoid(gt_ref[0])
        res = (sig[:, 1:2] * acc_ref[0] / jnp.maximum(l_ref[0], 1e-30)
               + sig[:, 2:3] * acc_ref[1] / jnp.maximum(l_ref[1], 1e-30))
        for hh in range(H):
            g = hh // R
            o_ref[0, :, hh * DH:(hh + 1) * DH] = res[hh * n_q:(hh + 1) * n_q, g * DH:(g + 1) * DH]


def _flash_sample(pool, page_table, qbd, bias_slc, negmask, new_slc, bias_new, win_states, li, win_new, bias_win,
                  gsw):
    db, n_pages = page_table.shape
    steps = n_pages // CHUNK_PAGES
    n_q = new_slc.shape[1]
    rows = H * n_q
    keys = CHUNK_PAGES * PAGE
    cm = lambda f: (lambda i, j, pt: f(i, j))
    const = lambda shape: pl.BlockSpec(shape, lambda i, j, pt: (0,) * len(shape), pipeline_mode=pl.Buffered(1))
    per_b = lambda shape: pl.BlockSpec((1,) + shape, lambda i, j, pt: (i,) + (0,) * len(shape))
    page_specs = [pl.BlockSpec((1, 8, DH, PAGE), lambda i, j, pt, k=k: (pt[i, j * CHUNK_PAGES + k], 1, 0, 0))
                  for k in range(CHUNK_PAGES)]
    return pl.pallas_call(
        functools.partial(_flash_sample_body, n_q=n_q),
        out_shape=SDS((db, n_q, ATT), f32),
        grid_spec=pltpu.PrefetchScalarGridSpec(
            num_scalar_prefetch=1, grid=(db, steps),
            in_specs=page_specs + [
                per_b((rows, 256)),
                pl.BlockSpec((rows, keys), cm(lambda i, j: (0, j))),
                pl.BlockSpec((1, G * n_q, keys), cm(lambda i, j: (i, 0, j))),
                per_b((n_q, 512)), const((rows, PAGE)),
                pl.BlockSpec((1, 1, WIN, 512), lambda i, j, pt: (li, i, 0, 0)),
                per_b((n_q, 512)), const((rows, WIN + PAGE)),
                per_b((rows, 128))],
            out_specs=per_b((n_q, ATT)),
            scratch_shapes=[pltpu.VMEM((2, rows, 1), f32), pltpu.VMEM((2, rows, 1), f32),
                            pltpu.VMEM((2, rows, 256), f32), pltpu.VMEM((PAGE, 512), f32),
                            pltpu.VMEM((WIN + PAGE, 512), f32)]),
        compiler_params=_cp(("arbitrary", "arbitrary"), 48), name="flash_sample")(
            page_table, *([pool] * CHUNK_PAGES), qbd, bias_slc, negmask, new_slc, bias_new,
            win_states, win_new, bias_win, gsw)


def _dil_tile(q, kc, vc, kp, vp, bias_ref, gi, prev_off):
    lane = lax.broadcasted_iota(jnp.int32, (DT, 128), 1)
    low = lane < DH
    kcb, vcb = kc.astype(bf16), vc.astype(bf16)
    outs, lses = [], []
    for e in range(2):
        qe = jnp.where(low if e == 0 else jnp.logical_not(low), q, 0.0).astype(bf16)
        s0 = _nt(qe, kcb) + bias_ref[gi, 0, e, 0]
        m = jnp.max(s0, axis=-1, keepdims=True)
        if kp is not None:
            s1 = _nt(qe, kp.astype(bf16)) + bias_ref[gi, 0, e, 1] + prev_off
            m = jnp.maximum(m, jnp.max(s1, axis=-1, keepdims=True))
        e0 = jnp.exp(s0 - m)
        l = jnp.sum(e0, axis=-1, keepdims=True)
        o = _dot(e0.astype(bf16), vcb)
        if kp is not None:
            e1 = jnp.exp(s1 - m)
            l = l + jnp.sum(e1, axis=-1, keepdims=True)
            o = o + _dot(e1.astype(bf16), vp.astype(bf16))
        l = jnp.maximum(l, 1e-30)
        outs.append(o / l)
        lses.append(m + jnp.log(l))
    return jnp.where(low, outs[0], outs[1]), jnp.where(low, lses[0], lses[1])


def _dil_combine(os_, ls_):
    mx = jnp.maximum(jnp.maximum(ls_[0], ls_[1]), ls_[2])
    ws = [jnp.exp(l - mx) for l in ls_]
    den = ws[0] + ws[1] + ws[2]
    return [o * w / den for o, w in zip(os_, ws)]


def _dil_prompt_body(q0, q1, q2, k0, k1, k2, v0, v1, v2, bias_ref, y_ref, osc, lsc, *, s):
    qs, ks, vs = (q0, q1, q2), (k0, k1, k2), (v0, v1, v2)
    for gi, (_, dil) in enumerate(DIL):
        n_t = s // dil // DT
        q_r, k_r, v_r = qs[gi], ks[gi], vs[gi]

        def tile(start, prev_start, prev_off, gi=gi, dil=dil, q_r=q_r, k_r=k_r, v_r=v_r, n_t=n_t):
            rows = lambda st: pl.ds(st, DT, stride=dil) if dil > 1 else pl.ds(pl.multiple_of(st, DT), DT)
            kp = k_r[0, rows(prev_start), :] if n_t > 1 else None
            vp = v_r[0, rows(prev_start), :] if n_t > 1 else None
            o, lse = _dil_tile(q_r[0, rows(start), :], k_r[0, rows(start), :], v_r[0, rows(start), :],
                               kp, vp, bias_ref, gi, prev_off)
            osc[gi, rows(start), :] = o
            lsc[gi, rows(start), :] = lse

        def body(it, carry, dil=dil, n_t=n_t, tile=tile):
            rho, t = it // n_t, it % n_t
            start = rho + t * (DT * dil)
            prev = rho + jnp.maximum(t - 1, 0) * (DT * dil)
            tile(start, prev, jnp.where(t == 0, NEG, 0.0))
            return carry

        lax.fori_loop(0, dil * n_t, body, 0, unroll=DIL_UNROLL)
    outs = _dil_combine([osc[0], osc[1], osc[2]], [lsc[0], lsc[1], lsc[2]])
    for pair in range(2):
        @pl.when(pl.program_id(1) == pair)
        def _(pair=pair):
            for gi, o in enumerate(outs):
                c = 2 * gi + pair
                y_ref[0, :, c * 128:(c + 1) * 128] = o


def _dil_prompt(zd, bias_tiles, b, s):
    col = lambda base: [pl.BlockSpec((1, s, 128), lambda i, j, c=base + 2 * gi: (i, 0, c + j)) for gi in range(3)]
    return pl.pallas_call(
        functools.partial(_dil_prompt_body, s=s),
        out_shape=SDS((b, s, ATT), f32), grid=(b, 2),
        in_specs=col(0) + col(6) + col(12) + [pl.BlockSpec((3, 1, 2, 2, DT, DT), lambda i, j: (0, j, 0, 0, 0, 0))],
        out_specs=pl.BlockSpec((1, s, ATT), lambda i, j: (i, 0, 0)),
        scratch_shapes=[pltpu.VMEM((3, s, 128), f32), pltpu.VMEM((3, s, 128), f32)],
        compiler_params=_cp(("parallel", "arbitrary"), 48), name="dil_prompt")(*([zd] * 9), bias_tiles)


def _dil_sample_body(q_ref, s0, s1, s2, new_ref, b0, b1, b2, y_ref, n0, n1, n2, a0, a1, a2, *, n_q):
    lane = lax.broadcasted_iota(jnp.int32, (n_q, 256), 1)
    os_, ls_ = [], []
    for gi, (st, bias, nst, alls) in enumerate(((s0, b0, n0, a0), (s1, b1, n1, a1), (s2, b2, n2, a2))):
        w = st.shape[2]
        alls[...] = jnp.zeros(alls.shape, f32)
        alls[0:w, :] = st[0, 0]
        alls[w:w + n_q, :] = new_ref[0, gi]
        nst[0] = alls[n_q:w + n_q, :]
        s = _nt(q_ref[0, gi], alls[:, 0:256].astype(bf16)) + bias[...]
        m = jnp.max(s, axis=-1, keepdims=True)
        e = jnp.exp(s - m)
        l = jnp.maximum(jnp.sum(e, axis=-1, keepdims=True), 1e-30)
        res = _dot(e.astype(bf16), alls[:, 256:512].astype(bf16)) / l
        lse = m + jnp.log(l)
        o = jnp.zeros((n_q, 256), f32)
        lm = jnp.zeros((n_q, 256), f32)
        for hg in range(4):
            inb = (lane >= hg * DH) & (lane < (hg + 1) * DH)
            o = jnp.where(inb, res[hg * n_q:(hg + 1) * n_q, :], o)
            lm = jnp.where(inb, lse[hg * n_q:(hg + 1) * n_q, :], lm)
        os_.append(o)
        ls_.append(lm)
    for gi, o in enumerate(_dil_combine(os_, ls_)):
        y_ref[0, :, gi * 256:(gi + 1) * 256] = o


def _dil_sample(qbd, states, li, new_kv, biases):
    db, n_q = new_kv.shape[0], new_kv.shape[2]
    per_b = lambda shape: pl.BlockSpec((1,) + shape, lambda i: (i,) + (0,) * len(shape))
    wins = [st.shape[2] for st in states]
    return pl.pallas_call(
        functools.partial(_dil_sample_body, n_q=n_q),
        out_shape=(SDS((db, n_q, ATT), f32),) + tuple(SDS((db, w, 512), f32) for w in wins), grid=(db,),
        in_specs=[per_b((3, 4 * n_q, 256))]
                 + [pl.BlockSpec((1, 1, w, 512), lambda i: (li, i, 0, 0)) for w in wins]
                 + [per_b((3, n_q, 512))] + [_resident(bb.shape) for bb in biases],
        out_specs=(per_b((n_q, ATT)),) + tuple(per_b((w, 512)) for w in wins),
        scratch_shapes=[pltpu.VMEM((w + PAGE, 512), f32) for w in wins],
        compiler_params=_cp(("parallel",), 48), name="dil_sample")(qbd, *states, new_kv, *biases)


def _rel_bucket(dist):
    dist = jnp.maximum(dist, 0)
    d32 = jnp.maximum(dist, 1).astype(f32)
    large = 16 + (jnp.log(d32 / 16) / math.log(2048 / 16) * 16).astype(jnp.int32)
    return jnp.where(dist < 16, dist, jnp.minimum(large, 31))


def _bias_rev(tbl, n, keep, scale=1):
    dist = (n - 1 - jnp.arange(n)) * scale
    return jnp.where(keep(dist), tbl[_rel_bucket(dist)].T, NEG)


def _toeplitz_tiles(bdr, n_delta, t):
    hn = bdr.shape[0]
    bdp = jnp.concatenate([bdr, jnp.full((hn, t - 1), NEG, f32)], axis=1)
    lp = bdp.shape[1]
    gq = jnp.stack([bdp[:, lp + 1 - d * t - 2 * t:lp - d * t] for d in range(n_delta)])
    gq = jnp.pad(gq, ((0, 0), (0, 0), (0, 1)))
    x = jnp.broadcast_to(gq[:, :, None, :], (n_delta, hn, t, 2 * t)).reshape(n_delta, hn, t * 2 * t)
    y = x[..., :t * (2 * t - 1)].reshape(n_delta, hn, t, 2 * t - 1)
    return y[..., t - 1:]


def _rows_by_query(bdr, n_q, n_keys, first_dist):
    last = bdr.shape[1] - 1
    return jnp.stack([bdr[:, last - first_dist - i:last - first_dist - i + n_keys] for i in range(n_q)], axis=1)


def _new_token_bias(tbl, n_q, keep):
    tri = jnp.arange(n_q)[:, None] - jnp.arange(PAGE)[None, :]
    ok = (tri >= 0) & keep(tri)
    return jnp.where(ok[None], jnp.transpose(tbl[_rel_bucket(tri)], (2, 0, 1)), NEG)


def _nsa_tables(rel_bias, s, n_q, past):
    everywhere = lambda d: d >= 0
    in_window = lambda d: d < WIN
    n = past + n_q
    key_blk = jnp.arange(s // TQ)[:, None, None] * (TQ // SEL_BLOCK) + jnp.arange(TQ)[None, :, None] // SEL_BLOCK
    bias_new = _new_token_bias(rel_bias, n_q, everywhere).reshape(H * n_q, PAGE)
    bias_win = _rows_by_query(_bias_rev(rel_bias, n, in_window), n_q, WIN, WIN).reshape(H * n_q, WIN)
    return dict(
        perm=_chunk_perm(),
        bias_s=_toeplitz_tiles(_bias_rev(rel_bias, s, everywhere), s // TQ, TQ) * LOG2E,
        bias_w=_toeplitz_tiles(_bias_rev(rel_bias, WIN + TQ, in_window), WIN // TQ + 1, TQ) * LOG2E,
        et_tiles=(key_blk == jnp.arange(128)[None, None, :]).astype(bf16),
        e_full=(jnp.arange(past // SEL_BLOCK)[:, None] == jnp.arange(past)[None, :] // SEL_BLOCK).astype(bf16),
        bias_slc=_rows_by_query(_bias_rev(rel_bias, n, everywhere), n_q, past, past).reshape(H * n_q, past),
        bias_new=bias_new,
        bias_win=jnp.concatenate([bias_win, bias_new], axis=1))


def _nsa_layer(yp, ys, g_pre, g_post, w_in, w_out, cmp_pos, cmp_w1, cmp_w2, tb, pool, page_table, win_states,
               li, b, s, db, n_q, past):
    w_pad = jnp.pad(w_in, ((0, 0), (0, NSA_PAD - w_in.shape[1]))).astype(bf16)
    w_out = w_out.astype(bf16)
    w1cat = jnp.concatenate([cmp_w1[:, :1024], cmp_w1[:, 1024:]], axis=-1).astype(bf16)
    pos2 = cmp_pos.reshape(2, 2, 1024)
    w2 = cmp_w2.astype(bf16)

    q_hm, q2_hm, kvt, kwin, ks, vs, kw, vw, gates = _nsa_proj_prompt(yp, g_pre, w_pad, b, s)
    kc, vc = _compress_prompt(kvt, tb["perm"], w1cat, pos2, w2, b, s)
    o_cmp, selneg = _cmp_sel_prompt(q_hm, kc, vc, gates, b, s)
    o_sw = _flash_prompt(q2_hm, ks, vs, kw, vw, tb["bias_s"], tb["bias_w"], selneg, tb["et_tiles"], gates, b, s)
    yp = _out_proj(yp, g_post, w_out, o_cmp.reshape(b * s, ATT), o_sw.reshape(b * s, ATT))
    kv_p = jnp.transpose(kvt.reshape(b, 4, G, DH, s), (0, 4, 1, 2, 3))
    win_p = kwin.reshape(b, s, 2, G, DH)[:, s - min(WIN, s):]

    z = _proj(ys, g_pre, w_pad)
    q = (z[:, :ATT] * (DH ** -0.5)).reshape(db, n_q, H, DH).transpose(0, 2, 1, 3)
    qg = q.reshape(db, G, R * n_q, DH).astype(bf16)
    eye = jnp.eye(G, dtype=f32)
    qbd = (q.reshape(db, G, R * n_q, 1, DH) * eye[None, :, None, :, None]).reshape(db, H * n_q, G * DH).astype(bf16)
    kv_new = z[:, ATT:ATT + 1024].reshape(db, n_q, 1024)
    win_new = z[:, ATT + 1024:ATT + 1536].reshape(db, n_q, 512)
    gt = z[:, ATT + 1536:ATT + 1536 + 3 * H].reshape(db, n_q, 3, H).transpose(0, 3, 1, 2)
    gsw = jnp.pad(gt.reshape(db, H * n_q, 3), ((0, 0), (0, 0), (0, 125)))
    kc_s, vc_s = _compress_sample(pool, page_table, tb["perm"], w1cat, pos2, w2)
    o_cmp_s, negmask = _cmp_sel_sample(qg, kc_s, vc_s, gsw, tb["e_full"], past)
    o_sw_s = _flash_sample(pool, page_table, qbd, tb["bias_slc"], negmask, kv_new[:, :, 512:], tb["bias_new"],
                           win_states.reshape(win_states.shape[0], db, WIN, 512), li, win_new, tb["bias_win"], gsw)
    ys = _out_proj(ys, g_post, w_out, o_cmp_s.reshape(db * n_q, ATT), o_sw_s.reshape(db * n_q, ATT))
    kv_s = kv_new.reshape(db, n_q, 4, G, DH)
    win_s = jnp.concatenate([win_states[li, :, n_q:], win_new.reshape(db, n_q, 2, G, DH)], axis=1)
    return yp, ys, kv_p, win_p, kv_s, win_s


def _dil_tables(rel_bias, n_q):
    tiles, biases = [], []
    for gi, (win, dil) in enumerate(DIL):
        tbl = rel_bias[:, gi * 4:(gi + 1) * 4]
        on_grid = lambda d, win=win, dil=dil: (d % dil == 0) & (d <= win)
        tiles.append(_toeplitz_tiles(_bias_rev(tbl, 2 * DT, on_grid, scale=dil), 2, DT))
        b_all = jnp.concatenate([_rows_by_query(_bias_rev(tbl, win + n_q, on_grid), n_q, win, win),
                                 _new_token_bias(tbl, n_q, on_grid)], axis=2)
        biases.append(b_all.reshape(4 * n_q, win + PAGE))
    return dict(tiles=jnp.stack(tiles).transpose(0, 2, 1, 3, 4).reshape(3, 2, 2, 2, DT, DT), sample=biases)


def _dil_layer(yp, ys, g_pre, g_post, w_in, w_out, tb, states, li, b, s, db, n_q):
    w_in = w_in.astype(bf16)
    w_out = w_out.astype(bf16)

    zd, kvt = _dil_proj(yp, g_pre, w_in, b, s)
    y = _dil_prompt(zd.reshape(b, s, 3 * ATT), tb["tiles"], b, s)
    yp = _out_proj(yp, g_post, w_out, y.reshape(b * s, ATT))
    kvt = kvt.reshape(b, 2, H, DH, s)
    st_p = []
    for gi, (win, _) in enumerate(DIL):
        keep = min(win, s)
        st_p.append(jnp.transpose(kvt[:, :, gi * 4:(gi + 1) * 4, :, s - keep:], (0, 4, 1, 2, 3)))

    z = _proj(ys, g_pre, w_in).reshape(db, n_q, 3, 3, 4, DH)
    q = (z[:, :, 0] * (DH ** -0.5)).transpose(0, 2, 3, 1, 4)
    eye = jnp.eye(4, dtype=f32)
    qbd = (q[:, :, :, :, None, :] * eye[None, None, :, None, :, None]).reshape(db, 3, 4 * n_q, 256).astype(bf16)
    new_kv = jnp.concatenate([z[:, :, 1].transpose(0, 2, 1, 3, 4).reshape(db, 3, n_q, 256),
                              z[:, :, 2].transpose(0, 2, 1, 3, 4).reshape(db, 3, n_q, 256)], axis=-1)
    flat = [st.reshape(st.shape[0], db, st.shape[2], 512) for st in states]
    y_s, n0, n1, n2 = _dil_sample(qbd, flat, li, new_kv, tb["sample"])
    ys = _out_proj(ys, g_post, w_out, y_s.reshape(db * n_q, ATT))
    st_s = [n.reshape(st.shape[1:]) for n, st in zip((n0, n1, n2), states)]
    return yp, ys, st_p, st_s


def kernel(x_prompt, x_sample, cache_nsa_kv, page_table, state_nsa_win, state_dil_w128, state_dil_w512,
           state_dil_w2048, norm_g, ffn_w_in, ffn_w_out, nsa_w_in, nsa_w_out, nsa_cmp_pos, nsa_cmp_w1,
           nsa_cmp_w2, dil_w_in, dil_w_out, rel_bias):
    b, s, _ = x_prompt.shape
    db, n_q, _ = x_sample.shape
    depth = norm_g.shape[0]
    past = page_table.shape[1] * PAGE
    yp = x_prompt.reshape(b * s, D)
    ys = x_sample.reshape(db * n_q, D)
    nsa_kv_p, nsa_win_p, nsa_kv_s, nsa_win_s = [], [], [], []
    dil_p, dil_s = [[], [], []], [[], [], []]
    nsa_tb = _nsa_tables(rel_bias, s, n_q, past)
    dil_tb = _dil_tables(rel_bias, n_q)
    for i in range(depth):
        li = i // 2
        g = norm_g[i]
        w_i, w_o = ffn_w_in[i, 0].astype(bf16), ffn_w_out[i, 0].astype(bf16)
        yp = _half_ffn(yp, g[0:2], w_i, w_o)
        ys = _half_ffn(ys, g[0:2], w_i, w_o)
        if i % 2 == 0:
            n_pool = cache_nsa_kv.shape[1]
            pool = jnp.transpose(cache_nsa_kv, (0, 1, 3, 4, 5, 2)).reshape(
                cache_nsa_kv.shape[0] * n_pool, 4 * G, DH, PAGE)
            yp, ys, kvp, winp, kvs, wins = _nsa_layer(
                yp, ys, g[2:3], g[3:4], nsa_w_in[li], nsa_w_out[li], nsa_cmp_pos[li], nsa_cmp_w1[li],
                nsa_cmp_w2[li], nsa_tb, pool, page_table + li * n_pool, state_nsa_win, li, b, s, db, n_q, past)
            nsa_kv_p.append(kvp)
            nsa_win_p.append(winp)
            nsa_kv_s.append(kvs)
            nsa_win_s.append(wins)
        else:
            states = (state_dil_w128, state_dil_w512, state_dil_w2048)
            yp, ys, stp, sts = _dil_layer(yp, ys, g[2:3], g[3:4], dil_w_in[li], dil_w_out[li], dil_tb, states,
                                          li, b, s, db, n_q)
            for gi in range(3):
                dil_p[gi].append(stp[gi])
                dil_s[gi].append(sts[gi])
        w_i, w_o = ffn_w_in[i, 1].astype(bf16), ffn_w_out[i, 1].astype(bf16)
        yp = _half_ffn(yp, g[4:6], w_i, w_o)
        ys = _half_ffn(ys, g[4:6], w_i, w_o)
    return (yp.reshape(b, s, D), ys.reshape(db, n_q, D),
            jnp.stack(nsa_kv_p), jnp.stack(nsa_win_p),
            jnp.stack(dil_p[0]), jnp.stack(dil_p[1]), jnp.stack(dil_p[2]),
            jnp.stack(nsa_kv_s), jnp.stack(nsa_win_s),
            jnp.stack(dil_s[0]), jnp.stack(dil_s[1]), jnp.stack(dil_s[2]))
```

```python
import functools
import math

import jax
import jax.numpy as jnp
from jax import lax
from jax.experimental import pallas as pl
from jax.experimental.pallas import tpu as pltpu

f32, bf16 = jnp.float32, jnp.bfloat16
SDS = jax.ShapeDtypeStruct

D = 1024
FF = 2816
FF_CHUNK = 256
H, DH = 12, 64
G, R = 4, 3
ATT = H * DH
EPS = 1e-6
NEG = -1e30
LOG2E = math.log2(math.e)
FORCE = 1e4
SEL_BLOCK, TOPK = 64, 16
WIN = 512
NSA_PAD = 2432
TQ = 256
RC = 32
PAGE = 128
CHUNK_PAGES = 8
CMP_GROUP = 4
DIL = ((128, 1), (512, 4), (2048, 16))
DT = 128
DIL_UNROLL = 4
MIB = 1024 * 1024


def _cp(sem, vmem_mib):
    return pltpu.CompilerParams(dimension_semantics=sem, vmem_limit_bytes=vmem_mib * MIB)


def _rms(x, g):
    return x * lax.rsqrt(jnp.mean(x * x, axis=-1, keepdims=True) + EPS) * g


def _nt(a, b):
    return lax.dot_general(a, b, (((1,), (1,)), ((), ())), preferred_element_type=f32)


def _dot(a, b):
    return jnp.dot(a, b, preferred_element_type=f32)


def _resident(shape):
    nd = len(shape)
    return pl.BlockSpec(shape, lambda *_: (0,) * nd, pipeline_mode=pl.Buffered(1))


def _ffn_body(x_ref, g_ref, win_ref, wout_ref, o_ref):
    x = x_ref[...]
    h = _rms(x, g_ref[0:1, :]).astype(bf16)
    acc = jnp.zeros(x.shape, f32)
    for c in range(FF // FF_CHUNK):
        lo, hi = c * FF_CHUNK, (c + 1) * FF_CHUNK
        gate = _dot(h, win_ref[:, lo:hi])
        up = _dot(h, win_ref[:, FF + lo:FF + hi])
        act = (gate * jax.nn.sigmoid(gate) * up).astype(bf16)
        acc = acc + _dot(act, wout_ref[lo:hi, :])
    o_ref[...] = x + 0.5 * _rms(acc, g_ref[1:2, :])


def _half_ffn(x, g2, w_in, w_out):
    m = x.shape[0]
    tm = min(m, 512)
    return pl.pallas_call(
        _ffn_body, out_shape=SDS((m, D), f32), grid=(m // tm,),
        in_specs=[pl.BlockSpec((tm, D), lambda i: (i, 0)), _resident((2, D)),
                  _resident((D, 2 * FF)), _resident((FF, D))],
        out_specs=pl.BlockSpec((tm, D), lambda i: (i, 0)),
        compiler_params=_cp(("parallel",), 48), name="half_ffn")(x, g2, w_in, w_out)


def _proj_body(x_ref, g_ref, w_ref, o_ref):
    h = _rms(x_ref[...], g_ref[...]).astype(bf16)
    o_ref[...] = _dot(h, w_ref[...])


def _proj(x, g, w):
    m, n = x.shape[0], w.shape[1]
    return pl.pallas_call(
        _proj_body, out_shape=SDS((m, n), f32), grid=(1,),
        in_specs=[_resident((m, D)), _resident((1, D)), _resident((D, n))],
        out_specs=pl.BlockSpec((m, n), lambda i: (0, 0)),
        compiler_params=_cp(("arbitrary",), 32), name="proj_sample")(x, g, w)


def _nsa_proj_body(x_ref, g_ref, w_ref, q_ref, q2_ref, kv_ref, kwin_ref, ks_ref, vs_ref, kw_ref, vw_ref, gt_ref):
    h = _rms(x_ref[...], g_ref[...]).astype(bf16)
    zq = _dot(h, w_ref[:, 0:ATT]) * (DH ** -0.5)
    zq2 = zq * LOG2E
    for hh in range(H):
        q_ref[0, hh] = zq[:, hh * DH:(hh + 1) * DH].astype(bf16)
        q2_ref[0, hh] = zq2[:, hh * DH:(hh + 1) * DH].astype(bf16)
    zc = _dot(h, w_ref[:, ATT:ATT + 512])
    kv_ref[0, 0:2 * G] = zc.T.reshape(2 * G, DH, zc.shape[0])
    zs = _dot(h, w_ref[:, ATT + 512:ATT + 1024])
    kv_ref[0, 2 * G:4 * G] = zs.T.reshape(2 * G, DH, zs.shape[0])
    zw = _dot(h, w_ref[:, ATT + 1024:ATT + 1536])
    kwin_ref[...] = zw
    ones = jnp.ones((zs.shape[0], DH), bf16)
    for g in range(G):
        ks_ref[0, g] = zs[:, g * DH:(g + 1) * DH].astype(bf16)
        vs_ref[0, g] = jnp.concatenate([zs[:, 256 + g * DH:256 + (g + 1) * DH].astype(bf16), ones], axis=1)
        kw_ref[0, g] = zw[:, g * DH:(g + 1) * DH].astype(bf16)
        vw_ref[0, g] = jnp.concatenate([zw[:, 256 + g * DH:256 + (g + 1) * DH].astype(bf16), ones], axis=1)
    gt_ref[...] = _dot(h, w_ref[:, ATT + 1536:NSA_PAD])


def _nsa_proj_prompt(x, g, w, b, s):
    tm = 512
    nt = s // tm
    hm = lambda n, w_: pl.BlockSpec((1, n, tm, w_), lambda i: (i // nt, 0, i % nt, 0))
    rows = lambda n: pl.BlockSpec((tm, n), lambda i: (i, 0))
    return pl.pallas_call(
        _nsa_proj_body,
        out_shape=(SDS((b, H, s, DH), bf16), SDS((b, H, s, DH), bf16), SDS((b, 4 * G, DH, s), f32),
                   SDS((b * s, 512), f32), SDS((b, G, s, DH), bf16), SDS((b, G, s, 2 * DH), bf16),
                   SDS((b, G, s, DH), bf16), SDS((b, G, s, 2 * DH), bf16), SDS((b * s, 128), f32)),
        grid=(b * s // tm,),
        in_specs=[rows(D), _resident((1, D)), _resident((D, NSA_PAD))],
        out_specs=(hm(H, DH), hm(H, DH), pl.BlockSpec((1, 4 * G, DH, tm), lambda i: (i // nt, 0, 0, i % nt)),
                   rows(512), hm(G, DH), hm(G, 2 * DH), hm(G, DH), hm(G, 2 * DH), rows(128)),
        compiler_params=_cp(("parallel",), 48), name="nsa_proj_prompt")(x, g, w)


def _dil_proj_body(x_ref, g_ref, w_ref, o_ref, kvt_ref):
    h = _rms(x_ref[...], g_ref[...]).astype(bf16)
    o_ref[:, 0:ATT] = _dot(h, w_ref[:, 0:ATT]) * (DH ** -0.5)
    for part in (1, 2):
        z = _dot(h, w_ref[:, part * ATT:(part + 1) * ATT])
        o_ref[:, part * ATT:(part + 1) * ATT] = z
        kvt_ref[0, (part - 1) * H:part * H] = z.T.reshape(H, DH, z.shape[0])


def _dil_proj(x, g, w, b, s):
    tm = 512
    nt = s // tm
    return pl.pallas_call(
        _dil_proj_body, out_shape=(SDS((b * s, 3 * ATT), f32), SDS((b, 2 * H, DH, s), f32)), grid=(b * s // tm,),
        in_specs=[pl.BlockSpec((tm, D), lambda i: (i, 0)), _resident((1, D)), _resident((D, 3 * ATT))],
        out_specs=(pl.BlockSpec((tm, 3 * ATT), lambda i: (i, 0)),
                   pl.BlockSpec((1, 2 * H, DH, tm), lambda i: (i // nt, 0, 0, i % nt))),
        compiler_params=_cp(("parallel",), 48), name="dil_proj")(x, g, w)


def _out_proj_body(x_ref, g_ref, w_ref, *refs):
    o_ref = refs[-1]
    o = refs[0][...]
    for r in refs[1:-1]:
        o = o + r[...]
    y = _dot(o.astype(bf16), w_ref[...])
    o_ref[...] = x_ref[...] + _rms(y, g_ref[...])


def _out_proj(x, g, w, *branches):
    m = x.shape[0]
    tm = min(m, 512)
    row = lambda n: pl.BlockSpec((tm, n), lambda i: (i, 0))
    return pl.pallas_call(
        _out_proj_body, out_shape=SDS((m, D), f32), grid=(m // tm,),
        in_specs=[row(D), _resident((1, D)), _resident((ATT, D))] + [row(ATT)] * len(branches),
        out_specs=row(D), compiler_params=_cp(("parallel",), 32), name="out_proj")(x, g, w, *branches)


def _compress_body(*refs, n_src, n_pages, paged, group=1):
    if paged:
        refs = refs[1:]
    srcs = refs[:n_src]
    perm_ref, w1_ref, pos_ref, w2_ref, kc_ref, vc_ref, lhs_ref, aprev_ref = refs[n_src:]
    nch = n_pages * PAGE // 16
    gn = group * nch
    step = pl.program_id(1) if paged else 0
    sub = step % group if paged else 0

    for p in range(n_pages):
        mats = srcs[p][0] if paged else srcs[0][0, :, :, p * PAGE:(p + 1) * PAGE]
        both = jnp.concatenate([jnp.concatenate([mats[2 * c], mats[2 * c + 1]], axis=1) for c in range(4)], axis=0)
        out = _dot(both.astype(bf16), perm_ref[...])
        for c in range(4):
            kind, pair = c // 2, c % 2
            o = out[c * DH:(c + 1) * DH]
            zt = jnp.concatenate([o[:, 0:128], o[:, 128:256]], axis=0).T
            for gl in range(2):
                start = (2 * pair + gl) * gn + sub * nch + p * 8
                dst = pl.ds(pl.multiple_of(start, 8), 8) if paged else pl.ds(start, 8)
                for r2 in range(8):
                    lhs_ref[kind, dst, r2 * 128:(r2 + 1) * 128] = zt[(gl * 8 + r2) * 8:(gl * 8 + r2 + 1) * 8, :]

    def mlp():
        first = (step < group) if paged else True
        if paged:
            @pl.when(step == group - 1)
            def _():
                aprev_ref[...] = jnp.zeros(aprev_ref.shape, f32)

        row = lax.broadcasted_iota(jnp.int32, (gn, 256), 0)
        row_o = lax.broadcasted_iota(jnp.int32, (gn, DH), 0)
        for kind, out_ref in ((0, kc_ref), (1, vc_ref)):
            pp = _dot(lhs_ref[kind].astype(bf16), w1_ref[kind])
            pos = _dot(pos_ref[kind].astype(bf16), w1_ref[kind])
            posterm = pos[0:1, 0:256] + pos[1:2, 256:512]
            for g in range(G):
                a = pp[g * gn:(g + 1) * gn, 0:256]
                bm = pp[g * gn:(g + 1) * gn, 256:512]
                shifted = pltpu.roll(a, 1, 0)
                if paged:
                    carry = aprev_ref[kind, (g + 1) * gn - 1:(g + 1) * gn, :]
                    shifted = jnp.where(row == 0, carry, shifted)
                    aprev_ref[kind, g * gn:(g + 1) * gn, :] = a
                pre = shifted + bm + posterm
                hid = (pre * jax.nn.sigmoid(pre)).astype(bf16)
                out = _dot(hid, w2_ref[kind])
                out = jnp.where((row_o == 0) & first, 0.0, out)
                out_ref[0, g] = out.astype(bf16)

    if paged:
        pl.when(sub == group - 1)(mlp)
    else:
        mlp()


def _chunk_perm():
    t = jnp.arange(PAGE)
    col = (t % 2) * 128 + ((t % 16) // 2) * 8 + t // 16
    src = jnp.concatenate([col, col + DH])
    return (src[:, None] == jnp.arange(256)[None, :]).astype(bf16)


def _compress_prompt(kvt, perm, w1cat, pos2, w2, b, s):
    nch = s // 16
    out = SDS((b, G, nch, DH), bf16)
    ospec = pl.BlockSpec((1, G, nch, DH), lambda i: (i, 0, 0, 0))
    return pl.pallas_call(
        functools.partial(_compress_body, n_src=1, n_pages=s // PAGE, paged=False),
        out_shape=(out, out), grid=(b,),
        in_specs=[pl.BlockSpec((1, 8, DH, s), lambda i: (i, 0, 0, 0)), _resident((256, 256)),
                  _resident((2, 1024, 512)), _resident((2, 2, 1024)), _resident((2, 256, DH))],
        out_specs=(ospec, ospec),
        scratch_shapes=[pltpu.VMEM((2, G * nch, 1024), f32), pltpu.VMEM((2, G * nch, 256), f32)],
        compiler_params=_cp(("parallel",), 48), name="compress_prompt")(kvt, perm, w1cat, pos2, w2)


def _compress_sample(pool_t, page_table, perm, w1cat, pos2, w2):
    db, n_pages = page_table.shape
    steps = n_pages // CHUNK_PAGES
    gn = CMP_GROUP * CHUNK_PAGES * PAGE // 16
    out = SDS((db, G, n_pages * PAGE // 16, DH), bf16)
    ospec = pl.BlockSpec((1, G, gn, DH), lambda i, j, pt: (i, 0, j // CMP_GROUP, 0))
    const = lambda shape: pl.BlockSpec(shape, lambda i, j, pt: (0,) * len(shape), pipeline_mode=pl.Buffered(1))
    page_specs = [pl.BlockSpec((1, 8, DH, PAGE), lambda i, j, pt, k=k: (pt[i, j * CHUNK_PAGES + k], 0, 0, 0))
                  for k in range(CHUNK_PAGES)]
    return pl.pallas_call(
        functools.partial(_compress_body, n_src=CHUNK_PAGES, n_pages=CHUNK_PAGES, paged=True, group=CMP_GROUP),
        out_shape=(out, out),
        grid_spec=pltpu.PrefetchScalarGridSpec(
            num_scalar_prefetch=1, grid=(db, steps),
            in_specs=page_specs + [const((256, 256)), const((2, 1024, 512)), const((2, 2, 1024)),
                                   const((2, 256, DH))],
            out_specs=(ospec, ospec),
            scratch_shapes=[pltpu.VMEM((2, G * gn, 1024), f32), pltpu.VMEM((2, G * gn, 256), f32)]),
        compiler_params=_cp(("arbitrary", "arbitrary"), 40), name="compress_sample")(
            page_table, *([pool_t] * CHUNK_PAGES), perm, w1cat, pos2, w2)


def _block_scores(impt_ref, n_blocks):
    return (impt_ref[pl.ds(1, n_blocks, stride=4), :] + impt_ref[pl.ds(2, n_blocks, stride=4), :]
            + impt_ref[pl.ds(3, n_blocks, stride=4), :])


def _force_scores(v, q_pos):
    j = lax.broadcasted_iota(jnp.int32, v.shape, 0)
    cur = q_pos >> 6
    forced = (j == 0) | (j == cur) | (j == cur - 1)
    return jnp.where(forced, FORCE, jnp.where(j > cur, -FORCE, v)), j


def _cmp_sel_prompt_body(q_ref, kc_ref, vc_ref, gt_ref, o_ref, sel_ref, impt_ref, *, n_sel):
    qt = pl.program_id(1)
    sig = jax.nn.sigmoid(gt_ref[...])
    t = qt * TQ + lax.broadcasted_iota(jnp.int32, (TQ, 128), 0)
    mp = lax.broadcasted_iota(jnp.int32, (TQ, 128), 1)
    valid = (mp >= 1) & (16 * mp + 15 <= t)
    for g in range(G):
        kc, vc = kc_ref[0, g], vc_ref[0, g]
        imp = jnp.zeros((TQ, 128), f32)
        for r in range(R):
            hh = g * R + r
            s = jnp.where(valid, _nt(q_ref[0, hh], kc), NEG)
            m = jnp.max(s, axis=-1, keepdims=True)
            e = jnp.where(valid, jnp.exp(s - m), 0.0)
            p = e / jnp.maximum(jnp.sum(e, axis=-1, keepdims=True), 1e-30)
            imp = imp + p
            o_ref[0, :, hh * DH:(hh + 1) * DH] = _dot(p.astype(bf16), vc) * sig[:, hh:hh + 1]
        for half in range(TQ // 128):
            impt_ref[...] = imp[half * 128:(half + 1) * 128, :].T
            v = _block_scores(impt_ref, n_sel)
            q_pos = qt * TQ + half * 128 + lax.broadcasted_iota(jnp.int32, v.shape, 1)
            val, j = _force_scores(v, q_pos)
            rank = jnp.zeros(v.shape, jnp.int32)
            for k in range(n_sel):
                vk = val[k:k + 1, :]
                rank = rank + jnp.where((vk > val) | ((vk == val) & (j > k)), 1, 0)
            neg = jnp.where(rank < TOPK, 0.0, NEG)
            full = jnp.concatenate([neg, jnp.zeros((128 - n_sel, 128), f32)], axis=0)
            sel_ref[0, g, half * 128:(half + 1) * 128, :] = full.T.astype(bf16)


def _cmp_sel_prompt(q_hm, kc, vc, gates, b, s):
    nq = s // TQ
    return pl.pallas_call(
        functools.partial(_cmp_sel_prompt_body, n_sel=s // SEL_BLOCK),
        out_shape=(SDS((b, s, ATT), f32), SDS((b, G, s, 128), bf16)), grid=(b, nq),
        in_specs=[pl.BlockSpec((1, H, TQ, DH), lambda i, j: (i, 0, j, 0)),
                  pl.BlockSpec((1, G, 128, DH), lambda i, j: (i, 0, 0, 0)),
                  pl.BlockSpec((1, G, 128, DH), lambda i, j: (i, 0, 0, 0)),
                  pl.BlockSpec((TQ, 128), lambda i, j: (i * nq + j, 0))],
        out_specs=(pl.BlockSpec((1, TQ, ATT), lambda i, j: (i, j, 0)),
                   pl.BlockSpec((1, G, TQ, 128), lambda i, j: (i, 0, j, 0))),
        scratch_shapes=[pltpu.VMEM((128, 128), f32)],
        compiler_params=_cp(("parallel", "parallel"), 32), name="cmp_sel_prompt")(q_hm, kc, vc, gates)


def _online_update(s, v, m_ref, l_ref, acc_ref, idx):
    m_old = m_ref[idx]
    m_new = jnp.maximum(m_old, jnp.max(s, axis=-1, keepdims=True))
    alpha = jnp.exp(m_old - m_new)
    p = jnp.exp(s - m_new)
    l_ref[idx] = alpha * l_ref[idx] + jnp.sum(p, axis=-1, keepdims=True)
    acc_ref[idx] = alpha * acc_ref[idx] + _dot(p.astype(bf16), v)
    m_ref[idx] = m_new


def _group_update(qa, ka, va, bias_ref, delta, h0, m_ref, acc_ref, idx0, s_ref, p_ref, a_ref):
    s_ref[...] = _nt(qa, ka)
    for r in range(R):
        for c in range(TQ // RC):
            lo = c * RC
            s = s_ref[r * TQ + lo:r * TQ + lo + RC, :] + bias_ref[delta, h0 + r, lo:lo + RC, :]
            m_old = m_ref[idx0 + r, lo:lo + RC, :]
            m_new = jnp.maximum(m_old, jnp.max(s, axis=-1, keepdims=True))
            p_ref[r * TQ + lo:r * TQ + lo + RC, :] = jnp.exp2(
                s - jnp.concatenate([m_new] * (TQ // 128), axis=1)).astype(bf16)
            m_ref[idx0 + r, lo:lo + RC, :] = m_new
            a_ref[r * TQ + lo:r * TQ + lo + RC, :] = jnp.exp2(m_old - m_new)
    pv = _dot(p_ref[...], va)
    for r in range(R):
        acc_ref[idx0 + r] = a_ref[r * TQ:(r + 1) * TQ, :] * acc_ref[idx0 + r] + pv[r * TQ:(r + 1) * TQ, :]


def _flash_prompt_body(q_ref, ks_ref, vs_ref, kw_ref, vw_ref, bs_ref, bw_ref, sel_ref, et_ref, gt_ref, o_ref,
                       m_ref, acc_ref, s_ref, p_ref, a_ref):
    qt, kt = pl.program_id(1), pl.program_id(2)
    delta = qt - kt
    scratch = (s_ref, p_ref, a_ref)

    @pl.when(kt == 0)
    def _():
        m_ref[...] = jnp.full(m_ref.shape, NEG, f32)
        acc_ref[...] = jnp.zeros(acc_ref.shape, f32)

    @pl.when(kt <= qt)
    def _():
        for g in range(G):
            q3 = q_ref[0, g * R:(g + 1) * R].reshape(R * TQ, DH)
            qa = jnp.concatenate([jnp.concatenate([sel_ref[0, g]] * R, axis=0), q3], axis=1)
            ka = jnp.concatenate([et_ref[0], ks_ref[0, g]], axis=1)
            _group_update(qa, ka, vs_ref[0, g], bs_ref, delta, g * R, m_ref, acc_ref, g * R, *scratch)

    @pl.when((kt <= qt) & (kt >= qt - (WIN // TQ)))
    def _():
        for g in range(G):
            q3 = q_ref[0, g * R:(g + 1) * R].reshape(R * TQ, DH)
            _group_update(q3, kw_ref[0, g], vw_ref[0, g], bw_ref, delta, g * R, m_ref, acc_ref, H + g * R, *scratch)

    @pl.when(kt == qt)
    def _():
        sig = jax.nn.sigmoid(gt_ref[...])
        for hh in range(H):
            a_s, a_w = acc_ref[hh], acc_ref[H + hh]
            o_s = a_s / jnp.maximum(pltpu.roll(a_s, DH, 1), 1e-30)
            o_w = a_w / jnp.maximum(pltpu.roll(a_w, DH, 1), 1e-30)
            o = sig[:, H + hh:H + hh + 1] * o_s + sig[:, 2 * H + hh:2 * H + hh + 1] * o_w
            o_ref[0, :, hh * DH:(hh + 1) * DH] = o[:, 0:DH]


def _flash_prompt(q2, ks, vs, kw, vw, bias_s, bias_w, selneg, et_tiles, gates, b, s):
    nq = s // TQ
    nw = WIN // TQ
    kv = lambda w_, f: pl.BlockSpec((1, G, TQ, w_), lambda i, j, k: (i, 0, f(j, k), 0))
    sel_kt = lambda j, k: jnp.minimum(k, j)
    win_kt = lambda j, k: jnp.clip(k, jnp.maximum(j - nw, 0), j)
    return pl.pallas_call(
        _flash_prompt_body, out_shape=SDS((b, s, ATT), f32), grid=(b, nq, nq),
        in_specs=[pl.BlockSpec((1, H, TQ, DH), lambda i, j, k: (i, 0, j, 0)),
                  kv(DH, sel_kt), kv(2 * DH, sel_kt), kv(DH, win_kt), kv(2 * DH, win_kt),
                  _resident(bias_s.shape), _resident(bias_w.shape),
                  pl.BlockSpec((1, G, TQ, 128), lambda i, j, k: (i, 0, j, 0)),
                  pl.BlockSpec((1, TQ, 128), lambda i, j, k: (jnp.minimum(k, j), 0, 0)),
                  pl.BlockSpec((TQ, 128), lambda i, j, k: (i * nq + j, 0))],
        out_specs=pl.BlockSpec((1, TQ, ATT), lambda i, j, k: (i, j, 0)),
        scratch_shapes=[pltpu.VMEM((2 * H, TQ, 128), f32), pltpu.VMEM((2 * H, TQ, 128), f32),
                        pltpu.VMEM((R * TQ, TQ), f32), pltpu.VMEM((R * TQ, TQ), bf16), pltpu.VMEM((R * TQ, 128), f32)],
        compiler_params=_cp(("parallel", "parallel", "arbitrary"), 58), name="flash_prompt")(
            q2, ks, vs, kw, vw, bias_s, bias_w, selneg, et_tiles, gates)


def _cmp_sel_sample_body(q_ref, kc_ref, vc_ref, gt_ref, e_ref, o_ref, nm_ref, impt_ref, val_ref, *, past, n_q):
    n_cmp = kc_ref.shape[2]
    n_sel = n_cmp // 4
    sig = jax.nn.sigmoid(gt_ref[0])
    rows = R * n_q
    mp = lax.broadcasted_iota(jnp.int32, (rows, n_cmp), 1)
    i_q = lax.broadcasted_iota(jnp.int32, (rows, n_cmp), 0) % n_q
    valid = (mp >= 1) & (16 * mp + 15 <= past + i_q)
    imps = []
    for g in range(G):
        s = jnp.where(valid, _nt(q_ref[0, g], kc_ref[0, g]), NEG)
        m = jnp.max(s, axis=-1, keepdims=True)
        e = jnp.where(valid, jnp.exp(s - m), 0.0)
        p = e / jnp.maximum(jnp.sum(e, axis=-1, keepdims=True), 1e-30)
        o = _dot(p.astype(bf16), vc_ref[0, g]) * sig[g * rows:(g + 1) * rows, 0:1]
        for r in range(R):
            hh = g * R + r
            o_ref[0, :, hh * DH:(hh + 1) * DH] = o[r * n_q:(r + 1) * n_q, :]
        imps.append(p[0:n_q] + p[n_q:2 * n_q] + p[2 * n_q:3 * n_q])
    imp = jnp.concatenate(imps + [jnp.zeros((128 - G * n_q, n_cmp), f32)], axis=0)
    impt_ref[...] = imp.T
    v = _block_scores(impt_ref, n_sel)
    q_pos = past + lax.broadcasted_iota(jnp.int32, v.shape, 1) % n_q
    val, j = _force_scores(v, q_pos)
    val_ref[...] = val

    def count(k, rank):
        vk = val_ref[pl.ds(k, 1), :]
        return rank + jnp.where((vk > val) | ((vk == val) & (j > k)), 1, 0)

    rank = lax.fori_loop(0, n_sel, count, jnp.zeros(v.shape, jnp.int32))
    rank = rank + jnp.where(FORCE > val, 1, 0)
    neg = jnp.where(rank < TOPK, 0.0, NEG).T
    nm_ref[0] = _dot(neg[0:G * n_q, :].astype(bf16), e_ref[...])


def _cmp_sel_sample(qg, kc, vc, gsw, e_full, past):
    db, n_q = qg.shape[0], qg.shape[2] // R
    n_cmp = kc.shape[2]
    return pl.pallas_call(
        functools.partial(_cmp_sel_sample_body, past=past, n_q=n_q),
        out_shape=(SDS((db, n_q, ATT), f32), SDS((db, G * n_q, past), f32)), grid=(db,),
        in_specs=[pl.BlockSpec((1, G, R * n_q, DH), lambda i: (i, 0, 0, 0)),
                  pl.BlockSpec((1, G, n_cmp, DH), lambda i: (i, 0, 0, 0)),
                  pl.BlockSpec((1, G, n_cmp, DH), lambda i: (i, 0, 0, 0)),
                  pl.BlockSpec((1, H * n_q, 128), lambda i: (i, 0, 0)),
                  _resident((n_cmp // 4, past))],
        out_specs=(pl.BlockSpec((1, n_q, ATT), lambda i: (i, 0, 0)),
                   pl.BlockSpec((1, G * n_q, past), lambda i: (i, 0, 0))),
        scratch_shapes=[pltpu.VMEM((n_cmp, 128), f32), pltpu.VMEM((n_cmp // 4, 128), f32)],
        compiler_params=_cp(("parallel",), 32), name="cmp_sel_sample")(qg, kc, vc, gsw, e_full)


def _update_parts(q, parts, bias, m_ref, l_ref, acc_ref, idx):
    s = jnp.concatenate([_dot(q, kt.astype(bf16)) for kt, _ in parts], axis=1) + bias
    m_old = m_ref[idx]
    m_new = jnp.maximum(m_old, jnp.max(s, axis=-1, keepdims=True))
    alpha = jnp.exp(m_old - m_new)
    p = jnp.exp(s - m_new)
    l_ref[idx] = alpha * l_ref[idx] + jnp.sum(p, axis=-1, keepdims=True)
    p = p.astype(bf16)
    pv = acc_ref[idx] * alpha
    off = 0
    for _, vt in parts:
        pv = pv + _nt(p[:, off:off + vt.shape[1]], vt.astype(bf16))
        off += vt.shape[1]
    acc_ref[idx] = pv
    m_ref[idx] = m_new


def _flash_sample_body(*refs, n_q):
    refs = refs[1:]
    pages = refs[:CHUNK_PAGES]
    (q_ref, bias_ref, nm_ref, new_ref, bnew_ref, win_ref, wnew_ref, bwin_ref, gt_ref, o_ref, nwin_ref,
     m_ref, l_ref, acc_ref) = refs[CHUNK_PAGES:]
    step = pl.program_id(1)
    q = q_ref[0]

    @pl.when(step == 0)
    def _():
        m_ref[...] = jnp.full(m_ref.shape, NEG, f32)
        l_ref[...] = jnp.zeros(l_ref.shape, f32)
        acc_ref[...] = jnp.zeros(acc_ref.shape, f32)

    nm = nm_ref[0]
    nm = jnp.concatenate([nm[g * n_q:(g + 1) * n_q] for g in range(G) for _ in range(R)], axis=0)
    parts = [(pages[p][0, 0:G].reshape(G * DH, PAGE), pages[p][0, G:2 * G].reshape(G * DH, PAGE))
             for p in range(CHUNK_PAGES)]
    _update_parts(q, parts, bias_ref[...] + nm, m_ref, l_ref, acc_ref, 0)

    @pl.when(step == pl.num_programs(1) - 1)
    def _():
        new = new_ref[0]
        _update_parts(q, [(new[0:256], new[256:512])], bnew_ref[...], m_ref, l_ref, acc_ref, 0)
        old, wnew = win_ref[0, 0].reshape(8 * DH, WIN), wnew_ref[0]
        _update_parts(q, [(old[0:256], old[256:512]), (wnew[0:256], wnew[256:512])], bwin_ref[...],
                      m_ref, l_ref, acc_ref, 1)
        head, tail = _shifted_state(old, wnew, n_q)
        nwin_ref[0, :, :, 0:WIN - PAGE] = head.reshape(8, DH, WIN - PAGE)
        nwin_ref[0, :, :, WIN - PAGE:WIN] = tail.reshape(8, DH, PAGE)
        sig = jax.nn.sigmoid(gt_ref[0])
        res = (sig[:, 1:2] * acc_ref[0] / jnp.maximum(l_ref[0], 1e-30)
               + sig[:, 2:3] * acc_ref[1] / jnp.maximum(l_ref[1], 1e-30))
        for hh in range(H):
            g = hh // R
            o_ref[0, :, hh * DH:(hh + 1) * DH] = res[hh * n_q:(hh + 1) * n_q, g * DH:(g + 1) * DH]


def _flash_sample(pool, page_table, qbd, bias_slc, negmask, new_slc, bias_new, win_states, li, win_new, bias_win,
                  gsw, n_q):
    db, n_pages = page_table.shape
    steps = n_pages // CHUNK_PAGES
    rows = H * n_q
    keys = CHUNK_PAGES * PAGE
    cm = lambda f: (lambda i, j, pt: f(i, j))
    const = lambda shape: pl.BlockSpec(shape, lambda i, j, pt: (0,) * len(shape), pipeline_mode=pl.Buffered(1))
    per_b = lambda shape: pl.BlockSpec((1,) + shape, lambda i, j, pt: (i,) + (0,) * len(shape))
    page_specs = [pl.BlockSpec((1, 8, DH, PAGE), lambda i, j, pt, k=k: (pt[i, j * CHUNK_PAGES + k], 1, 0, 0))
                  for k in range(CHUNK_PAGES)]
    return pl.pallas_call(
        functools.partial(_flash_sample_body, n_q=n_q),
        out_shape=(SDS((db, n_q, ATT), f32), SDS((db, 8, DH, WIN), f32)),
        grid_spec=pltpu.PrefetchScalarGridSpec(
            num_scalar_prefetch=1, grid=(db, steps),
            in_specs=page_specs + [
                per_b((rows, 256)),
                pl.BlockSpec((rows, keys), cm(lambda i, j: (0, j))),
                pl.BlockSpec((1, G * n_q, keys), cm(lambda i, j: (i, 0, j))),
                per_b((8 * DH, PAGE)), const((rows, PAGE)),
                pl.BlockSpec((1, 1, 8, DH, WIN), lambda i, j, pt: (li, i, 0, 0, 0)),
                per_b((8 * DH, PAGE)), const((rows, WIN + PAGE)),
                per_b((rows, 128))],
            out_specs=(per_b((n_q, ATT)), per_b((8, DH, WIN))),
            scratch_shapes=[pltpu.VMEM((2, rows, 1), f32), pltpu.VMEM((2, rows, 1), f32),
                            pltpu.VMEM((2, rows, 256), f32)]),
        compiler_params=_cp(("arbitrary", "arbitrary"), 48), name="flash_sample")(
            page_table, *([pool] * CHUNK_PAGES), qbd, bias_slc, negmask, new_slc, bias_new,
            win_states, win_new, bias_win, gsw)


def _dil_tile(q, kc, vc, kp, vp, bias_ref, gi, prev_idx):
    lane = lax.broadcasted_iota(jnp.int32, (DT, 128), 1)
    low = lane < DH
    if kp is None:
        kk, vv = kc.astype(bf16), vc.astype(bf16)
    else:
        kk = jnp.concatenate([kp, kc], axis=0).astype(bf16)
        vv = jnp.concatenate([vp, vc], axis=0).astype(bf16)
    outs, lses = [], []
    for e in range(2):
        qe = jnp.where(low if e == 0 else jnp.logical_not(low), q, 0.0).astype(bf16)
        bias = bias_ref[gi, 0, e, 0]
        if kp is not None:
            bias = jnp.concatenate([bias_ref[gi, 0, e, prev_idx], bias], axis=1)
        s = _nt(qe, kk) + bias
        m = jnp.max(s, axis=-1, keepdims=True)
        ex = jnp.exp(s - m)
        l = jnp.maximum(jnp.sum(ex, axis=-1, keepdims=True), 1e-30)
        outs.append(_dot(ex.astype(bf16), vv) / l)
        lses.append(m + jnp.log(l))
    return jnp.where(low, outs[0], outs[1]), jnp.where(low, lses[0], lses[1])


def _dil_combine(os_, ls_):
    mx = jnp.maximum(jnp.maximum(ls_[0], ls_[1]), ls_[2])
    ws = [jnp.exp(l - mx) for l in ls_]
    den = ws[0] + ws[1] + ws[2]
    return [o * w / den for o, w in zip(os_, ws)]


def _dil_prompt_body(q0, q1, q2, k0, k1, k2, v0, v1, v2, bias_ref, y_ref, osc, lsc, *, s):
    qs, ks, vs = (q0, q1, q2), (k0, k1, k2), (v0, v1, v2)
    for gi, (_, dil) in enumerate(DIL):
        n_t = s // dil // DT
        q_r, k_r, v_r = qs[gi], ks[gi], vs[gi]

        def tile(start, prev_start, prev_idx, gi=gi, dil=dil, q_r=q_r, k_r=k_r, v_r=v_r, n_t=n_t):
            rows = lambda st: pl.ds(st, DT, stride=dil) if dil > 1 else pl.ds(pl.multiple_of(st, DT), DT)
            kp = k_r[0, rows(prev_start), :] if n_t > 1 else None
            vp = v_r[0, rows(prev_start), :] if n_t > 1 else None
            o, lse = _dil_tile(q_r[0, rows(start), :], k_r[0, rows(start), :], v_r[0, rows(start), :],
                               kp, vp, bias_ref, gi, prev_idx)
            osc[gi, rows(start), :] = o
            lsc[gi, rows(start), :] = lse

        def body(it, carry, dil=dil, n_t=n_t, tile=tile):
            rho, t = it // n_t, it % n_t
            start = rho + t * (DT * dil)
            prev = rho + jnp.maximum(t - 1, 0) * (DT * dil)
            tile(start, prev, jnp.where(t == 0, 2, 1))
            return carry

        lax.fori_loop(0, dil * n_t, body, 0, unroll=DIL_UNROLL)
    outs = _dil_combine([osc[0], osc[1], osc[2]], [lsc[0], lsc[1], lsc[2]])
    for pair in range(2):
        @pl.when(pl.program_id(1) == pair)
        def _(pair=pair):
            for gi, o in enumerate(outs):
                c = 2 * gi + pair
                y_ref[0, :, c * 128:(c + 1) * 128] = o


def _dil_prompt(zd, bias_tiles, b, s):
    col = lambda base: [pl.BlockSpec((1, s, 128), lambda i, j, c=base + 2 * gi: (i, 0, c + j)) for gi in range(3)]
    return pl.pallas_call(
        functools.partial(_dil_prompt_body, s=s),
        out_shape=SDS((b, s, ATT), f32), grid=(b, 2),
        in_specs=col(0) + col(6) + col(12) + [pl.BlockSpec((3, 1, 2, 3, DT, DT), lambda i, j: (0, j, 0, 0, 0, 0))],
        out_specs=pl.BlockSpec((1, s, ATT), lambda i, j: (i, 0, 0)),
        scratch_shapes=[pltpu.VMEM((3, s, 128), f32), pltpu.VMEM((3, s, 128), f32)],
        compiler_params=_cp(("parallel", "arbitrary"), 48), name="dil_prompt")(*([zd] * 9), bias_tiles)


def _shifted_state(old, new, n_q):
    w = old.shape[1]
    rolled = pltpu.roll(old, w - n_q, 1)
    lane = lax.broadcasted_iota(jnp.int32, new.shape, 1)
    tail = jnp.where(lane >= PAGE - n_q, pltpu.roll(new, PAGE - n_q, 1), rolled[:, w - PAGE:])
    return rolled[:, :w - PAGE], tail


def _dil_sample_body(q_ref, s0, s1, s2, new_ref, b0, b1, b2, y_ref, n0, n1, n2, *, n_q):
    lane = lax.broadcasted_iota(jnp.int32, (n_q, 256), 1)
    os_, ls_ = [], []
    for gi, (st, bias, nst) in enumerate(((s0, b0, n0), (s1, b1, n1), (s2, b2, n2))):
        w = st.shape[4]
        old = st[0, 0].reshape(8 * DH, w)
        new = new_ref[0, gi]
        head, tail = _shifted_state(old, new, n_q)
        if w > PAGE:
            nst[0, :, :, 0:w - PAGE] = head.reshape(8, DH, w - PAGE)
        nst[0, :, :, w - PAGE:w] = tail.reshape(8, DH, PAGE)
        q = q_ref[0, gi]
        s = jnp.concatenate([_dot(q, old[0:256].astype(bf16)), _dot(q, new[0:256].astype(bf16))], axis=1) + bias[...]
        m = jnp.max(s, axis=-1, keepdims=True)
        e = jnp.exp(s - m)
        l = jnp.maximum(jnp.sum(e, axis=-1, keepdims=True), 1e-30)
        eb = e.astype(bf16)
        res = (_nt(eb[:, 0:w], old[256:512].astype(bf16)) + _nt(eb[:, w:], new[256:512].astype(bf16))) / l
        lse = m + jnp.log(l)
        o = jnp.zeros((n_q, 256), f32)
        lm = jnp.zeros((n_q, 256), f32)
        for hg in range(4):
            inb = (lane >= hg * DH) & (lane < (hg + 1) * DH)
            o = jnp.where(inb, res[hg * n_q:(hg + 1) * n_q, :], o)
            lm = jnp.where(inb, lse[hg * n_q:(hg + 1) * n_q, :], lm)
        os_.append(o)
        ls_.append(lm)
    for gi, o in enumerate(_dil_combine(os_, ls_)):
        y_ref[0, :, gi * 256:(gi + 1) * 256] = o


def _dil_sample(qbd, states, li, new_t, biases, n_q):
    db = new_t.shape[0]
    per_b = lambda shape: pl.BlockSpec((1,) + shape, lambda i: (i,) + (0,) * len(shape))
    wins = [st.shape[4] for st in states]
    return pl.pallas_call(
        functools.partial(_dil_sample_body, n_q=n_q),
        out_shape=(SDS((db, n_q, ATT), f32),) + tuple(SDS((db, 8, DH, w), f32) for w in wins), grid=(db,),
        in_specs=[per_b((3, 4 * n_q, 256))]
                 + [pl.BlockSpec((1, 1, 8, DH, w), lambda i: (li, i, 0, 0, 0)) for w in wins]
                 + [per_b((3, 8 * DH, PAGE))] + [_resident(bb.shape) for bb in biases],
        out_specs=(per_b((n_q, ATT)),) + tuple(per_b((8, DH, w)) for w in wins),
        compiler_params=_cp(("parallel",), 48), name="dil_sample")(qbd, *states, new_t, *biases)


def _rel_bucket(dist):
    dist = jnp.maximum(dist, 0)
    d32 = jnp.maximum(dist, 1).astype(f32)
    large = 16 + (jnp.log(d32 / 16) / math.log(2048 / 16) * 16).astype(jnp.int32)
    return jnp.where(dist < 16, dist, jnp.minimum(large, 31))


def _bias_rev(tbl, n, keep, scale=1):
    dist = (n - 1 - jnp.arange(n)) * scale
    return jnp.where(keep(dist), tbl[_rel_bucket(dist)].T, NEG)


def _toeplitz_tiles(bdr, n_delta, t):
    hn = bdr.shape[0]
    bdp = jnp.concatenate([bdr, jnp.full((hn, t - 1), NEG, f32)], axis=1)
    lp = bdp.shape[1]
    gq = jnp.stack([bdp[:, lp + 1 - d * t - 2 * t:lp - d * t] for d in range(n_delta)])
    gq = jnp.pad(gq, ((0, 0), (0, 0), (0, 1)))
    x = jnp.broadcast_to(gq[:, :, None, :], (n_delta, hn, t, 2 * t)).reshape(n_delta, hn, t * 2 * t)
    y = x[..., :t * (2 * t - 1)].reshape(n_delta, hn, t, 2 * t - 1)
    return y[..., t - 1:]


def _rows_by_query(bdr, n_q, n_keys, first_dist):
    last = bdr.shape[1] - 1
    return jnp.stack([bdr[:, last - first_dist - i:last - first_dist - i + n_keys] for i in range(n_q)], axis=1)


def _new_token_bias(tbl, n_q, keep):
    tri = jnp.arange(n_q)[:, None] - jnp.arange(PAGE)[None, :]
    ok = (tri >= 0) & keep(tri)
    return jnp.where(ok[None], jnp.transpose(tbl[_rel_bucket(tri)], (2, 0, 1)), NEG)


def _nsa_tables(rel_bias, s, n_q, past):
    everywhere = lambda d: d >= 0
    in_window = lambda d: d < WIN
    n = past + n_q
    key_blk = jnp.arange(s // TQ)[:, None, None] * (TQ // SEL_BLOCK) + jnp.arange(TQ)[None, :, None] // SEL_BLOCK
    bias_new = _new_token_bias(rel_bias, n_q, everywhere).reshape(H * n_q, PAGE)
    bias_win = _rows_by_query(_bias_rev(rel_bias, n, in_window), n_q, WIN, WIN).reshape(H * n_q, WIN)
    return dict(
        perm=_chunk_perm(),
        bias_s=_toeplitz_tiles(_bias_rev(rel_bias, s, everywhere), s // TQ, TQ) * LOG2E,
        bias_w=_toeplitz_tiles(_bias_rev(rel_bias, WIN + TQ, in_window), WIN // TQ + 1, TQ) * LOG2E,
        et_tiles=(key_blk == jnp.arange(128)[None, None, :]).astype(bf16),
        e_full=(jnp.arange(past // SEL_BLOCK)[:, None] == jnp.arange(past)[None, :] // SEL_BLOCK).astype(bf16),
        bias_slc=_rows_by_query(_bias_rev(rel_bias, n, everywhere), n_q, past, past).reshape(H * n_q, past),
        bias_new=bias_new,
        bias_win=jnp.concatenate([bias_win, bias_new], axis=1))


def _nsa_layer(yp, ys, g_pre, g_post, w_in, w_out, cmp_pos, cmp_w1, cmp_w2, tb, pool, page_table, win_states,
               li, b, s, db, n_q, past):
    w_pad = jnp.pad(w_in, ((0, 0), (0, NSA_PAD - w_in.shape[1]))).astype(bf16)
    w_out = w_out.astype(bf16)
    w1cat = jnp.concatenate([cmp_w1[:, :1024], cmp_w1[:, 1024:]], axis=-1).astype(bf16)
    pos2 = cmp_pos.reshape(2, 2, 1024)
    w2 = cmp_w2.astype(bf16)

    q_hm, q2_hm, kvt, kwin, ks, vs, kw, vw, gates = _nsa_proj_prompt(yp, g_pre, w_pad, b, s)
    kc, vc = _compress_prompt(kvt, tb["perm"], w1cat, pos2, w2, b, s)
    o_cmp, selneg = _cmp_sel_prompt(q_hm, kc, vc, gates, b, s)
    o_sw = _flash_prompt(q2_hm, ks, vs, kw, vw, tb["bias_s"], tb["bias_w"], selneg, tb["et_tiles"], gates, b, s)
    yp = _out_proj(yp, g_post, w_out, o_cmp.reshape(b * s, ATT), o_sw.reshape(b * s, ATT))
    kv_p = jnp.transpose(kvt.reshape(b, 4, G, DH, s), (0, 4, 1, 2, 3))
    win_p = kwin.reshape(b, s, 2, G, DH)[:, s - min(WIN, s):]

    z = _proj(ys, g_pre, w_pad)
    q = (z[:, :ATT] * (DH ** -0.5)).reshape(db, n_q, H, DH).transpose(0, 2, 1, 3)
    qg = q.reshape(db, G, R * n_q, DH).astype(bf16)
    eye = jnp.eye(G, dtype=f32)
    qbd = (q.reshape(db, G, R * n_q, 1, DH) * eye[None, :, None, :, None]).reshape(db, H * n_q, G * DH).astype(bf16)
    kv_new = z[:, ATT:ATT + 1024].reshape(db, n_q, 1024)
    win_new = z[:, ATT + 1024:ATT + 1536].reshape(db, n_q, 512)
    gt = z[:, ATT + 1536:ATT + 1536 + 3 * H].reshape(db, n_q, 3, H).transpose(0, 3, 1, 2)
    gsw = jnp.pad(gt.reshape(db, H * n_q, 3), ((0, 0), (0, 0), (0, 125)))
    kc_s, vc_s = _compress_sample(pool, page_table, tb["perm"], w1cat, pos2, w2)
    o_cmp_s, negmask = _cmp_sel_sample(qg, kc_s, vc_s, gsw, tb["e_full"], past)
    page_t = lambda x: jnp.pad(jnp.transpose(x, (0, 2, 1)), ((0, 0), (0, 0), (0, PAGE - n_q)))
    win_view = jnp.transpose(win_states, (0, 1, 3, 4, 5, 2)).reshape(win_states.shape[0], db, 8, DH, WIN)
    o_sw_s, win_t = _flash_sample(pool, page_table, qbd, tb["bias_slc"], negmask, page_t(kv_new[:, :, 512:]),
                                  tb["bias_new"], win_view, li, page_t(win_new), tb["bias_win"], gsw, n_q)
    ys = _out_proj(ys, g_post, w_out, o_cmp_s.reshape(db * n_q, ATT), o_sw_s.reshape(db * n_q, ATT))
    kv_s = kv_new.reshape(db, n_q, 4, G, DH)
    win_s = jnp.transpose(win_t.reshape(db, 2, G, DH, WIN), (0, 4, 1, 2, 3))
    return yp, ys, kv_p, win_p, kv_s, win_s


def _dil_tables(rel_bias, n_q):
    tiles, biases = [], []
    for gi, (win, dil) in enumerate(DIL):
        tbl = rel_bias[:, gi * 4:(gi + 1) * 4]
        on_grid = lambda d, win=win, dil=dil: (d % dil == 0) & (d <= win)
        tl = _toeplitz_tiles(_bias_rev(tbl, 2 * DT, on_grid, scale=dil), 2, DT)
        tiles.append(jnp.concatenate([tl, jnp.full_like(tl[:1], NEG)]))
        b_all = jnp.concatenate([_rows_by_query(_bias_rev(tbl, win + n_q, on_grid), n_q, win, win),
                                 _new_token_bias(tbl, n_q, on_grid)], axis=2)
        biases.append(b_all.reshape(4 * n_q, win + PAGE))
    return dict(tiles=jnp.stack(tiles).transpose(0, 2, 1, 3, 4).reshape(3, 2, 2, 3, DT, DT), sample=biases)


def _dil_layer(yp, ys, g_pre, g_post, w_in, w_out, tb, states, li, b, s, db, n_q):
    w_in = w_in.astype(bf16)
    w_out = w_out.astype(bf16)

    zd, kvt = _dil_proj(yp, g_pre, w_in, b, s)
    y = _dil_prompt(zd.reshape(b, s, 3 * ATT), tb["tiles"], b, s)
    yp = _out_proj(yp, g_post, w_out, y.reshape(b * s, ATT))
    kvt = kvt.reshape(b, 2, H, DH, s)
    st_p = []
    for gi, (win, _) in enumerate(DIL):
        keep = min(win, s)
        st_p.append(jnp.transpose(kvt[:, :, gi * 4:(gi + 1) * 4, :, s - keep:], (0, 4, 1, 2, 3)))

    z = _proj(ys, g_pre, w_in).reshape(db, n_q, 3, 3, 4, DH)
    q = (z[:, :, 0] * (DH ** -0.5)).transpose(0, 2, 3, 1, 4)
    eye = jnp.eye(4, dtype=f32)
    qbd = (q[:, :, :, :, None, :] * eye[None, None, :, None, :, None]).reshape(db, 3, 4 * n_q, 256).astype(bf16)
    new_t = jnp.transpose(z[:, :, 1:3], (0, 3, 2, 4, 5, 1)).reshape(db, 3, 8 * DH, n_q)
    new_t = jnp.pad(new_t, ((0, 0), (0, 0), (0, 0), (0, PAGE - n_q)))
    views = [jnp.transpose(st, (0, 1, 3, 4, 5, 2)).reshape(st.shape[0], db, 8, DH, st.shape[2]) for st in states]
    y_s, n0, n1, n2 = _dil_sample(qbd, views, li, new_t, tb["sample"], n_q)
    ys = _out_proj(ys, g_post, w_out, y_s.reshape(db * n_q, ATT))
    st_s = [jnp.transpose(n.reshape(db, 2, 4, DH, n.shape[3]), (0, 4, 1, 2, 3)) for n in (n0, n1, n2)]
    return yp, ys, st_p, st_s


def kernel(x_prompt, x_sample, cache_nsa_kv, page_table, state_nsa_win, state_dil_w128, state_dil_w512,
           state_dil_w2048, norm_g, ffn_w_in, ffn_w_out, nsa_w_in, nsa_w_out, nsa_cmp_pos, nsa_cmp_w1,
           nsa_cmp_w2, dil_w_in, dil_w_out, rel_bias):
    b, s, _ = x_prompt.shape
    db, n_q, _ = x_sample.shape
    depth = norm_g.shape[0]
    past = page_table.shape[1] * PAGE
    yp = x_prompt.reshape(b * s, D)
    ys = x_sample.reshape(db * n_q, D)
    nsa_kv_p, nsa_win_p, nsa_kv_s, nsa_win_s = [], [], [], []
    dil_p, dil_s = [[], [], []], [[], [], []]
    nsa_tb = _nsa_tables(rel_bias, s, n_q, past)
    dil_tb = _dil_tables(rel_bias, n_q)
    for i in range(depth):
        li = i // 2
        g = norm_g[i]
        w_i, w_o = ffn_w_in[i, 0].astype(bf16), ffn_w_out[i, 0].astype(bf16)
        yp = _half_ffn(yp, g[0:2], w_i, w_o)
        ys = _half_ffn(ys, g[0:2], w_i, w_o)
        if i % 2 == 0:
            n_pool = cache_nsa_kv.shape[1]
            pool = jnp.transpose(cache_nsa_kv, (0, 1, 3, 4, 5, 2)).reshape(
                cache_nsa_kv.shape[0] * n_pool, 4 * G, DH, PAGE)
            yp, ys, kvp, winp, kvs, wins = _nsa_layer(
                yp, ys, g[2:3], g[3:4], nsa_w_in[li], nsa_w_out[li], nsa_cmp_pos[li], nsa_cmp_w1[li],
                nsa_cmp_w2[li], nsa_tb, pool, page_table + li * n_pool, state_nsa_win, li, b, s, db, n_q, past)
            nsa_kv_p.append(kvp)
            nsa_win_p.append(winp)
            nsa_kv_s.append(kvs)
            nsa_win_s.append(wins)
        else:
            states = (state_dil_w128, state_dil_w512, state_dil_w2048)
            yp, ys, stp, sts = _dil_layer(yp, ys, g[2:3], g[3:4], dil_w_in[li], dil_w_out[li], dil_tb, states,
                                          li, b, s, db, n_q)
            for gi in range(3):
                dil_p[gi].append(stp[gi])
                dil_s[gi].append(sts[gi])
        w_i, w_o = ffn_w_in[i, 1].astype(bf16), ffn_w_out[i, 1].astype(bf16)
        yp = _half_ffn(yp, g[4:6], w_i, w_o)
        ys = _half_ffn(ys, g[4:6], w_i, w_o)
    return (yp.reshape(b, s, D), ys.reshape(db, n_q, D),
            jnp.stack(nsa_kv_p), jnp.stack(nsa_win_p),
            jnp.stack(dil_p[0]), jnp.stack(dil_p[1]), jnp.stack(dil_p[2]),
            jnp.stack(nsa_kv_s), jnp.stack(nsa_win_s),
            jnp.stack(dil_s[0]), jnp.stack(dil_s[1]), jnp.stack(dil_s[2]))
```

```python
import functools
import math

import jax
import jax.numpy as jnp
from jax import lax
from jax.experimental import pallas as pl
from jax.experimental.pallas import tpu as pltpu

f32, bf16 = jnp.float32, jnp.bfloat16
SDS = jax.ShapeDtypeStruct

D = 1024
FF = 2816
FF_CHUNK = 256
H, DH = 12, 64
G, R = 4, 3
ATT = H * DH
EPS = 1e-6
NEG = -1e30
LOG2E = math.log2(math.e)
FORCE = 1e4
SEL_BLOCK, TOPK = 64, 16
WIN = 512
NSA_PAD = 2432
TQ = 256
RC = 64
PAGE = 128
CHUNK_PAGES = 16
CMP_GROUP = 2
DIL = ((128, 1), (512, 4), (2048, 16))
DT = 128
DIL_UNROLL = 4
MIB = 1024 * 1024


def _cp(sem, vmem_mib):
    return pltpu.CompilerParams(dimension_semantics=sem, vmem_limit_bytes=vmem_mib * MIB)


def _rms(x, g):
    return x * lax.rsqrt(jnp.mean(x * x, axis=-1, keepdims=True) + EPS) * g


def _nt(a, b):
    return lax.dot_general(a, b, (((1,), (1,)), ((), ())), preferred_element_type=f32)


def _dot(a, b):
    return jnp.dot(a, b, preferred_element_type=f32)


def _resident(shape):
    nd = len(shape)
    return pl.BlockSpec(shape, lambda *_: (0,) * nd, pipeline_mode=pl.Buffered(1))


def _ffn_body(x_ref, g_ref, win_ref, wout_ref, o_ref):
    x = x_ref[...]
    h = _rms(x, g_ref[0:1, :]).astype(bf16)
    acc = jnp.zeros(x.shape, f32)
    for c in range(FF // FF_CHUNK):
        lo, hi = c * FF_CHUNK, (c + 1) * FF_CHUNK
        gate = _dot(h, win_ref[:, lo:hi])
        up = _dot(h, win_ref[:, FF + lo:FF + hi])
        act = (gate * jax.nn.sigmoid(gate) * up).astype(bf16)
        acc = acc + _dot(act, wout_ref[lo:hi, :])
    o_ref[...] = x + 0.5 * _rms(acc, g_ref[1:2, :])


def _half_ffn(x, g2, w_in, w_out):
    m = x.shape[0]
    tm = min(m, 512)
    return pl.pallas_call(
        _ffn_body, out_shape=SDS((m, D), f32), grid=(m // tm,),
        in_specs=[pl.BlockSpec((tm, D), lambda i: (i, 0)), _resident((2, D)),
                  _resident((D, 2 * FF)), _resident((FF, D))],
        out_specs=pl.BlockSpec((tm, D), lambda i: (i, 0)),
        compiler_params=_cp(("parallel",), 48), name="half_ffn")(x, g2, w_in, w_out)


def _proj_body(x_ref, g_ref, w_ref, o_ref):
    h = _rms(x_ref[...], g_ref[...]).astype(bf16)
    o_ref[...] = _dot(h, w_ref[...])


def _proj(x, g, w):
    m, n = x.shape[0], w.shape[1]
    return pl.pallas_call(
        _proj_body, out_shape=SDS((m, n), f32), grid=(1,),
        in_specs=[_resident((m, D)), _resident((1, D)), _resident((D, n))],
        out_specs=pl.BlockSpec((m, n), lambda i: (0, 0)),
        compiler_params=_cp(("arbitrary",), 32), name="proj_sample")(x, g, w)


def _nsa_proj_body(x_ref, g_ref, w_ref, q_ref, q2_ref, kv_ref, kwin_ref, ks_ref, vs_ref, kw_ref, vw_ref, gt_ref):
    h = _rms(x_ref[...], g_ref[...]).astype(bf16)
    zq = _dot(h, w_ref[:, 0:ATT]) * (DH ** -0.5)
    zq2 = zq * LOG2E
    for hh in range(H):
        q_ref[0, hh] = zq[:, hh * DH:(hh + 1) * DH].astype(bf16)
        q2_ref[0, hh] = zq2[:, hh * DH:(hh + 1) * DH].astype(bf16)
    zc = _dot(h, w_ref[:, ATT:ATT + 512])
    kv_ref[0, 0:2 * G] = zc.T.reshape(2 * G, DH, zc.shape[0])
    zs = _dot(h, w_ref[:, ATT + 512:ATT + 1024])
    kv_ref[0, 2 * G:4 * G] = zs.T.reshape(2 * G, DH, zs.shape[0])
    zw = _dot(h, w_ref[:, ATT + 1024:ATT + 1536])
    kwin_ref[...] = zw
    ones = jnp.ones((zs.shape[0], DH), bf16)
    for g in range(G):
        ks_ref[0, g] = zs[:, g * DH:(g + 1) * DH].astype(bf16)
        vs_ref[0, g] = jnp.concatenate([zs[:, 256 + g * DH:256 + (g + 1) * DH].astype(bf16), ones], axis=1)
        kw_ref[0, g] = zw[:, g * DH:(g + 1) * DH].astype(bf16)
        vw_ref[0, g] = jnp.concatenate([zw[:, 256 + g * DH:256 + (g + 1) * DH].astype(bf16), ones], axis=1)
    gt_ref[...] = _dot(h, w_ref[:, ATT + 1536:NSA_PAD])


def _nsa_proj_prompt(x, g, w, b, s):
    tm = 512
    nt = s // tm
    hm = lambda n, w_: pl.BlockSpec((1, n, tm, w_), lambda i: (i // nt, 0, i % nt, 0))
    rows = lambda n: pl.BlockSpec((tm, n), lambda i: (i, 0))
    return pl.pallas_call(
        _nsa_proj_body,
        out_shape=(SDS((b, H, s, DH), bf16), SDS((b, H, s, DH), bf16), SDS((b, 4 * G, DH, s), f32),
                   SDS((b * s, 512), f32), SDS((b, G, s, DH), bf16), SDS((b, G, s, 2 * DH), bf16),
                   SDS((b, G, s, DH), bf16), SDS((b, G, s, 2 * DH), bf16), SDS((b * s, 128), f32)),
        grid=(b * s // tm,),
        in_specs=[rows(D), _resident((1, D)), _resident((D, NSA_PAD))],
        out_specs=(hm(H, DH), hm(H, DH), pl.BlockSpec((1, 4 * G, DH, tm), lambda i: (i // nt, 0, 0, i % nt)),
                   rows(512), hm(G, DH), hm(G, 2 * DH), hm(G, DH), hm(G, 2 * DH), rows(128)),
        compiler_params=_cp(("parallel",), 48), name="nsa_proj_prompt")(x, g, w)


def _dil_proj_body(x_ref, g_ref, w_ref, o_ref, kvt_ref):
    h = _rms(x_ref[...], g_ref[...]).astype(bf16)
    o_ref[:, 0:ATT] = _dot(h, w_ref[:, 0:ATT]) * (DH ** -0.5)
    for part in (1, 2):
        z = _dot(h, w_ref[:, part * ATT:(part + 1) * ATT])
        o_ref[:, part * ATT:(part + 1) * ATT] = z
        kvt_ref[0, (part - 1) * H:part * H] = z.T.reshape(H, DH, z.shape[0])


def _dil_proj(x, g, w, b, s):
    tm = 512
    nt = s // tm
    return pl.pallas_call(
        _dil_proj_body, out_shape=(SDS((b * s, 3 * ATT), f32), SDS((b, 2 * H, DH, s), f32)), grid=(b * s // tm,),
        in_specs=[pl.BlockSpec((tm, D), lambda i: (i, 0)), _resident((1, D)), _resident((D, 3 * ATT))],
        out_specs=(pl.BlockSpec((tm, 3 * ATT), lambda i: (i, 0)),
                   pl.BlockSpec((1, 2 * H, DH, tm), lambda i: (i // nt, 0, 0, i % nt))),
        compiler_params=_cp(("parallel",), 48), name="dil_proj")(x, g, w)


def _out_proj_body(x_ref, g_ref, w_ref, *refs):
    o_ref = refs[-1]
    o = refs[0][...]
    for r in refs[1:-1]:
        o = o + r[...]
    y = _dot(o.astype(bf16), w_ref[...])
    o_ref[...] = x_ref[...] + _rms(y, g_ref[...])


def _out_proj(x, g, w, *branches):
    m = x.shape[0]
    tm = min(m, 512)
    row = lambda n: pl.BlockSpec((tm, n), lambda i: (i, 0))
    return pl.pallas_call(
        _out_proj_body, out_shape=SDS((m, D), f32), grid=(m // tm,),
        in_specs=[row(D), _resident((1, D)), _resident((ATT, D))] + [row(ATT)] * len(branches),
        out_specs=row(D), compiler_params=_cp(("parallel",), 32), name="out_proj")(x, g, w, *branches)


def _compress_body(*refs, n_src, n_pages, paged, group=1):
    if paged:
        refs = refs[1:]
    srcs = refs[:n_src]
    perm_ref, w1_ref, pos_ref, w2_ref, kc_ref, vc_ref, lhs_ref, aprev_ref = refs[n_src:]
    nch = n_pages * PAGE // 16
    gn = group * nch
    step = pl.program_id(1) if paged else 0
    sub = step % group if paged else 0

    for p in range(n_pages):
        mats = srcs[p][0] if paged else srcs[0][0, :, :, p * PAGE:(p + 1) * PAGE]
        both = jnp.concatenate([jnp.concatenate([mats[2 * c], mats[2 * c + 1]], axis=1) for c in range(4)], axis=0)
        out = _dot(both.astype(bf16), perm_ref[...])
        for c in range(4):
            kind, pair = c // 2, c % 2
            o = out[c * DH:(c + 1) * DH]
            zt = jnp.concatenate([o[:, 0:128], o[:, 128:256]], axis=0).T
            for gl in range(2):
                start = (2 * pair + gl) * gn + sub * nch + p * 8
                dst = pl.ds(pl.multiple_of(start, 8), 8) if paged else pl.ds(start, 8)
                for r2 in range(8):
                    lhs_ref[kind, dst, r2 * 128:(r2 + 1) * 128] = zt[(gl * 8 + r2) * 8:(gl * 8 + r2 + 1) * 8, :]

    def mlp():
        first = (step < group) if paged else True
        if paged:
            @pl.when(step == group - 1)
            def _():
                aprev_ref[...] = jnp.zeros(aprev_ref.shape, f32)

        row = lax.broadcasted_iota(jnp.int32, (gn, 256), 0)
        row_o = lax.broadcasted_iota(jnp.int32, (gn, DH), 0)
        for kind, out_ref in ((0, kc_ref), (1, vc_ref)):
            pp = _dot(lhs_ref[kind].astype(bf16), w1_ref[kind])
            pos = _dot(pos_ref[kind].astype(bf16), w1_ref[kind])
            posterm = pos[0:1, 0:256] + pos[1:2, 256:512]
            for g in range(G):
                a = pp[g * gn:(g + 1) * gn, 0:256]
                bm = pp[g * gn:(g + 1) * gn, 256:512]
                shifted = pltpu.roll(a, 1, 0)
                if paged:
                    carry = aprev_ref[kind, (g + 1) * gn - 1:(g + 1) * gn, :]
                    shifted = jnp.where(row == 0, carry, shifted)
                    aprev_ref[kind, g * gn:(g + 1) * gn, :] = a
                pre = shifted + bm + posterm
                hid = (pre * (0.5 + 0.5 * jnp.tanh(0.5 * pre))).astype(bf16)
                out = _dot(hid, w2_ref[kind])
                out = jnp.where((row_o == 0) & first, 0.0, out)
                out_ref[0, g] = out.astype(bf16)

    if paged:
        pl.when(sub == group - 1)(mlp)
    else:
        mlp()


def _chunk_perm():
    t = jnp.arange(PAGE)
    col = (t % 2) * 128 + ((t % 16) // 2) * 8 + t // 16
    src = jnp.concatenate([col, col + DH])
    return (src[:, None] == jnp.arange(256)[None, :]).astype(bf16)


def _compress_prompt(kvt, perm, w1cat, pos2, w2, b, s):
    nch = s // 16
    out = SDS((b, G, nch, DH), bf16)
    ospec = pl.BlockSpec((1, G, nch, DH), lambda i: (i, 0, 0, 0))
    return pl.pallas_call(
        functools.partial(_compress_body, n_src=1, n_pages=s // PAGE, paged=False),
        out_shape=(out, out), grid=(b,),
        in_specs=[pl.BlockSpec((1, 8, DH, s), lambda i: (i, 0, 0, 0)), _resident((256, 256)),
                  _resident((2, 1024, 512)), _resident((2, 2, 1024)), _resident((2, 256, DH))],
        out_specs=(ospec, ospec),
        scratch_shapes=[pltpu.VMEM((2, G * nch, 1024), f32), pltpu.VMEM((2, G * nch, 256), f32)],
        compiler_params=_cp(("parallel",), 48), name="compress_prompt")(kvt, perm, w1cat, pos2, w2)


def _compress_sample(pool_t, page_table, perm, w1cat, pos2, w2):
    db, n_pages = page_table.shape
    steps = n_pages // CHUNK_PAGES
    gn = CMP_GROUP * CHUNK_PAGES * PAGE // 16
    out = SDS((db, G, n_pages * PAGE // 16, DH), bf16)
    ospec = pl.BlockSpec((1, G, gn, DH), lambda i, j, pt: (i, 0, j // CMP_GROUP, 0))
    const = lambda shape: pl.BlockSpec(shape, lambda i, j, pt: (0,) * len(shape), pipeline_mode=pl.Buffered(1))
    page_specs = [pl.BlockSpec((1, 8, DH, PAGE), lambda i, j, pt, k=k: (pt[i, j * CHUNK_PAGES + k], 0, 0, 0))
                  for k in range(CHUNK_PAGES)]
    return pl.pallas_call(
        functools.partial(_compress_body, n_src=CHUNK_PAGES, n_pages=CHUNK_PAGES, paged=True, group=CMP_GROUP),
        out_shape=(out, out),
        grid_spec=pltpu.PrefetchScalarGridSpec(
            num_scalar_prefetch=1, grid=(db, steps),
            in_specs=page_specs + [const((256, 256)), const((2, 1024, 512)), const((2, 2, 1024)),
                                   const((2, 256, DH))],
            out_specs=(ospec, ospec),
            scratch_shapes=[pltpu.VMEM((2, G * gn, 1024), f32), pltpu.VMEM((2, G * gn, 256), f32)]),
        compiler_params=_cp(("arbitrary", "arbitrary"), 40), name="compress_sample")(
            page_table, *([pool_t] * CHUNK_PAGES), perm, w1cat, pos2, w2)


def _block_scores(impt_ref, n_blocks):
    return (impt_ref[pl.ds(1, n_blocks, stride=4), :] + impt_ref[pl.ds(2, n_blocks, stride=4), :]
            + impt_ref[pl.ds(3, n_blocks, stride=4), :])


def _force_scores(v, q_pos):
    j = lax.broadcasted_iota(jnp.int32, v.shape, 0)
    cur = q_pos >> 6
    forced = (j == 0) | (j == cur) | (j == cur - 1)
    return jnp.where(forced, FORCE, jnp.where(j > cur, -FORCE, v)), j


def _cmp_sel_prompt_body(q_ref, kc_ref, vc_ref, gt_ref, o_ref, sel_ref, impt_ref, *, n_sel):
    qt = pl.program_id(1)
    sig = jax.nn.sigmoid(gt_ref[...])
    t = qt * TQ + lax.broadcasted_iota(jnp.int32, (TQ, 128), 0)
    mp = lax.broadcasted_iota(jnp.int32, (TQ, 128), 1)
    valid = (mp >= 1) & (16 * mp + 15 <= t)
    for g in range(G):
        kc, vc = kc_ref[0, g], vc_ref[0, g]
        imp = jnp.zeros((TQ, 128), f32)
        for r in range(R):
            hh = g * R + r
            s = jnp.where(valid, _nt(q_ref[0, hh], kc), NEG)
            m = jnp.max(s, axis=-1, keepdims=True)
            e = jnp.where(valid, jnp.exp(s - m), 0.0)
            p = e / jnp.maximum(jnp.sum(e, axis=-1, keepdims=True), 1e-30)
            imp = imp + p
            o_ref[0, :, hh * DH:(hh + 1) * DH] = _dot(p.astype(bf16), vc) * sig[:, hh:hh + 1]
        for half in range(TQ // 128):
            impt_ref[...] = imp[half * 128:(half + 1) * 128, :].T
            v = _block_scores(impt_ref, n_sel)
            q_pos = qt * TQ + half * 128 + lax.broadcasted_iota(jnp.int32, v.shape, 1)
            val, j = _force_scores(v, q_pos)
            rank = jnp.zeros(v.shape, jnp.int32)
            for k in range(n_sel):
                vk = val[k:k + 1, :]
                rank = rank + jnp.where((vk > val) | ((vk == val) & (j > k)), 1, 0)
            neg = jnp.where(rank < TOPK, 0.0, NEG)
            full = jnp.concatenate([neg, jnp.zeros((128 - n_sel, 128), f32)], axis=0)
            sel_ref[0, g, half * 128:(half + 1) * 128, :] = full.T.astype(bf16)


def _cmp_sel_prompt(q_hm, kc, vc, gates, b, s):
    nq = s // TQ
    return pl.pallas_call(
        functools.partial(_cmp_sel_prompt_body, n_sel=s // SEL_BLOCK),
        out_shape=(SDS((b, s, ATT), f32), SDS((b, G, s, 128), bf16)), grid=(b, nq),
        in_specs=[pl.BlockSpec((1, H, TQ, DH), lambda i, j: (i, 0, j, 0)),
                  pl.BlockSpec((1, G, 128, DH), lambda i, j: (i, 0, 0, 0)),
                  pl.BlockSpec((1, G, 128, DH), lambda i, j: (i, 0, 0, 0)),
                  pl.BlockSpec((TQ, 128), lambda i, j: (i * nq + j, 0))],
        out_specs=(pl.BlockSpec((1, TQ, ATT), lambda i, j: (i, j, 0)),
                   pl.BlockSpec((1, G, TQ, 128), lambda i, j: (i, 0, j, 0))),
        scratch_shapes=[pltpu.VMEM((128, 128), f32)],
        compiler_params=_cp(("parallel", "parallel"), 32), name="cmp_sel_prompt")(q_hm, kc, vc, gates)


def _online_update(s, v, m_ref, l_ref, acc_ref, idx):
    m_old = m_ref[idx]
    m_new = jnp.maximum(m_old, jnp.max(s, axis=-1, keepdims=True))
    alpha = jnp.exp(m_old - m_new)
    p = jnp.exp(s - m_new)
    l_ref[idx] = alpha * l_ref[idx] + jnp.sum(p, axis=-1, keepdims=True)
    acc_ref[idx] = alpha * acc_ref[idx] + _dot(p.astype(bf16), v)
    m_ref[idx] = m_new


def _group_update(qa, ka, va, bias_ref, delta, h0, m_ref, acc_ref, idx0, s_ref, p_ref, a_ref):
    s_ref[...] = _nt(qa, ka)
    for r in range(R):
        for c in range(TQ // RC):
            lo = c * RC
            s = s_ref[r * TQ + lo:r * TQ + lo + RC, :] + bias_ref[delta, h0 + r, lo:lo + RC, :]
            m_old = m_ref[idx0 + r, lo:lo + RC, :]
            m_new = jnp.maximum(m_old, jnp.max(s, axis=-1, keepdims=True))
            p_ref[r * TQ + lo:r * TQ + lo + RC, :] = jnp.exp2(
                s - jnp.concatenate([m_new] * (TQ // 128), axis=1)).astype(bf16)
            m_ref[idx0 + r, lo:lo + RC, :] = m_new
            a_ref[r * TQ + lo:r * TQ + lo + RC, :] = jnp.exp2(m_old - m_new)
    pv = _dot(p_ref[...], va)
    for r in range(R):
        acc_ref[idx0 + r] = a_ref[r * TQ:(r + 1) * TQ, :] * acc_ref[idx0 + r] + pv[r * TQ:(r + 1) * TQ, :]


def _flash_prompt_body(q_ref, ks_ref, vs_ref, kw_ref, vw_ref, bs_ref, bw_ref, sel_ref, et_ref, gt_ref, o_ref,
                       m_ref, acc_ref, s_ref, p_ref, a_ref):
    qt, kt = pl.program_id(1), pl.program_id(2)
    delta = qt - kt
    scratch = (s_ref, p_ref, a_ref)

    @pl.when(kt == 0)
    def _():
        m_ref[...] = jnp.full(m_ref.shape, NEG, f32)
        acc_ref[...] = jnp.zeros(acc_ref.shape, f32)

    @pl.when(kt <= qt)
    def _():
        for g in range(G):
            q3 = q_ref[0, g * R:(g + 1) * R].reshape(R * TQ, DH)
            qa = jnp.concatenate([jnp.concatenate([sel_ref[0, g]] * R, axis=0), q3], axis=1)
            ka = jnp.concatenate([et_ref[0], ks_ref[0, g]], axis=1)
            _group_update(qa, ka, vs_ref[0, g], bs_ref, delta, g * R, m_ref, acc_ref, g * R, *scratch)

    @pl.when((kt <= qt) & (kt >= qt - (WIN // TQ)))
    def _():
        for g in range(G):
            q3 = q_ref[0, g * R:(g + 1) * R].reshape(R * TQ, DH)
            _group_update(q3, kw_ref[0, g], vw_ref[0, g], bw_ref, delta, g * R, m_ref, acc_ref, H + g * R, *scratch)

    @pl.when(kt == qt)
    def _():
        sig = jax.nn.sigmoid(gt_ref[...])
        for hh in range(H):
            a_s, a_w = acc_ref[hh], acc_ref[H + hh]
            o_s = a_s / jnp.maximum(pltpu.roll(a_s, DH, 1), 1e-30)
            o_w = a_w / jnp.maximum(pltpu.roll(a_w, DH, 1), 1e-30)
            o = sig[:, H + hh:H + hh + 1] * o_s + sig[:, 2 * H + hh:2 * H + hh + 1] * o_w
            o_ref[0, :, hh * DH:(hh + 1) * DH] = o[:, 0:DH]


def _flash_prompt(q2, ks, vs, kw, vw, bias_s, bias_w, selneg, et_tiles, gates, b, s):
    nq = s // TQ
    nw = WIN // TQ
    kv = lambda w_, f: pl.BlockSpec((1, G, TQ, w_), lambda i, j, k: (i, 0, f(j, k), 0))
    sel_kt = lambda j, k: jnp.minimum(k, j)
    win_kt = lambda j, k: jnp.clip(k, jnp.maximum(j - nw, 0), j)
    return pl.pallas_call(
        _flash_prompt_body, out_shape=SDS((b, s, ATT), f32), grid=(b, nq, nq),
        in_specs=[pl.BlockSpec((1, H, TQ, DH), lambda i, j, k: (i, 0, j, 0)),
                  kv(DH, sel_kt), kv(2 * DH, sel_kt), kv(DH, win_kt), kv(2 * DH, win_kt),
                  _resident(bias_s.shape), _resident(bias_w.shape),
                  pl.BlockSpec((1, G, TQ, 128), lambda i, j, k: (i, 0, j, 0)),
                  pl.BlockSpec((1, TQ, 128), lambda i, j, k: (jnp.minimum(k, j), 0, 0)),
                  pl.BlockSpec((TQ, 128), lambda i, j, k: (i * nq + j, 0))],
        out_specs=pl.BlockSpec((1, TQ, ATT), lambda i, j, k: (i, j, 0)),
        scratch_shapes=[pltpu.VMEM((2 * H, TQ, 128), f32), pltpu.VMEM((2 * H, TQ, 128), f32),
                        pltpu.VMEM((R * TQ, TQ), f32), pltpu.VMEM((R * TQ, TQ), bf16), pltpu.VMEM((R * TQ, 128), f32)],
        compiler_params=_cp(("parallel", "parallel", "arbitrary"), 58), name="flash_prompt")(
            q2, ks, vs, kw, vw, bias_s, bias_w, selneg, et_tiles, gates)


def _cmp_sel_sample_body(q_ref, kc_ref, vc_ref, gt_ref, e_ref, o_ref, nm_ref, impt_ref, val_ref, *, past, n_q):
    n_cmp = kc_ref.shape[2]
    n_sel = n_cmp // 4
    sig = jax.nn.sigmoid(gt_ref[0])
    rows = R * n_q
    mp = lax.broadcasted_iota(jnp.int32, (rows, n_cmp), 1)
    i_q = lax.broadcasted_iota(jnp.int32, (rows, n_cmp), 0) % n_q
    valid = (mp >= 1) & (16 * mp + 15 <= past + i_q)
    imps = []
    for g in range(G):
        s = jnp.where(valid, _nt(q_ref[0, g], kc_ref[0, g]), NEG)
        m = jnp.max(s, axis=-1, keepdims=True)
        e = jnp.where(valid, jnp.exp(s - m), 0.0)
        p = e / jnp.maximum(jnp.sum(e, axis=-1, keepdims=True), 1e-30)
        o = _dot(p.astype(bf16), vc_ref[0, g]) * sig[g * rows:(g + 1) * rows, 0:1]
        for r in range(R):
            hh = g * R + r
            o_ref[0, :, hh * DH:(hh + 1) * DH] = o[r * n_q:(r + 1) * n_q, :]
        imps.append(p[0:n_q] + p[n_q:2 * n_q] + p[2 * n_q:3 * n_q])
    imp = jnp.concatenate(imps + [jnp.zeros((128 - G * n_q, n_cmp), f32)], axis=0)
    impt_ref[...] = imp.T
    v = _block_scores(impt_ref, n_sel)
    q_pos = past + lax.broadcasted_iota(jnp.int32, v.shape, 1) % n_q
    val, j = _force_scores(v, q_pos)
    val_ref[...] = val

    def count(k, rank):
        vk = val_ref[pl.ds(k, 1), :]
        return rank + jnp.where((vk > val) | ((vk == val) & (j > k)), 1, 0)

    rank = lax.fori_loop(0, n_sel, count, jnp.zeros(v.shape, jnp.int32))
    rank = rank + jnp.where(FORCE > val, 1, 0)
    neg = jnp.where(rank < TOPK, 0.0, NEG).T
    nm_ref[0] = _dot(neg[0:G * n_q, :].astype(bf16), e_ref[...])


def _cmp_sel_sample(qg, kc, vc, gsw, e_full, past):
    db, n_q = qg.shape[0], qg.shape[2] // R
    n_cmp = kc.shape[2]
    return pl.pallas_call(
        functools.partial(_cmp_sel_sample_body, past=past, n_q=n_q),
        out_shape=(SDS((db, n_q, ATT), f32), SDS((db, G * n_q, past), f32)), grid=(db,),
        in_specs=[pl.BlockSpec((1, G, R * n_q, DH), lambda i: (i, 0, 0, 0)),
                  pl.BlockSpec((1, G, n_cmp, DH), lambda i: (i, 0, 0, 0)),
                  pl.BlockSpec((1, G, n_cmp, DH), lambda i: (i, 0, 0, 0)),
                  pl.BlockSpec((1, H * n_q, 128), lambda i: (i, 0, 0)),
                  _resident((n_cmp // 4, past))],
        out_specs=(pl.BlockSpec((1, n_q, ATT), lambda i: (i, 0, 0)),
                   pl.BlockSpec((1, G * n_q, past), lambda i: (i, 0, 0))),
        scratch_shapes=[pltpu.VMEM((n_cmp, 128), f32), pltpu.VMEM((n_cmp // 4, 128), f32)],
        compiler_params=_cp(("parallel",), 32), name="cmp_sel_sample")(qg, kc, vc, gsw, e_full)


def _update_parts(q, parts, bias, m_ref, l_ref, acc_ref, idx):
    s = jnp.concatenate([_dot(q, kt.astype(bf16)) for kt, _ in parts], axis=1) + bias
    m_old = m_ref[idx]
    m_new = jnp.maximum(m_old, jnp.max(s, axis=-1, keepdims=True))
    alpha = jnp.exp(m_old - m_new)
    p = jnp.exp(s - m_new)
    l_ref[idx] = alpha * l_ref[idx] + jnp.sum(p, axis=-1, keepdims=True)
    p = p.astype(bf16)
    pv = acc_ref[idx] * alpha
    off = 0
    for _, vt in parts:
        pv = pv + _nt(p[:, off:off + vt.shape[1]], vt.astype(bf16))
        off += vt.shape[1]
    acc_ref[idx] = pv
    m_ref[idx] = m_new


def _flash_sample_body(*refs, n_q):
    refs = refs[1:]
    pages = refs[:CHUNK_PAGES]
    (q_ref, bias_ref, nm_ref, new_ref, bnew_ref, win_ref, wnew_ref, bwin_ref, gt_ref, o_ref, nwin_ref,
     m_ref, l_ref, acc_ref) = refs[CHUNK_PAGES:]
    step = pl.program_id(1)
    q = q_ref[0]

    @pl.when(step == 0)
    def _():
        m_ref[...] = jnp.full(m_ref.shape, NEG, f32)
        l_ref[...] = jnp.zeros(l_ref.shape, f32)
        acc_ref[...] = jnp.zeros(acc_ref.shape, f32)

    nm = nm_ref[0]
    nm = jnp.concatenate([nm[g * n_q:(g + 1) * n_q] for g in range(G) for _ in range(R)], axis=0)
    parts = [(pages[p][0, 0:G].reshape(G * DH, PAGE), pages[p][0, G:2 * G].reshape(G * DH, PAGE))
             for p in range(CHUNK_PAGES)]
    _update_parts(q, parts, bias_ref[...] + nm, m_ref, l_ref, acc_ref, 0)

    @pl.when(step == pl.num_programs(1) - 1)
    def _():
        new = new_ref[0]
        _update_parts(q, [(new[0:256], new[256:512])], bnew_ref[...], m_ref, l_ref, acc_ref, 0)
        old, wnew = win_ref[0, 0].reshape(8 * DH, WIN), wnew_ref[0]
        _update_parts(q, [(old[0:256], old[256:512]), (wnew[0:256], wnew[256:512])], bwin_ref[...],
                      m_ref, l_ref, acc_ref, 1)
        head, tail = _shifted_state(old, wnew, n_q)
        nwin_ref[0, :, :, 0:WIN - PAGE] = head.reshape(8, DH, WIN - PAGE)
        nwin_ref[0, :, :, WIN - PAGE:WIN] = tail.reshape(8, DH, PAGE)
        sig = jax.nn.sigmoid(gt_ref[0])
        res = (sig[:, 1:2] * acc_ref[0] / jnp.maximum(l_ref[0], 1e-30)
               + sig[:, 2:3] * acc_ref[1] / jnp.maximum(l_ref[1], 1e-30))
        for hh in range(H):
            g = hh // R
            o_ref[0, :, hh * DH:(hh + 1) * DH] = res[hh * n_q:(hh + 1) * n_q, g * DH:(g + 1) * DH]


def _flash_sample(pool, page_table, qbd, bias_slc, negmask, new_slc, bias_new, win_states, li, win_new, bias_win,
                  gsw, n_q):
    db, n_pages = page_table.shape
    steps = n_pages // CHUNK_PAGES
    rows = H * n_q
    keys = CHUNK_PAGES * PAGE
    cm = lambda f: (lambda i, j, pt: f(i, j))
    const = lambda shape: pl.BlockSpec(shape, lambda i, j, pt: (0,) * len(shape), pipeline_mode=pl.Buffered(1))
    per_b = lambda shape: pl.BlockSpec((1,) + shape, lambda i, j, pt: (i,) + (0,) * len(shape))
    page_specs = [pl.BlockSpec((1, 8, DH, PAGE), lambda i, j, pt, k=k: (pt[i, j * CHUNK_PAGES + k], 1, 0, 0))
                  for k in range(CHUNK_PAGES)]
    return pl.pallas_call(
        functools.partial(_flash_sample_body, n_q=n_q),
        out_shape=(SDS((db, n_q, ATT), f32), SDS((db, 8, DH, WIN), f32)),
        grid_spec=pltpu.PrefetchScalarGridSpec(
            num_scalar_prefetch=1, grid=(db, steps),
            in_specs=page_specs + [
                per_b((rows, 256)),
                pl.BlockSpec((rows, keys), cm(lambda i, j: (0, j))),
                pl.BlockSpec((1, G * n_q, keys), cm(lambda i, j: (i, 0, j))),
                per_b((8 * DH, PAGE)), const((rows, PAGE)),
                pl.BlockSpec((1, 1, 8, DH, WIN), lambda i, j, pt: (li, i, 0, 0, 0)),
                per_b((8 * DH, PAGE)), const((rows, WIN + PAGE)),
                per_b((rows, 128))],
            out_specs=(per_b((n_q, ATT)), per_b((8, DH, WIN))),
            scratch_shapes=[pltpu.VMEM((2, rows, 1), f32), pltpu.VMEM((2, rows, 1), f32),
                            pltpu.VMEM((2, rows, 256), f32)]),
        compiler_params=_cp(("arbitrary", "arbitrary"), 48), name="flash_sample")(
            page_table, *([pool] * CHUNK_PAGES), qbd, bias_slc, negmask, new_slc, bias_new,
            win_states, win_new, bias_win, gsw)


def _dil_tile(q, kc, vc, kp, vp, bias_ref, gi, prev_idx):
    lane = lax.broadcasted_iota(jnp.int32, (DT, 128), 1)
    low = lane < DH
    if kp is None:
        kk, vv = kc.astype(bf16), vc.astype(bf16)
    else:
        kk = jnp.concatenate([kp, kc], axis=0).astype(bf16)
        vv = jnp.concatenate([vp, vc], axis=0).astype(bf16)
    outs, lses = [], []
    for e in range(2):
        qe = jnp.where(low if e == 0 else jnp.logical_not(low), q, 0.0).astype(bf16)
        bias = bias_ref[gi, 0, e, 0]
        if kp is not None:
            bias = jnp.concatenate([bias_ref[gi, 0, e, prev_idx], bias], axis=1)
        s = _nt(qe, kk) + bias
        m = jnp.max(s, axis=-1, keepdims=True)
        ex = jnp.exp(s - m)
        l = jnp.maximum(jnp.sum(ex, axis=-1, keepdims=True), 1e-30)
        outs.append(_dot(ex.astype(bf16), vv) / l)
        lses.append(m + jnp.log(l))
    return jnp.where(low, outs[0], outs[1]), jnp.where(low, lses[0], lses[1])


def _dil_combine(os_, ls_):
    mx = jnp.maximum(jnp.maximum(ls_[0], ls_[1]), ls_[2])
    ws = [jnp.exp(l - mx) for l in ls_]
    den = ws[0] + ws[1] + ws[2]
    return [o * w / den for o, w in zip(os_, ws)]


def _dil_prompt_body(q0, q1, q2, k0, k1, k2, v0, v1, v2, bias_ref, y_ref, osc, lsc, *, s):
    qs, ks, vs = (q0, q1, q2), (k0, k1, k2), (v0, v1, v2)
    for gi, (_, dil) in enumerate(DIL):
        n_t = s // dil // DT
        q_r, k_r, v_r = qs[gi], ks[gi], vs[gi]

        def tile(start, prev_start, prev_idx, gi=gi, dil=dil, q_r=q_r, k_r=k_r, v_r=v_r, n_t=n_t):
            rows = lambda st: pl.ds(st, DT, stride=dil) if dil > 1 else pl.ds(pl.multiple_of(st, DT), DT)
            kp = k_r[0, rows(prev_start), :] if n_t > 1 else None
            vp = v_r[0, rows(prev_start), :] if n_t > 1 else None
            o, lse = _dil_tile(q_r[0, rows(start), :], k_r[0, rows(start), :], v_r[0, rows(start), :],
                               kp, vp, bias_ref, gi, prev_idx)
            osc[gi, rows(start), :] = o
            lsc[gi, rows(start), :] = lse

        def body(it, carry, dil=dil, n_t=n_t, tile=tile):
            rho, t = it // n_t, it % n_t
            start = rho + t * (DT * dil)
            prev = rho + jnp.maximum(t - 1, 0) * (DT * dil)
            tile(start, prev, jnp.where(t == 0, 2, 1))
            return carry

        lax.fori_loop(0, dil * n_t, body, 0, unroll=DIL_UNROLL)
    outs = _dil_combine([osc[0], osc[1], osc[2]], [lsc[0], lsc[1], lsc[2]])
    for pair in range(2):
        @pl.when(pl.program_id(1) == pair)
        def _(pair=pair):
            for gi, o in enumerate(outs):
                c = 2 * gi + pair
                y_ref[0, :, c * 128:(c + 1) * 128] = o


def _dil_prompt(zd, bias_tiles, b, s):
    col = lambda base: [pl.BlockSpec((1, s, 128), lambda i, j, c=base + 2 * gi: (i, 0, c + j)) for gi in range(3)]
    return pl.pallas_call(
        functools.partial(_dil_prompt_body, s=s),
        out_shape=SDS((b, s, ATT), f32), grid=(b, 2),
        in_specs=col(0) + col(6) + col(12) + [pl.BlockSpec((3, 1, 2, 3, DT, DT), lambda i, j: (0, j, 0, 0, 0, 0))],
        out_specs=pl.BlockSpec((1, s, ATT), lambda i, j: (i, 0, 0)),
        scratch_shapes=[pltpu.VMEM((3, s, 128), f32), pltpu.VMEM((3, s, 128), f32)],
        compiler_params=_cp(("parallel", "arbitrary"), 48), name="dil_prompt")(*([zd] * 9), bias_tiles)


def _shifted_state(old, new, n_q):
    w = old.shape[1]
    rolled = pltpu.roll(old, w - n_q, 1)
    lane = lax.broadcasted_iota(jnp.int32, new.shape, 1)
    tail = jnp.where(lane >= PAGE - n_q, pltpu.roll(new, PAGE - n_q, 1), rolled[:, w - PAGE:])
    return rolled[:, :w - PAGE], tail


def _dil_sample_body(q_ref, s0, s1, s2, new_ref, b0, b1, b2, y_ref, n0, n1, n2, *, n_q):
    lane = lax.broadcasted_iota(jnp.int32, (n_q, 256), 1)
    os_, ls_ = [], []
    for gi, (st, bias, nst) in enumerate(((s0, b0, n0), (s1, b1, n1), (s2, b2, n2))):
        w = st.shape[4]
        old = st[0, 0].reshape(8 * DH, w)
        new = new_ref[0, gi]
        head, tail = _shifted_state(old, new, n_q)
        if w > PAGE:
            nst[0, :, :, 0:w - PAGE] = head.reshape(8, DH, w - PAGE)
        nst[0, :, :, w - PAGE:w] = tail.reshape(8, DH, PAGE)
        q = q_ref[0, gi]
        s = jnp.concatenate([_dot(q, old[0:256].astype(bf16)), _dot(q, new[0:256].astype(bf16))], axis=1) + bias[...]
        m = jnp.max(s, axis=-1, keepdims=True)
        e = jnp.exp(s - m)
        l = jnp.maximum(jnp.sum(e, axis=-1, keepdims=True), 1e-30)
        eb = e.astype(bf16)
        res = (_nt(eb[:, 0:w], old[256:512].astype(bf16)) + _nt(eb[:, w:], new[256:512].astype(bf16))) / l
        lse = m + jnp.log(l)
        o = jnp.zeros((n_q, 256), f32)
        lm = jnp.zeros((n_q, 256), f32)
        for hg in range(4):
            inb = (lane >= hg * DH) & (lane < (hg + 1) * DH)
            o = jnp.where(inb, res[hg * n_q:(hg + 1) * n_q, :], o)
            lm = jnp.where(inb, lse[hg * n_q:(hg + 1) * n_q, :], lm)
        os_.append(o)
        ls_.append(lm)
    for gi, o in enumerate(_dil_combine(os_, ls_)):
        y_ref[0, :, gi * 256:(gi + 1) * 256] = o


def _dil_sample(qbd, states, li, new_t, biases, n_q):
    db = new_t.shape[0]
    per_b = lambda shape: pl.BlockSpec((1,) + shape, lambda i: (i,) + (0,) * len(shape))
    wins = [st.shape[4] for st in states]
    return pl.pallas_call(
        functools.partial(_dil_sample_body, n_q=n_q),
        out_shape=(SDS((db, n_q, ATT), f32),) + tuple(SDS((db, 8, DH, w), f32) for w in wins), grid=(db,),
        in_specs=[per_b((3, 4 * n_q, 256))]
                 + [pl.BlockSpec((1, 1, 8, DH, w), lambda i: (li, i, 0, 0, 0)) for w in wins]
                 + [per_b((3, 8 * DH, PAGE))] + [_resident(bb.shape) for bb in biases],
        out_specs=(per_b((n_q, ATT)),) + tuple(per_b((8, DH, w)) for w in wins),
        compiler_params=_cp(("parallel",), 48), name="dil_sample")(qbd, *states, new_t, *biases)


def _rel_bucket(dist):
    dist = jnp.maximum(dist, 0)
    d32 = jnp.maximum(dist, 1).astype(f32)
    large = 16 + (jnp.log(d32 / 16) / math.log(2048 / 16) * 16).astype(jnp.int32)
    return jnp.where(dist < 16, dist, jnp.minimum(large, 31))


def _bias_rev(tbl, n, keep, scale=1):
    dist = (n - 1 - jnp.arange(n)) * scale
    return jnp.where(keep(dist), tbl[_rel_bucket(dist)].T, NEG)


def _toeplitz_tiles(bdr, n_delta, t):
    hn = bdr.shape[0]
    bdp = jnp.concatenate([bdr, jnp.full((hn, t - 1), NEG, f32)], axis=1)
    lp = bdp.shape[1]
    gq = jnp.stack([bdp[:, lp + 1 - d * t - 2 * t:lp - d * t] for d in range(n_delta)])
    gq = jnp.pad(gq, ((0, 0), (0, 0), (0, 1)))
    x = jnp.broadcast_to(gq[:, :, None, :], (n_delta, hn, t, 2 * t)).reshape(n_delta, hn, t * 2 * t)
    y = x[..., :t * (2 * t - 1)].reshape(n_delta, hn, t, 2 * t - 1)
    return y[..., t - 1:]


def _rows_by_query(bdr, n_q, n_keys, first_dist):
    last = bdr.shape[1] - 1
    return jnp.stack([bdr[:, last - first_dist - i:last - first_dist - i + n_keys] for i in range(n_q)], axis=1)


def _new_token_bias(tbl, n_q, keep):
    tri = jnp.arange(n_q)[:, None] - jnp.arange(PAGE)[None, :]
    ok = (tri >= 0) & keep(tri)
    return jnp.where(ok[None], jnp.transpose(tbl[_rel_bucket(tri)], (2, 0, 1)), NEG)


def _nsa_tables(rel_bias, s, n_q, past):
    everywhere = lambda d: d >= 0
    in_window = lambda d: d < WIN
    n = past + n_q
    key_blk = jnp.arange(s // TQ)[:, None, None] * (TQ // SEL_BLOCK) + jnp.arange(TQ)[None, :, None] // SEL_BLOCK
    bias_new = _new_token_bias(rel_bias, n_q, everywhere).reshape(H * n_q, PAGE)
    bias_win = _rows_by_query(_bias_rev(rel_bias, n, in_window), n_q, WIN, WIN).reshape(H * n_q, WIN)
    return dict(
        perm=_chunk_perm(),
        bias_s=_toeplitz_tiles(_bias_rev(rel_bias, s, everywhere), s // TQ, TQ) * LOG2E,
        bias_w=_toeplitz_tiles(_bias_rev(rel_bias, WIN + TQ, in_window), WIN // TQ + 1, TQ) * LOG2E,
        et_tiles=(key_blk == jnp.arange(128)[None, None, :]).astype(bf16),
        e_full=(jnp.arange(past // SEL_BLOCK)[:, None] == jnp.arange(past)[None, :] // SEL_BLOCK).astype(bf16),
        bias_slc=_rows_by_query(_bias_rev(rel_bias, n, everywhere), n_q, past, past).reshape(H * n_q, past),
        bias_new=bias_new,
        bias_win=jnp.concatenate([bias_win, bias_new], axis=1))


def _nsa_layer(yp, ys, g_pre, g_post, w_in, w_out, cmp_pos, cmp_w1, cmp_w2, tb, pool, page_table, win_states,
               li, b, s, db, n_q, past):
    w_pad = jnp.pad(w_in, ((0, 0), (0, NSA_PAD - w_in.shape[1]))).astype(bf16)
    w_out = w_out.astype(bf16)
    w1cat = jnp.concatenate([cmp_w1[:, :1024], cmp_w1[:, 1024:]], axis=-1).astype(bf16)
    pos2 = cmp_pos.reshape(2, 2, 1024)
    w2 = cmp_w2.astype(bf16)

    q_hm, q2_hm, kvt, kwin, ks, vs, kw, vw, gates = _nsa_proj_prompt(yp, g_pre, w_pad, b, s)
    kc, vc = _compress_prompt(kvt, tb["perm"], w1cat, pos2, w2, b, s)
    o_cmp, selneg = _cmp_sel_prompt(q_hm, kc, vc, gates, b, s)
    o_sw = _flash_prompt(q2_hm, ks, vs, kw, vw, tb["bias_s"], tb["bias_w"], selneg, tb["et_tiles"], gates, b, s)
    yp = _out_proj(yp, g_post, w_out, o_cmp.reshape(b * s, ATT), o_sw.reshape(b * s, ATT))
    kv_p = jnp.transpose(kvt.reshape(b, 4, G, DH, s), (0, 4, 1, 2, 3))
    win_p = kwin.reshape(b, s, 2, G, DH)[:, s - min(WIN, s):]

    z = _proj(ys, g_pre, w_pad)
    q = (z[:, :ATT] * (DH ** -0.5)).reshape(db, n_q, H, DH).transpose(0, 2, 1, 3)
    qg = q.reshape(db, G, R * n_q, DH).astype(bf16)
    eye = jnp.eye(G, dtype=f32)
    qbd = (q.reshape(db, G, R * n_q, 1, DH) * eye[None, :, None, :, None]).reshape(db, H * n_q, G * DH).astype(bf16)
    kv_new = z[:, ATT:ATT + 1024].reshape(db, n_q, 1024)
    win_new = z[:, ATT + 1024:ATT + 1536].reshape(db, n_q, 512)
    gt = z[:, ATT + 1536:ATT + 1536 + 3 * H].reshape(db, n_q, 3, H).transpose(0, 3, 1, 2)
    gsw = jnp.pad(gt.reshape(db, H * n_q, 3), ((0, 0), (0, 0), (0, 125)))
    kc_s, vc_s = _compress_sample(pool, page_table, tb["perm"], w1cat, pos2, w2)
    o_cmp_s, negmask = _cmp_sel_sample(qg, kc_s, vc_s, gsw, tb["e_full"], past)
    page_t = lambda x: jnp.pad(jnp.transpose(x, (0, 2, 1)), ((0, 0), (0, 0), (0, PAGE - n_q)))
    win_view = jnp.transpose(win_states, (0, 1, 3, 4, 5, 2)).reshape(win_states.shape[0], db, 8, DH, WIN)
    o_sw_s, win_t = _flash_sample(pool, page_table, qbd, tb["bias_slc"], negmask, page_t(kv_new[:, :, 512:]),
                                  tb["bias_new"], win_view, li, page_t(win_new), tb["bias_win"], gsw, n_q)
    ys = _out_proj(ys, g_post, w_out, o_cmp_s.reshape(db * n_q, ATT), o_sw_s.reshape(db * n_q, ATT))
    kv_s = kv_new.reshape(db, n_q, 4, G, DH)
    win_s = jnp.transpose(win_t.reshape(db, 2, G, DH, WIN), (0, 4, 1, 2, 3))
    return yp, ys, kv_p, win_p, kv_s, win_s


def _dil_tables(rel_bias, n_q):
    tiles, biases = [], []
    for gi, (win, dil) in enumerate(DIL):
        tbl = rel_bias[:, gi * 4:(gi + 1) * 4]
        on_grid = lambda d, win=win, dil=dil: (d % dil == 0) & (d <= win)
        tl = _toeplitz_tiles(_bias_rev(tbl, 2 * DT, on_grid, scale=dil), 2, DT)
        tiles.append(jnp.concatenate([tl, jnp.full_like(tl[:1], NEG)]))
        b_all = jnp.concatenate([_rows_by_query(_bias_rev(tbl, win + n_q, on_grid), n_q, win, win),
                                 _new_token_bias(tbl, n_q, on_grid)], axis=2)
        biases.append(b_all.reshape(4 * n_q, win + PAGE))
    return dict(tiles=jnp.stack(tiles).transpose(0, 2, 1, 3, 4).reshape(3, 2, 2, 3, DT, DT), sample=biases)


def _dil_layer(yp, ys, g_pre, g_post, w_in, w_out, tb, states, li, b, s, db, n_q):
    w_in = w_in.astype(bf16)
    w_out = w_out.astype(bf16)

    zd, kvt = _dil_proj(yp, g_pre, w_in, b, s)
    y = _dil_prompt(zd.reshape(b, s, 3 * ATT), tb["tiles"], b, s)
    yp = _out_proj(yp, g_post, w_out, y.reshape(b * s, ATT))
    kvt = kvt.reshape(b, 2, H, DH, s)
    st_p = []
    for gi, (win, _) in enumerate(DIL):
        keep = min(win, s)
        st_p.append(jnp.transpose(kvt[:, :, gi * 4:(gi + 1) * 4, :, s - keep:], (0, 4, 1, 2, 3)))

    z = _proj(ys, g_pre, w_in).reshape(db, n_q, 3, 3, 4, DH)
    q = (z[:, :, 0] * (DH ** -0.5)).transpose(0, 2, 3, 1, 4)
    eye = jnp.eye(4, dtype=f32)
    qbd = (q[:, :, :, :, None, :] * eye[None, None, :, None, :, None]).reshape(db, 3, 4 * n_q, 256).astype(bf16)
    new_t = jnp.transpose(z[:, :, 1:3], (0, 3, 2, 4, 5, 1)).reshape(db, 3, 8 * DH, n_q)
    new_t = jnp.pad(new_t, ((0, 0), (0, 0), (0, 0), (0, PAGE - n_q)))
    views = [jnp.transpose(st, (0, 1, 3, 4, 5, 2)).reshape(st.shape[0], db, 8, DH, st.shape[2]) for st in states]
    y_s, n0, n1, n2 = _dil_sample(qbd, views, li, new_t, tb["sample"], n_q)
    ys = _out_proj(ys, g_post, w_out, y_s.reshape(db * n_q, ATT))
    st_s = [jnp.transpose(n.reshape(db, 2, 4, DH, n.shape[3]), (0, 4, 1, 2, 3)) for n in (n0, n1, n2)]
    return yp, ys, st_p, st_s


def kernel(x_prompt, x_sample, cache_nsa_kv, page_table, state_nsa_win, state_dil_w128, state_dil_w512,
           state_dil_w2048, norm_g, ffn_w_in, ffn_w_out, nsa_w_in, nsa_w_out, nsa_cmp_pos, nsa_cmp_w1,
           nsa_cmp_w2, dil_w_in, dil_w_out, rel_bias):
    b, s, _ = x_prompt.shape
    db, n_q, _ = x_sample.shape
    depth = norm_g.shape[0]
    past = page_table.shape[1] * PAGE
    yp = x_prompt.reshape(b * s, D)
    ys = x_sample.reshape(db * n_q, D)
    nsa_kv_p, nsa_win_p, nsa_kv_s, nsa_win_s = [], [], [], []
    dil_p, dil_s = [[], [], []], [[], [], []]
    nsa_tb = _nsa_tables(rel_bias, s, n_q, past)
    dil_tb = _dil_tables(rel_bias, n_q)
    for i in range(depth):
        li = i // 2
        g = norm_g[i]
        w_i, w_o = ffn_w_in[i, 0].astype(bf16), ffn_w_out[i, 0].astype(bf16)
        yp = _half_ffn(yp, g[0:2], w_i, w_o)
        ys = _half_ffn(ys, g[0:2], w_i, w_o)
        if i % 2 == 0:
            n_pool = cache_nsa_kv.shape[1]
            pool = jnp.transpose(cache_nsa_kv, (0, 1, 3, 4, 5, 2)).reshape(
                cache_nsa_kv.shape[0] * n_pool, 4 * G, DH, PAGE)
            yp, ys, kvp, winp, kvs, wins = _nsa_layer(
                yp, ys, g[2:3], g[3:4], nsa_w_in[li], nsa_w_out[li], nsa_cmp_pos[li], nsa_cmp_w1[li],
                nsa_cmp_w2[li], nsa_tb, pool, page_table + li * n_pool, state_nsa_win, li, b, s, db, n_q, past)
            nsa_kv_p.append(kvp)
            nsa_win_p.append(winp)
            nsa_kv_s.append(kvs)
            nsa_win_s.append(wins)
        else:
            states = (state_dil_w128, state_dil_w512, state_dil_w2048)
            yp, ys, stp, sts = _dil_layer(yp, ys, g[2:3], g[3:4], dil_w_in[li], dil_w_out[li], dil_tb, states,
                                          li, b, s, db, n_q)
            for gi in range(3):
                dil_p[gi].append(stp[gi])
                dil_s[gi].append(sts[gi])
        w_i, w_o = ffn_w_in[i, 1].astype(bf16), ffn_w_out[i, 1].astype(bf16)
        yp = _half_ffn(yp, g[4:6], w_i, w_o)
        ys = _half_ffn(ys, g[4:6], w_i, w_o)
    return (yp.reshape(b, s, D), ys.reshape(db, n_q, D),
            jnp.stack(nsa_kv_p), jnp.stack(nsa_win_p),
            jnp.stack(dil_p[0]), jnp.stack(dil_p[1]), jnp.stack(dil_p[2]),
            jnp.stack(nsa_kv_s), jnp.stack(nsa_win_s),
            jnp.stack(dil_s[0]), jnp.stack(dil_s[1]), jnp.stack(dil_s[2]))
```

```python
import functools
import math

import jax
import jax.numpy as jnp
from jax import lax
from jax.experimental import pallas as pl
from jax.experimental.pallas import tpu as pltpu

f32, bf16 = jnp.float32, jnp.bfloat16
SDS = jax.ShapeDtypeStruct

D = 1024
FF = 2816
FF_CHUNK = 256
H, DH = 12, 64
G, R = 4, 3
ATT = H * DH
EPS = 1e-6
NEG = -1e30
LOG2E = math.log2(math.e)
FORCE = 1e4
SEL_BLOCK, TOPK = 64, 16
WIN = 512
NSA_PAD = 2432
TQ = 256
RC = 64
PAGE = 128
CHUNK_PAGES = 32
CMP_GROUP = 1
DIL = ((128, 1), (512, 4), (2048, 16))
DT = 128
DIL_UNROLL = 4
MIB = 1024 * 1024


def _cp(sem, vmem_mib):
    return pltpu.CompilerParams(dimension_semantics=sem, vmem_limit_bytes=vmem_mib * MIB)


def _rms(x, g):
    return x * lax.rsqrt(jnp.mean(x * x, axis=-1, keepdims=True) + EPS) * g


def _nt(a, b):
    return lax.dot_general(a, b, (((1,), (1,)), ((), ())), preferred_element_type=f32)


def _dot(a, b):
    return jnp.dot(a, b, preferred_element_type=f32)


def _resident(shape):
    nd = len(shape)
    return pl.BlockSpec(shape, lambda *_: (0,) * nd, pipeline_mode=pl.Buffered(1))


def _ffn_body(x_ref, g_ref, win_ref, wout_ref, o_ref):
    x = x_ref[...]
    h = _rms(x, g_ref[0:1, :]).astype(bf16)
    acc = jnp.zeros(x.shape, f32)
    for c in range(FF // FF_CHUNK):
        lo, hi = c * FF_CHUNK, (c + 1) * FF_CHUNK
        gate = _dot(h, win_ref[:, lo:hi])
        up = _dot(h, win_ref[:, FF + lo:FF + hi])
        act = (gate * jax.nn.sigmoid(gate) * up).astype(bf16)
        acc = acc + _dot(act, wout_ref[lo:hi, :])
    o_ref[...] = x + 0.5 * _rms(acc, g_ref[1:2, :])


def _half_ffn(x, g2, w_in, w_out):
    m = x.shape[0]
    tm = min(m, 512)
    return pl.pallas_call(
        _ffn_body, out_shape=SDS((m, D), f32), grid=(m // tm,),
        in_specs=[pl.BlockSpec((tm, D), lambda i: (i, 0)), _resident((2, D)),
                  _resident((D, 2 * FF)), _resident((FF, D))],
        out_specs=pl.BlockSpec((tm, D), lambda i: (i, 0)),
        compiler_params=_cp(("parallel",), 48), name="half_ffn")(x, g2, w_in, w_out)


def _proj_body(x_ref, g_ref, w_ref, o_ref):
    h = _rms(x_ref[...], g_ref[...]).astype(bf16)
    o_ref[...] = _dot(h, w_ref[...])


def _proj(x, g, w):
    m, n = x.shape[0], w.shape[1]
    return pl.pallas_call(
        _proj_body, out_shape=SDS((m, n), f32), grid=(1,),
        in_specs=[_resident((m, D)), _resident((1, D)), _resident((D, n))],
        out_specs=pl.BlockSpec((m, n), lambda i: (0, 0)),
        compiler_params=_cp(("arbitrary",), 32), name="proj_sample")(x, g, w)


def _nsa_proj_body(x_ref, g_ref, w_ref, q_ref, q2_ref, kv_ref, kwin_ref, ks_ref, vs_ref, kw_ref, vw_ref, gt_ref):
    h = _rms(x_ref[...], g_ref[...]).astype(bf16)
    zq = _dot(h, w_ref[:, 0:ATT]) * (DH ** -0.5)
    zq2 = zq * LOG2E
    for hh in range(H):
        q_ref[0, hh] = zq[:, hh * DH:(hh + 1) * DH].astype(bf16)
        q2_ref[0, hh] = zq2[:, hh * DH:(hh + 1) * DH].astype(bf16)
    zc = _dot(h, w_ref[:, ATT:ATT + 512])
    kv_ref[0, 0:2 * G] = zc.T.reshape(2 * G, DH, zc.shape[0])
    zs = _dot(h, w_ref[:, ATT + 512:ATT + 1024])
    kv_ref[0, 2 * G:4 * G] = zs.T.reshape(2 * G, DH, zs.shape[0])
    zw = _dot(h, w_ref[:, ATT + 1024:ATT + 1536])
    kwin_ref[...] = zw
    ones = jnp.ones((zs.shape[0], DH), bf16)
    for g in range(G):
        ks_ref[0, g] = zs[:, g * DH:(g + 1) * DH].astype(bf16)
        vs_ref[0, g] = jnp.concatenate([zs[:, 256 + g * DH:256 + (g + 1) * DH].astype(bf16), ones], axis=1)
        kw_ref[0, g] = zw[:, g * DH:(g + 1) * DH].astype(bf16)
        vw_ref[0, g] = jnp.concatenate([zw[:, 256 + g * DH:256 + (g + 1) * DH].astype(bf16), ones], axis=1)
    gt_ref[...] = _dot(h, w_ref[:, ATT + 1536:NSA_PAD])


def _nsa_proj_prompt(x, g, w, b, s):
    tm = 512
    nt = s // tm
    hm = lambda n, w_: pl.BlockSpec((1, n, tm, w_), lambda i: (i // nt, 0, i % nt, 0))
    rows = lambda n: pl.BlockSpec((tm, n), lambda i: (i, 0))
    return pl.pallas_call(
        _nsa_proj_body,
        out_shape=(SDS((b, H, s, DH), bf16), SDS((b, H, s, DH), bf16), SDS((b, 4 * G, DH, s), f32),
                   SDS((b * s, 512), f32), SDS((b, G, s, DH), bf16), SDS((b, G, s, 2 * DH), bf16),
                   SDS((b, G, s, DH), bf16), SDS((b, G, s, 2 * DH), bf16), SDS((b * s, 128), f32)),
        grid=(b * s // tm,),
        in_specs=[rows(D), _resident((1, D)), _resident((D, NSA_PAD))],
        out_specs=(hm(H, DH), hm(H, DH), pl.BlockSpec((1, 4 * G, DH, tm), lambda i: (i // nt, 0, 0, i % nt)),
                   rows(512), hm(G, DH), hm(G, 2 * DH), hm(G, DH), hm(G, 2 * DH), rows(128)),
        compiler_params=_cp(("parallel",), 48), name="nsa_proj_prompt")(x, g, w)


def _dil_proj_body(x_ref, g_ref, w_ref, o_ref, kvt_ref):
    h = _rms(x_ref[...], g_ref[...]).astype(bf16)
    o_ref[:, 0:ATT] = _dot(h, w_ref[:, 0:ATT]) * (DH ** -0.5)
    for part in (1, 2):
        z = _dot(h, w_ref[:, part * ATT:(part + 1) * ATT])
        o_ref[:, part * ATT:(part + 1) * ATT] = z
        kvt_ref[0, (part - 1) * H:part * H] = z.T.reshape(H, DH, z.shape[0])


def _dil_proj(x, g, w, b, s):
    tm = 512
    nt = s // tm
    return pl.pallas_call(
        _dil_proj_body, out_shape=(SDS((b * s, 3 * ATT), f32), SDS((b, 2 * H, DH, s), f32)), grid=(b * s // tm,),
        in_specs=[pl.BlockSpec((tm, D), lambda i: (i, 0)), _resident((1, D)), _resident((D, 3 * ATT))],
        out_specs=(pl.BlockSpec((tm, 3 * ATT), lambda i: (i, 0)),
                   pl.BlockSpec((1, 2 * H, DH, tm), lambda i: (i // nt, 0, 0, i % nt))),
        compiler_params=_cp(("parallel",), 48), name="dil_proj")(x, g, w)


def _out_proj_body(x_ref, g_ref, w_ref, *refs):
    o_ref = refs[-1]
    o = refs[0][...]
    for r in refs[1:-1]:
        o = o + r[...]
    y = _dot(o.astype(bf16), w_ref[...])
    o_ref[...] = x_ref[...] + _rms(y, g_ref[...])


def _out_proj(x, g, w, *branches):
    m = x.shape[0]
    tm = min(m, 512)
    row = lambda n: pl.BlockSpec((tm, n), lambda i: (i, 0))
    return pl.pallas_call(
        _out_proj_body, out_shape=SDS((m, D), f32), grid=(m // tm,),
        in_specs=[row(D), _resident((1, D)), _resident((ATT, D))] + [row(ATT)] * len(branches),
        out_specs=row(D), compiler_params=_cp(("parallel",), 32), name="out_proj")(x, g, w, *branches)


def _compress_body(*refs, n_src, n_pages, paged, group=1):
    if paged:
        refs = refs[1:]
    srcs = refs[:n_src]
    perm_ref, w1_ref, pos_ref, w2_ref, kc_ref, vc_ref, lhs_ref, aprev_ref = refs[n_src:]
    nch = n_pages * PAGE // 16
    gn = group * nch
    step = pl.program_id(1) if paged else 0
    sub = step % group if paged else 0

    for p in range(n_pages):
        mats = srcs[p][0] if paged else srcs[0][0, :, :, p * PAGE:(p + 1) * PAGE]
        both = jnp.concatenate([jnp.concatenate([mats[2 * c], mats[2 * c + 1]], axis=1) for c in range(4)], axis=0)
        out = _dot(both.astype(bf16), perm_ref[...])
        for c in range(4):
            kind, pair = c // 2, c % 2
            o = out[c * DH:(c + 1) * DH]
            zt = jnp.concatenate([o[:, 0:128], o[:, 128:256]], axis=0).T
            for gl in range(2):
                start = (2 * pair + gl) * gn + sub * nch + p * 8
                dst = pl.ds(pl.multiple_of(start, 8), 8) if paged else pl.ds(start, 8)
                for r2 in range(8):
                    lhs_ref[kind, dst, r2 * 128:(r2 + 1) * 128] = zt[(gl * 8 + r2) * 8:(gl * 8 + r2 + 1) * 8, :]

    def mlp():
        first = (step < group) if paged else True
        if paged:
            @pl.when(step == group - 1)
            def _():
                aprev_ref[...] = jnp.zeros(aprev_ref.shape, f32)

        row = lax.broadcasted_iota(jnp.int32, (gn, 256), 0)
        row_o = lax.broadcasted_iota(jnp.int32, (gn, DH), 0)
        for kind, out_ref in ((0, kc_ref), (1, vc_ref)):
            pp = _dot(lhs_ref[kind].astype(bf16), w1_ref[kind])
            pos = _dot(pos_ref[kind].astype(bf16), w1_ref[kind])
            posterm = pos[0:1, 0:256] + pos[1:2, 256:512]
            for g in range(G):
                a = pp[g * gn:(g + 1) * gn, 0:256]
                bm = pp[g * gn:(g + 1) * gn, 256:512]
                shifted = pltpu.roll(a, 1, 0)
                if paged:
                    carry = aprev_ref[kind, (g + 1) * gn - 1:(g + 1) * gn, :]
                    shifted = jnp.where(row == 0, carry, shifted)
                    aprev_ref[kind, g * gn:(g + 1) * gn, :] = a
                pre = shifted + bm + posterm
                hid = (pre * (0.5 + 0.5 * jnp.tanh(0.5 * pre))).astype(bf16)
                out = _dot(hid, w2_ref[kind])
                out = jnp.where((row_o == 0) & first, 0.0, out)
                out_ref[0, g] = out.astype(bf16)

    if paged:
        pl.when(sub == group - 1)(mlp)
    else:
        mlp()


def _chunk_perm():
    t = jnp.arange(PAGE)
    col = (t % 2) * 128 + ((t % 16) // 2) * 8 + t // 16
    src = jnp.concatenate([col, col + DH])
    return (src[:, None] == jnp.arange(256)[None, :]).astype(bf16)


def _compress_prompt(kvt, perm, w1cat, pos2, w2, b, s):
    nch = s // 16
    out = SDS((b, G, nch, DH), bf16)
    ospec = pl.BlockSpec((1, G, nch, DH), lambda i: (i, 0, 0, 0))
    return pl.pallas_call(
        functools.partial(_compress_body, n_src=1, n_pages=s // PAGE, paged=False),
        out_shape=(out, out), grid=(b,),
        in_specs=[pl.BlockSpec((1, 8, DH, s), lambda i: (i, 0, 0, 0)), _resident((256, 256)),
                  _resident((2, 1024, 512)), _resident((2, 2, 1024)), _resident((2, 256, DH))],
        out_specs=(ospec, ospec),
        scratch_shapes=[pltpu.VMEM((2, G * nch, 1024), f32), pltpu.VMEM((2, G * nch, 256), f32)],
        compiler_params=_cp(("parallel",), 48), name="compress_prompt")(kvt, perm, w1cat, pos2, w2)


def _compress_sample(pool_t, page_table, perm, w1cat, pos2, w2):
    db, n_pages = page_table.shape
    steps = n_pages // CHUNK_PAGES
    gn = CMP_GROUP * CHUNK_PAGES * PAGE // 16
    out = SDS((db, G, n_pages * PAGE // 16, DH), bf16)
    ospec = pl.BlockSpec((1, G, gn, DH), lambda i, j, pt: (i, 0, j // CMP_GROUP, 0))
    const = lambda shape: pl.BlockSpec(shape, lambda i, j, pt: (0,) * len(shape), pipeline_mode=pl.Buffered(1))
    page_specs = [pl.BlockSpec((1, 8, DH, PAGE), lambda i, j, pt, k=k: (pt[i, j * CHUNK_PAGES + k], 0, 0, 0))
                  for k in range(CHUNK_PAGES)]
    return pl.pallas_call(
        functools.partial(_compress_body, n_src=CHUNK_PAGES, n_pages=CHUNK_PAGES, paged=True, group=CMP_GROUP),
        out_shape=(out, out),
        grid_spec=pltpu.PrefetchScalarGridSpec(
            num_scalar_prefetch=1, grid=(db, steps),
            in_specs=page_specs + [const((256, 256)), const((2, 1024, 512)), const((2, 2, 1024)),
                                   const((2, 256, DH))],
            out_specs=(ospec, ospec),
            scratch_shapes=[pltpu.VMEM((2, G * gn, 1024), f32), pltpu.VMEM((2, G * gn, 256), f32)]),
        compiler_params=_cp(("arbitrary", "arbitrary"), 48), name="compress_sample")(
            page_table, *([pool_t] * CHUNK_PAGES), perm, w1cat, pos2, w2)


def _block_scores(impt_ref, n_blocks):
    return (impt_ref[pl.ds(1, n_blocks, stride=4), :] + impt_ref[pl.ds(2, n_blocks, stride=4), :]
            + impt_ref[pl.ds(3, n_blocks, stride=4), :])


def _rank_desc(val):
    n = val.shape[0]
    tiles = [val[8 * t:8 * t + 8] for t in range(n // 8)]
    ranks = [jnp.zeros((8, val.shape[1]), jnp.int32) for _ in tiles]
    row = lax.broadcasted_iota(jnp.int32, (8, val.shape[1]), 0)
    for k in range(n):
        vk = val[k:k + 1, :]
        for t, blk in enumerate(tiles):
            if 8 * t > k:
                beats = vk >= blk
            elif 8 * t + 7 < k:
                beats = vk > blk
            else:
                beats = (vk > blk) | ((vk == blk) & (row + 8 * t > k))
            ranks[t] = ranks[t] + jnp.where(beats, 1, 0)
    return jnp.concatenate(ranks, axis=0)


def _force_scores(v, q_pos):
    j = lax.broadcasted_iota(jnp.int32, v.shape, 0)
    cur = q_pos >> 6
    forced = (j == 0) | (j == cur) | (j == cur - 1)
    return jnp.where(forced, FORCE, jnp.where(j > cur, -FORCE, v)), j


def _cmp_sel_prompt_body(q_ref, kc_ref, vc_ref, gt_ref, o_ref, sel_ref, impt_ref, *, n_sel):
    qt = pl.program_id(1)
    sig = jax.nn.sigmoid(gt_ref[...])
    t = qt * TQ + lax.broadcasted_iota(jnp.int32, (TQ, 128), 0)
    mp = lax.broadcasted_iota(jnp.int32, (TQ, 128), 1)
    valid = (mp >= 1) & (16 * mp + 15 <= t)
    for g in range(G):
        kc, vc = kc_ref[0, g], vc_ref[0, g]
        imp = jnp.zeros((TQ, 128), f32)
        for r in range(R):
            hh = g * R + r
            s = jnp.where(valid, _nt(q_ref[0, hh], kc), NEG)
            m = jnp.max(s, axis=-1, keepdims=True)
            e = jnp.where(valid, jnp.exp(s - m), 0.0)
            p = e / jnp.maximum(jnp.sum(e, axis=-1, keepdims=True), 1e-30)
            imp = imp + p
            o_ref[0, :, hh * DH:(hh + 1) * DH] = _dot(p.astype(bf16), vc) * sig[:, hh:hh + 1]
        for half in range(TQ // 128):
            impt_ref[...] = imp[half * 128:(half + 1) * 128, :].T
            v = _block_scores(impt_ref, n_sel)
            q_pos = qt * TQ + half * 128 + lax.broadcasted_iota(jnp.int32, v.shape, 1)
            val, _ = _force_scores(v, q_pos)
            neg = jnp.where(_rank_desc(val) < TOPK, 0.0, NEG)
            full = jnp.concatenate([neg, jnp.zeros((128 - n_sel, 128), f32)], axis=0)
            sel_ref[0, g, half * 128:(half + 1) * 128, :] = full.T.astype(bf16)


def _cmp_sel_prompt(q_hm, kc, vc, gates, b, s):
    nq = s // TQ
    return pl.pallas_call(
        functools.partial(_cmp_sel_prompt_body, n_sel=s // SEL_BLOCK),
        out_shape=(SDS((b, s, ATT), f32), SDS((b, G, s, 128), bf16)), grid=(b, nq),
        in_specs=[pl.BlockSpec((1, H, TQ, DH), lambda i, j: (i, 0, j, 0)),
                  pl.BlockSpec((1, G, 128, DH), lambda i, j: (i, 0, 0, 0)),
                  pl.BlockSpec((1, G, 128, DH), lambda i, j: (i, 0, 0, 0)),
                  pl.BlockSpec((TQ, 128), lambda i, j: (i * nq + j, 0))],
        out_specs=(pl.BlockSpec((1, TQ, ATT), lambda i, j: (i, j, 0)),
                   pl.BlockSpec((1, G, TQ, 128), lambda i, j: (i, 0, j, 0))),
        scratch_shapes=[pltpu.VMEM((128, 128), f32)],
        compiler_params=_cp(("parallel", "parallel"), 32), name="cmp_sel_prompt")(q_hm, kc, vc, gates)


def _group_update(qa, ka, va, bias_ref, delta, h0, m_ref, acc_ref, idx0, s_ref, p_ref, a_ref):
    s_ref[...] = _nt(qa, ka)
    for r in range(R):
        for c in range(TQ // RC):
            lo = c * RC
            s = s_ref[r * TQ + lo:r * TQ + lo + RC, :] + bias_ref[delta, h0 + r, lo:lo + RC, :]
            m_old = m_ref[idx0 + r, lo:lo + RC, :]
            m_new = jnp.maximum(m_old, jnp.max(s, axis=-1, keepdims=True))
            p_ref[r * TQ + lo:r * TQ + lo + RC, :] = jnp.exp2(
                s - jnp.concatenate([m_new] * (TQ // 128), axis=1)).astype(bf16)
            m_ref[idx0 + r, lo:lo + RC, :] = m_new
            a_ref[r * TQ + lo:r * TQ + lo + RC, :] = jnp.exp2(m_old - m_new)
    pv = _dot(p_ref[...], va)
    for r in range(R):
        acc_ref[idx0 + r] = a_ref[r * TQ:(r + 1) * TQ, :] * acc_ref[idx0 + r] + pv[r * TQ:(r + 1) * TQ, :]


def _flash_prompt_body(q_ref, ks_ref, vs_ref, kw_ref, vw_ref, bs_ref, bw_ref, sel_ref, et_ref, gt_ref, o_ref,
                       m_ref, acc_ref, s_ref, p_ref, a_ref):
    qt, kt = pl.program_id(1), pl.program_id(2)
    delta = qt - kt
    scratch = (s_ref, p_ref, a_ref)

    @pl.when(kt == 0)
    def _():
        m_ref[...] = jnp.full(m_ref.shape, NEG, f32)
        acc_ref[...] = jnp.zeros(acc_ref.shape, f32)

    @pl.when(kt <= qt)
    def _():
        for g in range(G):
            q3 = q_ref[0, g * R:(g + 1) * R].reshape(R * TQ, DH)
            qa = jnp.concatenate([jnp.concatenate([sel_ref[0, g]] * R, axis=0), q3], axis=1)
            ka = jnp.concatenate([et_ref[0], ks_ref[0, g]], axis=1)
            _group_update(qa, ka, vs_ref[0, g], bs_ref, delta, g * R, m_ref, acc_ref, g * R, *scratch)

    @pl.when((kt <= qt) & (kt >= qt - (WIN // TQ)))
    def _():
        for g in range(G):
            q3 = q_ref[0, g * R:(g + 1) * R].reshape(R * TQ, DH)
            _group_update(q3, kw_ref[0, g], vw_ref[0, g], bw_ref, delta, g * R, m_ref, acc_ref, H + g * R, *scratch)

    @pl.when(kt == qt)
    def _():
        sig = jax.nn.sigmoid(gt_ref[...])
        for hh in range(H):
            a_s, a_w = acc_ref[hh], acc_ref[H + hh]
            o_s = a_s / jnp.maximum(pltpu.roll(a_s, DH, 1), 1e-30)
            o_w = a_w / jnp.maximum(pltpu.roll(a_w, DH, 1), 1e-30)
            o = sig[:, H + hh:H + hh + 1] * o_s + sig[:, 2 * H + hh:2 * H + hh + 1] * o_w
            o_ref[0, :, hh * DH:(hh + 1) * DH] = o[:, 0:DH]


def _flash_prompt(q2, ks, vs, kw, vw, bias_s, bias_w, selneg, et_tiles, gates, b, s):
    nq = s // TQ
    nw = WIN // TQ
    kv = lambda w_, f: pl.BlockSpec((1, G, TQ, w_), lambda i, j, k: (i, 0, f(j, k), 0))
    sel_kt = lambda j, k: jnp.minimum(k, j)
    win_kt = lambda j, k: jnp.clip(k, jnp.maximum(j - nw, 0), j)
    return pl.pallas_call(
        _flash_prompt_body, out_shape=SDS((b, s, ATT), f32), grid=(b, nq, nq),
        in_specs=[pl.BlockSpec((1, H, TQ, DH), lambda i, j, k: (i, 0, j, 0)),
                  kv(DH, sel_kt), kv(2 * DH, sel_kt), kv(DH, win_kt), kv(2 * DH, win_kt),
                  _resident(bias_s.shape), _resident(bias_w.shape),
                  pl.BlockSpec((1, G, TQ, 128), lambda i, j, k: (i, 0, j, 0)),
                  pl.BlockSpec((1, TQ, 128), lambda i, j, k: (jnp.minimum(k, j), 0, 0)),
                  pl.BlockSpec((TQ, 128), lambda i, j, k: (i * nq + j, 0))],
        out_specs=pl.BlockSpec((1, TQ, ATT), lambda i, j, k: (i, j, 0)),
        scratch_shapes=[pltpu.VMEM((2 * H, TQ, 128), f32), pltpu.VMEM((2 * H, TQ, 128), f32),
                        pltpu.VMEM((R * TQ, TQ), f32), pltpu.VMEM((R * TQ, TQ), bf16), pltpu.VMEM((R * TQ, 128), f32)],
        compiler_params=_cp(("parallel", "parallel", "arbitrary"), 58), name="flash_prompt")(
            q2, ks, vs, kw, vw, bias_s, bias_w, selneg, et_tiles, gates)


def _cmp_sel_sample_body(q_ref, kc_ref, vc_ref, gt_ref, e_ref, o_ref, nm_ref, impt_ref, *, past, n_q):
    n_cmp = kc_ref.shape[2]
    n_sel = n_cmp // 4
    sig = jax.nn.sigmoid(gt_ref[0])
    rows = R * n_q
    mp = lax.broadcasted_iota(jnp.int32, (rows, n_cmp), 1)
    i_q = lax.broadcasted_iota(jnp.int32, (rows, n_cmp), 0) % n_q
    valid = (mp >= 1) & (16 * mp + 15 <= past + i_q)
    imps = []
    for g in range(G):
        s = jnp.where(valid, _nt(q_ref[0, g], kc_ref[0, g]), NEG)
        m = jnp.max(s, axis=-1, keepdims=True)
        e = jnp.where(valid, jnp.exp(s - m), 0.0)
        p = e / jnp.maximum(jnp.sum(e, axis=-1, keepdims=True), 1e-30)
        o = _dot(p.astype(bf16), vc_ref[0, g]) * sig[g * rows:(g + 1) * rows, 0:1]
        for r in range(R):
            hh = g * R + r
            o_ref[0, :, hh * DH:(hh + 1) * DH] = o[r * n_q:(r + 1) * n_q, :]
        imps.append(p[0:n_q] + p[n_q:2 * n_q] + p[2 * n_q:3 * n_q])
    imp = jnp.concatenate(imps + [jnp.zeros((128 - G * n_q, n_cmp), f32)], axis=0)
    impt_ref[...] = imp.T
    v = _block_scores(impt_ref, n_sel)
    q_pos = past + lax.broadcasted_iota(jnp.int32, v.shape, 1) % n_q
    val, _ = _force_scores(v, q_pos)
    rank = _rank_desc(val) + jnp.where(FORCE > val, 1, 0)
    neg = jnp.where(rank < TOPK, 0.0, NEG).T
    nm_ref[0] = _dot(neg[0:G * n_q, :].astype(bf16), e_ref[...])


def _cmp_sel_sample(qg, kc, vc, gsw, e_full, past):
    db, n_q = qg.shape[0], qg.shape[2] // R
    n_cmp = kc.shape[2]
    return pl.pallas_call(
        functools.partial(_cmp_sel_sample_body, past=past, n_q=n_q),
        out_shape=(SDS((db, n_q, ATT), f32), SDS((db, G * n_q, past), f32)), grid=(db,),
        in_specs=[pl.BlockSpec((1, G, R * n_q, DH), lambda i: (i, 0, 0, 0)),
                  pl.BlockSpec((1, G, n_cmp, DH), lambda i: (i, 0, 0, 0)),
                  pl.BlockSpec((1, G, n_cmp, DH), lambda i: (i, 0, 0, 0)),
                  pl.BlockSpec((1, H * n_q, 128), lambda i: (i, 0, 0)),
                  _resident((n_cmp // 4, past))],
        out_specs=(pl.BlockSpec((1, n_q, ATT), lambda i: (i, 0, 0)),
                   pl.BlockSpec((1, G * n_q, past), lambda i: (i, 0, 0))),
        scratch_shapes=[pltpu.VMEM((n_cmp, 128), f32)],
        compiler_params=_cp(("parallel",), 32), name="cmp_sel_sample")(qg, kc, vc, gsw, e_full)


def _update_parts(q, parts, bias, m_ref, l_ref, acc_ref, idx):
    s = jnp.concatenate([_dot(q, kt.astype(bf16)) for kt, _ in parts], axis=1) + bias
    m_old = m_ref[idx]
    m_new = jnp.maximum(m_old, jnp.max(s, axis=-1, keepdims=True))
    alpha = jnp.exp(m_old - m_new)
    p = jnp.exp(s - m_new)
    l_ref[idx] = alpha * l_ref[idx] + jnp.sum(p, axis=-1, keepdims=True)
    p = p.astype(bf16)
    pv = acc_ref[idx] * alpha
    off = 0
    for _, vt in parts:
        pv = pv + _nt(p[:, off:off + vt.shape[1]], vt.astype(bf16))
        off += vt.shape[1]
    acc_ref[idx] = pv
    m_ref[idx] = m_new


def _flash_sample_body(*refs, n_q):
    refs = refs[1:]
    pages = refs[:CHUNK_PAGES]
    (q_ref, bias_ref, nm_ref, new_ref, bnew_ref, win_ref, wnew_ref, bwin_ref, gt_ref, o_ref, nwin_ref,
     m_ref, l_ref, acc_ref) = refs[CHUNK_PAGES:]
    step = pl.program_id(1)
    q = q_ref[0]

    @pl.when(step == 0)
    def _():
        m_ref[...] = jnp.full(m_ref.shape, NEG, f32)
        l_ref[...] = jnp.zeros(l_ref.shape, f32)
        acc_ref[...] = jnp.zeros(acc_ref.shape, f32)

    nm = nm_ref[0]
    nm = jnp.concatenate([nm[g * n_q:(g + 1) * n_q] for g in range(G) for _ in range(R)], axis=0)
    parts = [(pages[p][0, 0:G].reshape(G * DH, PAGE), pages[p][0, G:2 * G].reshape(G * DH, PAGE))
             for p in range(CHUNK_PAGES)]
    _update_parts(q, parts, bias_ref[...] + nm, m_ref, l_ref, acc_ref, 0)

    @pl.when(step == pl.num_programs(1) - 1)
    def _():
        new = new_ref[0]
        _update_parts(q, [(new[0:256], new[256:512])], bnew_ref[...], m_ref, l_ref, acc_ref, 0)
        old, wnew = win_ref[0, 0].reshape(8 * DH, WIN), wnew_ref[0]
        _update_parts(q, [(old[0:256], old[256:512]), (wnew[0:256], wnew[256:512])], bwin_ref[...],
                      m_ref, l_ref, acc_ref, 1)
        head, tail = _shifted_state(old, wnew, n_q)
        nwin_ref[0, :, :, 0:WIN - PAGE] = head.reshape(8, DH, WIN - PAGE)
        nwin_ref[0, :, :, WIN - PAGE:WIN] = tail.reshape(8, DH, PAGE)
        sig = jax.nn.sigmoid(gt_ref[0])
        res = (sig[:, 1:2] * acc_ref[0] / jnp.maximum(l_ref[0], 1e-30)
               + sig[:, 2:3] * acc_ref[1] / jnp.maximum(l_ref[1], 1e-30))
        for hh in range(H):
            g = hh // R
            o_ref[0, :, hh * DH:(hh + 1) * DH] = res[hh * n_q:(hh + 1) * n_q, g * DH:(g + 1) * DH]


def _flash_sample(pool, page_table, qbd, bias_slc, negmask, new_slc, bias_new, win_states, li, win_new, bias_win,
                  gsw, n_q):
    db, n_pages = page_table.shape
    steps = n_pages // CHUNK_PAGES
    rows = H * n_q
    keys = CHUNK_PAGES * PAGE
    cm = lambda f: (lambda i, j, pt: f(i, j))
    const = lambda shape: pl.BlockSpec(shape, lambda i, j, pt: (0,) * len(shape), pipeline_mode=pl.Buffered(1))
    per_b = lambda shape: pl.BlockSpec((1,) + shape, lambda i, j, pt: (i,) + (0,) * len(shape))
    page_specs = [pl.BlockSpec((1, 8, DH, PAGE), lambda i, j, pt, k=k: (pt[i, j * CHUNK_PAGES + k], 1, 0, 0))
                  for k in range(CHUNK_PAGES)]
    return pl.pallas_call(
        functools.partial(_flash_sample_body, n_q=n_q),
        out_shape=(SDS((db, n_q, ATT), f32), SDS((db, 8, DH, WIN), f32)),
        grid_spec=pltpu.PrefetchScalarGridSpec(
            num_scalar_prefetch=1, grid=(db, steps),
            in_specs=page_specs + [
                per_b((rows, 256)),
                pl.BlockSpec((rows, keys), cm(lambda i, j: (0, j))),
                pl.BlockSpec((1, G * n_q, keys), cm(lambda i, j: (i, 0, j))),
                per_b((8 * DH, PAGE)), const((rows, PAGE)),
                pl.BlockSpec((1, 1, 8, DH, WIN), lambda i, j, pt: (li, i, 0, 0, 0)),
                per_b((8 * DH, PAGE)), const((rows, WIN + PAGE)),
                per_b((rows, 128))],
            out_specs=(per_b((n_q, ATT)), per_b((8, DH, WIN))),
            scratch_shapes=[pltpu.VMEM((2, rows, 1), f32), pltpu.VMEM((2, rows, 1), f32),
                            pltpu.VMEM((2, rows, 256), f32)]),
        compiler_params=_cp(("arbitrary", "arbitrary"), 48), name="flash_sample")(
            page_table, *([pool] * CHUNK_PAGES), qbd, bias_slc, negmask, new_slc, bias_new,
            win_states, win_new, bias_win, gsw)


def _dil_tile(q, kc, vc, kp, vp, bias_ref, gi, prev_idx):
    lane = lax.broadcasted_iota(jnp.int32, (DT, 128), 1)
    low = lane < DH
    if kp is None:
        kk, vv = kc.astype(bf16), vc.astype(bf16)
    else:
        kk = jnp.concatenate([kp, kc], axis=0).astype(bf16)
        vv = jnp.concatenate([vp, vc], axis=0).astype(bf16)
    outs, lses = [], []
    for e in range(2):
        qe = jnp.where(low if e == 0 else jnp.logical_not(low), q, 0.0).astype(bf16)
        bias = bias_ref[gi, 0, e, 0]
        if kp is not None:
            bias = jnp.concatenate([bias_ref[gi, 0, e, prev_idx], bias], axis=1)
        s = _nt(qe, kk) + bias
        m = jnp.max(s, axis=-1, keepdims=True)
        ex = jnp.exp(s - m)
        l = jnp.maximum(jnp.sum(ex, axis=-1, keepdims=True), 1e-30)
        outs.append(_dot(ex.astype(bf16), vv) / l)
        lses.append(m + jnp.log(l))
    return jnp.where(low, outs[0], outs[1]), jnp.where(low, lses[0], lses[1])


def _dil_combine(os_, ls_):
    mx = jnp.maximum(jnp.maximum(ls_[0], ls_[1]), ls_[2])
    ws = [jnp.exp(l - mx) for l in ls_]
    den = ws[0] + ws[1] + ws[2]
    return [o * w / den for o, w in zip(os_, ws)]


def _dil_prompt_body(q0, q1, q2, k0, k1, k2, v0, v1, v2, bias_ref, y_ref, osc, lsc, *, s):
    qs, ks, vs = (q0, q1, q2), (k0, k1, k2), (v0, v1, v2)
    for gi, (_, dil) in enumerate(DIL):
        n_t = s // dil // DT
        q_r, k_r, v_r = qs[gi], ks[gi], vs[gi]

        def tile(start, prev_start, prev_idx, gi=gi, dil=dil, q_r=q_r, k_r=k_r, v_r=v_r, n_t=n_t):
            rows = lambda st: pl.ds(st, DT, stride=dil) if dil > 1 else pl.ds(pl.multiple_of(st, DT), DT)
            kp = k_r[0, rows(prev_start), :] if n_t > 1 else None
            vp = v_r[0, rows(prev_start), :] if n_t > 1 else None
            o, lse = _dil_tile(q_r[0, rows(start), :], k_r[0, rows(start), :], v_r[0, rows(start), :],
                               kp, vp, bias_ref, gi, prev_idx)
            osc[gi, rows(start), :] = o
            lsc[gi, rows(start), :] = lse

        def body(it, carry, dil=dil, n_t=n_t, tile=tile):
            rho, t = it // n_t, it % n_t
            start = rho + t * (DT * dil)
            prev = rho + jnp.maximum(t - 1, 0) * (DT * dil)
            tile(start, prev, jnp.where(t == 0, 2, 1))
            return carry

        lax.fori_loop(0, dil * n_t, body, 0, unroll=DIL_UNROLL)
    outs = _dil_combine([osc[0], osc[1], osc[2]], [lsc[0], lsc[1], lsc[2]])
    for pair in range(2):
        @pl.when(pl.program_id(1) == pair)
        def _(pair=pair):
            for gi, o in enumerate(outs):
                c = 2 * gi + pair
                y_ref[0, :, c * 128:(c + 1) * 128] = o


def _dil_prompt(zd, bias_tiles, b, s):
    col = lambda base: [pl.BlockSpec((1, s, 128), lambda i, j, c=base + 2 * gi: (i, 0, c + j)) for gi in range(3)]
    return pl.pallas_call(
        functools.partial(_dil_prompt_body, s=s),
        out_shape=SDS((b, s, ATT), f32), grid=(b, 2),
        in_specs=col(0) + col(6) + col(12) + [pl.BlockSpec((3, 1, 2, 3, DT, DT), lambda i, j: (0, j, 0, 0, 0, 0))],
        out_specs=pl.BlockSpec((1, s, ATT), lambda i, j: (i, 0, 0)),
        scratch_shapes=[pltpu.VMEM((3, s, 128), f32), pltpu.VMEM((3, s, 128), f32)],
        compiler_params=_cp(("parallel", "arbitrary"), 48), name="dil_prompt")(*([zd] * 9), bias_tiles)


def _shifted_state(old, new, n_q):
    w = old.shape[1]
    rolled = pltpu.roll(old, w - n_q, 1)
    lane = lax.broadcasted_iota(jnp.int32, new.shape, 1)
    tail = jnp.where(lane >= PAGE - n_q, pltpu.roll(new, PAGE - n_q, 1), rolled[:, w - PAGE:])
    return rolled[:, :w - PAGE], tail


def _dil_sample_body(q_ref, s0, s1, s2, new_ref, b0, b1, b2, y_ref, n0, n1, n2, *, n_q):
    lane = lax.broadcasted_iota(jnp.int32, (n_q, 256), 1)
    os_, ls_ = [], []
    for gi, (st, bias, nst) in enumerate(((s0, b0, n0), (s1, b1, n1), (s2, b2, n2))):
        w = st.shape[4]
        old = st[0, 0].reshape(8 * DH, w)
        new = new_ref[0, gi]
        head, tail = _shifted_state(old, new, n_q)
        if w > PAGE:
            nst[0, :, :, 0:w - PAGE] = head.reshape(8, DH, w - PAGE)
        nst[0, :, :, w - PAGE:w] = tail.reshape(8, DH, PAGE)
        q = q_ref[0, gi]
        s = jnp.concatenate([_dot(q, old[0:256].astype(bf16)), _dot(q, new[0:256].astype(bf16))], axis=1) + bias[...]
        m = jnp.max(s, axis=-1, keepdims=True)
        e = jnp.exp(s - m)
        l = jnp.maximum(jnp.sum(e, axis=-1, keepdims=True), 1e-30)
        eb = e.astype(bf16)
        res = (_nt(eb[:, 0:w], old[256:512].astype(bf16)) + _nt(eb[:, w:], new[256:512].astype(bf16))) / l
        lse = m + jnp.log(l)
        o = jnp.zeros((n_q, 256), f32)
        lm = jnp.zeros((n_q, 256), f32)
        for hg in range(4):
            inb = (lane >= hg * DH) & (lane < (hg + 1) * DH)
            o = jnp.where(inb, res[hg * n_q:(hg + 1) * n_q, :], o)
            lm = jnp.where(inb, lse[hg * n_q:(hg + 1) * n_q, :], lm)
        os_.append(o)
        ls_.append(lm)
    for gi, o in enumerate(_dil_combine(os_, ls_)):
        y_ref[0, :, gi * 256:(gi + 1) * 256] = o


def _dil_sample(qbd, states, li, new_t, biases, n_q):
    db = new_t.shape[0]
    per_b = lambda shape: pl.BlockSpec((1,) + shape, lambda i: (i,) + (0,) * len(shape))
    wins = [st.shape[4] for st in states]
    return pl.pallas_call(
        functools.partial(_dil_sample_body, n_q=n_q),
        out_shape=(SDS((db, n_q, ATT), f32),) + tuple(SDS((db, 8, DH, w), f32) for w in wins), grid=(db,),
        in_specs=[per_b((3, 4 * n_q, 256))]
                 + [pl.BlockSpec((1, 1, 8, DH, w), lambda i: (li, i, 0, 0, 0)) for w in wins]
                 + [per_b((3, 8 * DH, PAGE))] + [_resident(bb.shape) for bb in biases],
        out_specs=(per_b((n_q, ATT)),) + tuple(per_b((8, DH, w)) for w in wins),
        compiler_params=_cp(("parallel",), 48), name="dil_sample")(qbd, *states, new_t, *biases)


def _rel_bucket(dist):
    dist = jnp.maximum(dist, 0)
    d32 = jnp.maximum(dist, 1).astype(f32)
    large = 16 + (jnp.log(d32 / 16) / math.log(2048 / 16) * 16).astype(jnp.int32)
    return jnp.where(dist < 16, dist, jnp.minimum(large, 31))


def _bias_rev(tbl, n, keep, scale=1):
    dist = (n - 1 - jnp.arange(n)) * scale
    return jnp.where(keep(dist), tbl[_rel_bucket(dist)].T, NEG)


def _toeplitz_tiles(bdr, n_delta, t):
    hn = bdr.shape[0]
    bdp = jnp.concatenate([bdr, jnp.full((hn, t - 1), NEG, f32)], axis=1)
    lp = bdp.shape[1]
    gq = jnp.stack([bdp[:, lp + 1 - d * t - 2 * t:lp - d * t] for d in range(n_delta)])
    gq = jnp.pad(gq, ((0, 0), (0, 0), (0, 1)))
    x = jnp.broadcast_to(gq[:, :, None, :], (n_delta, hn, t, 2 * t)).reshape(n_delta, hn, t * 2 * t)
    y = x[..., :t * (2 * t - 1)].reshape(n_delta, hn, t, 2 * t - 1)
    return y[..., t - 1:]


def _rows_by_query(bdr, n_q, n_keys, first_dist):
    last = bdr.shape[1] - 1
    return jnp.stack([bdr[:, last - first_dist - i:last - first_dist - i + n_keys] for i in range(n_q)], axis=1)


def _new_token_bias(tbl, n_q, keep):
    tri = jnp.arange(n_q)[:, None] - jnp.arange(PAGE)[None, :]
    ok = (tri >= 0) & keep(tri)
    return jnp.where(ok[None], jnp.transpose(tbl[_rel_bucket(tri)], (2, 0, 1)), NEG)


def _nsa_tables(rel_bias, s, n_q, past):
    everywhere = lambda d: d >= 0
    in_window = lambda d: d < WIN
    n = past + n_q
    key_blk = jnp.arange(s // TQ)[:, None, None] * (TQ // SEL_BLOCK) + jnp.arange(TQ)[None, :, None] // SEL_BLOCK
    bias_new = _new_token_bias(rel_bias, n_q, everywhere).reshape(H * n_q, PAGE)
    bias_win = _rows_by_query(_bias_rev(rel_bias, n, in_window), n_q, WIN, WIN).reshape(H * n_q, WIN)
    return dict(
        perm=_chunk_perm(),
        bias_s=_toeplitz_tiles(_bias_rev(rel_bias, s, everywhere), s // TQ, TQ) * LOG2E,
        bias_w=_toeplitz_tiles(_bias_rev(rel_bias, WIN + TQ, in_window), WIN // TQ + 1, TQ) * LOG2E,
        et_tiles=(key_blk == jnp.arange(128)[None, None, :]).astype(bf16),
        e_full=(jnp.arange(past // SEL_BLOCK)[:, None] == jnp.arange(past)[None, :] // SEL_BLOCK).astype(bf16),
        bias_slc=_rows_by_query(_bias_rev(rel_bias, n, everywhere), n_q, past, past).reshape(H * n_q, past),
        bias_new=bias_new,
        bias_win=jnp.concatenate([bias_win, bias_new], axis=1))


def _nsa_layer(yp, ys, g_pre, g_post, w_in, w_out, cmp_pos, cmp_w1, cmp_w2, tb, pool, page_table, win_states,
               li, b, s, db, n_q, past):
    w_pad = jnp.pad(w_in, ((0, 0), (0, NSA_PAD - w_in.shape[1]))).astype(bf16)
    w_out = w_out.astype(bf16)
    w1cat = jnp.concatenate([cmp_w1[:, :1024], cmp_w1[:, 1024:]], axis=-1).astype(bf16)
    pos2 = cmp_pos.reshape(2, 2, 1024)
    w2 = cmp_w2.astype(bf16)

    q_hm, q2_hm, kvt, kwin, ks, vs, kw, vw, gates = _nsa_proj_prompt(yp, g_pre, w_pad, b, s)
    kc, vc = _compress_prompt(kvt, tb["perm"], w1cat, pos2, w2, b, s)
    o_cmp, selneg = _cmp_sel_prompt(q_hm, kc, vc, gates, b, s)
    o_sw = _flash_prompt(q2_hm, ks, vs, kw, vw, tb["bias_s"], tb["bias_w"], selneg, tb["et_tiles"], gates, b, s)
    yp = _out_proj(yp, g_post, w_out, o_cmp.reshape(b * s, ATT), o_sw.reshape(b * s, ATT))
    kv_p = jnp.transpose(kvt.reshape(b, 4, G, DH, s), (0, 4, 1, 2, 3))
    win_p = kwin.reshape(b, s, 2, G, DH)[:, s - min(WIN, s):]

    z = _proj(ys, g_pre, w_pad)
    q = (z[:, :ATT] * (DH ** -0.5)).reshape(db, n_q, H, DH).transpose(0, 2, 1, 3)
    qg = q.reshape(db, G, R * n_q, DH).astype(bf16)
    eye = jnp.eye(G, dtype=f32)
    qbd = (q.reshape(db, G, R * n_q, 1, DH) * eye[None, :, None, :, None]).reshape(db, H * n_q, G * DH).astype(bf16)
    kv_new = z[:, ATT:ATT + 1024].reshape(db, n_q, 1024)
    win_new = z[:, ATT + 1024:ATT + 1536].reshape(db, n_q, 512)
    gt = z[:, ATT + 1536:ATT + 1536 + 3 * H].reshape(db, n_q, 3, H).transpose(0, 3, 1, 2)
    gsw = jnp.pad(gt.reshape(db, H * n_q, 3), ((0, 0), (0, 0), (0, 125)))
    kc_s, vc_s = _compress_sample(pool, page_table, tb["perm"], w1cat, pos2, w2)
    o_cmp_s, negmask = _cmp_sel_sample(qg, kc_s, vc_s, gsw, tb["e_full"], past)
    page_t = lambda x: jnp.pad(jnp.transpose(x, (0, 2, 1)), ((0, 0), (0, 0), (0, PAGE - n_q)))
    win_view = jnp.transpose(win_states, (0, 1, 3, 4, 5, 2)).reshape(win_states.shape[0], db, 8, DH, WIN)
    o_sw_s, win_t = _flash_sample(pool, page_table, qbd, tb["bias_slc"], negmask, page_t(kv_new[:, :, 512:]),
                                  tb["bias_new"], win_view, li, page_t(win_new), tb["bias_win"], gsw, n_q)
    ys = _out_proj(ys, g_post, w_out, o_cmp_s.reshape(db * n_q, ATT), o_sw_s.reshape(db * n_q, ATT))
    kv_s = kv_new.reshape(db, n_q, 4, G, DH)
    win_s = jnp.transpose(win_t.reshape(db, 2, G, DH, WIN), (0, 4, 1, 2, 3))
    return yp, ys, kv_p, win_p, kv_s, win_s


def _dil_tables(rel_bias, n_q):
    tiles, biases = [], []
    for gi, (win, dil) in enumerate(DIL):
        tbl = rel_bias[:, gi * 4:(gi + 1) * 4]
        on_grid = lambda d, win=win, dil=dil: (d % dil == 0) & (d <= win)
        tl = _toeplitz_tiles(_bias_rev(tbl, 2 * DT, on_grid, scale=dil), 2, DT)
        tiles.append(jnp.concatenate([tl, jnp.full_like(tl[:1], NEG)]))
        b_all = jnp.concatenate([_rows_by_query(_bias_rev(tbl, win + n_q, on_grid), n_q, win, win),
                                 _new_token_bias(tbl, n_q, on_grid)], axis=2)
        biases.append(b_all.reshape(4 * n_q, win + PAGE))
    return dict(tiles=jnp.stack(tiles).transpose(0, 2, 1, 3, 4).reshape(3, 2, 2, 3, DT, DT), sample=biases)


def _dil_layer(yp, ys, g_pre, g_post, w_in, w_out, tb, states, li, b, s, db, n_q):
    w_in = w_in.astype(bf16)
    w_out = w_out.astype(bf16)

    zd, kvt = _dil_proj(yp, g_pre, w_in, b, s)
    y = _dil_prompt(zd.reshape(b, s, 3 * ATT), tb["tiles"], b, s)
    yp = _out_proj(yp, g_post, w_out, y.reshape(b * s, ATT))
    kvt = kvt.reshape(b, 2, H, DH, s)
    st_p = []
    for gi, (win, _) in enumerate(DIL):
        keep = min(win, s)
        st_p.append(jnp.transpose(kvt[:, :, gi * 4:(gi + 1) * 4, :, s - keep:], (0, 4, 1, 2, 3)))

    z = _proj(ys, g_pre, w_in).reshape(db, n_q, 3, 3, 4, DH)
    q = (z[:, :, 0] * (DH ** -0.5)).transpose(0, 2, 3, 1, 4)
    eye = jnp.eye(4, dtype=f32)
    qbd = (q[:, :, :, :, None, :] * eye[None, None, :, None, :, None]).reshape(db, 3, 4 * n_q, 256).astype(bf16)
    new_t = jnp.transpose(z[:, :, 1:3], (0, 3, 2, 4, 5, 1)).reshape(db, 3, 8 * DH, n_q)
    new_t = jnp.pad(new_t, ((0, 0), (0, 0), (0, 0), (0, PAGE - n_q)))
    views = [jnp.transpose(st, (0, 1, 3, 4, 5, 2)).reshape(st.shape[0], db, 8, DH, st.shape[2]) for st in states]
    y_s, n0, n1, n2 = _dil_sample(qbd, views, li, new_t, tb["sample"], n_q)
    ys = _out_proj(ys, g_post, w_out, y_s.reshape(db * n_q, ATT))
    st_s = [jnp.transpose(n.reshape(db, 2, 4, DH, n.shape[3]), (0, 4, 1, 2, 3)) for n in (n0, n1, n2)]
    return yp, ys, st_p, st_s


def kernel(x_prompt, x_sample, cache_nsa_kv, page_table, state_nsa_win, state_dil_w128, state_dil_w512,
           state_dil_w2048, norm_g, ffn_w_in, ffn_w_out, nsa_w_in, nsa_w_out, nsa_cmp_pos, nsa_cmp_w1,
           nsa_cmp_w2, dil_w_in, dil_w_out, rel_bias):
    b, s, _ = x_prompt.shape
    db, n_q, _ = x_sample.shape
    depth = norm_g.shape[0]
    past = page_table.shape[1] * PAGE
    yp = x_prompt.reshape(b * s, D)
    ys = x_sample.reshape(db * n_q, D)
    nsa_kv_p, nsa_win_p, nsa_kv_s, nsa_win_s = [], [], [], []
    dil_p, dil_s = [[], [], []], [[], [], []]
    nsa_tb = _nsa_tables(rel_bias, s, n_q, past)
    dil_tb = _dil_tables(rel_bias, n_q)
    for i in range(depth):
        li = i // 2
        g = norm_g[i]
        w_i, w_o = ffn_w_in[i, 0].astype(bf16), ffn_w_out[i, 0].astype(bf16)
        yp = _half_ffn(yp, g[0:2], w_i, w_o)
        ys = _half_ffn(ys, g[0:2], w_i, w_o)
        if i % 2 == 0:
            n_pool = cache_nsa_kv.shape[1]
            pool = jnp.transpose(cache_nsa_kv, (0, 1, 3, 4, 5, 2)).reshape(
                cache_nsa_kv.shape[0] * n_pool, 4 * G, DH, PAGE)
            yp, ys, kvp, winp, kvs, wins = _nsa_layer(
                yp, ys, g[2:3], g[3:4], nsa_w_in[li], nsa_w_out[li], nsa_cmp_pos[li], nsa_cmp_w1[li],
                nsa_cmp_w2[li], nsa_tb, pool, page_table + li * n_pool, state_nsa_win, li, b, s, db, n_q, past)
            nsa_kv_p.append(kvp)
            nsa_win_p.append(winp)
            nsa_kv_s.append(kvs)
            nsa_win_s.append(wins)
        else:
            states = (state_dil_w128, state_dil_w512, state_dil_w2048)
            yp, ys, stp, sts = _dil_layer(yp, ys, g[2:3], g[3:4], dil_w_in[li], dil_w_out[li], dil_tb, states,
                                          li, b, s, db, n_q)
            for gi in range(3):
                dil_p[gi].append(stp[gi])
                dil_s[gi].append(sts[gi])
        w_i, w_o = ffn_w_in[i, 1].astype(bf16), ffn_w_out[i, 1].astype(bf16)
        yp = _half_ffn(yp, g[4:6], w_i, w_o)
        ys = _half_ffn(ys, g[4:6], w_i, w_o)
    return (yp.reshape(b, s, D), ys.reshape(db, n_q, D),
            jnp.stack(nsa_kv_p), jnp.stack(nsa_win_p),
            jnp.stack(dil_p[0]), jnp.stack(dil_p[1]), jnp.stack(dil_p[2]),
            jnp.stack(nsa_kv_s), jnp.stack(nsa_win_s),
            jnp.stack(dil_s[0]), jnp.stack(dil_s[1]), jnp.stack(dil_s[2]))
```

```python
import functools
import math

import jax
import jax.numpy as jnp
from jax import lax
from jax.experimental import pallas as pl
from jax.experimental.pallas import tpu as pltpu

f32, bf16 = jnp.float32, jnp.bfloat16
SDS = jax.ShapeDtypeStruct

D = 1024
FF = 2816
FF_CHUNK = 256
H, DH = 12, 64
G, R = 4, 3
ATT = H * DH
EPS = 1e-6
NEG = -1e30
LOG2E = math.log2(math.e)
FORCE = 1e4
SEL_BLOCK, TOPK = 64, 16
WIN = 512
NSA_PAD = 2432
TQ = 256
RC = 64
PAGE = 128
CHUNK_PAGES = 32
CMP_GROUP = 1
DIL = ((128, 1), (512, 4), (2048, 16))
DT = 128
DIL_UNROLL = 4
MIB = 1024 * 1024


def _cp(sem, vmem_mib):
    return pltpu.CompilerParams(dimension_semantics=sem, vmem_limit_bytes=vmem_mib * MIB)


def _rms(x, g):
    return x * lax.rsqrt(jnp.mean(x * x, axis=-1, keepdims=True) + EPS) * g


def _nt(a, b):
    return lax.dot_general(a, b, (((1,), (1,)), ((), ())), preferred_element_type=f32)


def _dot(a, b):
    return jnp.dot(a, b, preferred_element_type=f32)


def _resident(shape):
    nd = len(shape)
    return pl.BlockSpec(shape, lambda *_: (0,) * nd, pipeline_mode=pl.Buffered(1))


def _ffn_core(x, g_ref, win_ref, wout_ref):
    h = _rms(x, g_ref[0:1, :]).astype(bf16)
    acc = jnp.zeros(x.shape, f32)
    for c in range(FF // FF_CHUNK):
        lo, hi = c * FF_CHUNK, (c + 1) * FF_CHUNK
        gate = _dot(h, win_ref[:, lo:hi])
        up = _dot(h, win_ref[:, FF + lo:FF + hi])
        act = (gate * jax.nn.sigmoid(gate) * up).astype(bf16)
        acc = acc + _dot(act, wout_ref[lo:hi, :])
    return x + 0.5 * _rms(acc, g_ref[1:2, :])


def _ffn_body(x_ref, g_ref, win_ref, wout_ref, o_ref):
    o_ref[...] = _ffn_core(x_ref[...], g_ref, win_ref, wout_ref)


def _mix_ffn_body(x_ref, gm_ref, wo_ref, g_ref, win_ref, wout_ref, *refs):
    o = refs[0][...]
    for r in refs[1:-1]:
        o = o + r[...]
    x = x_ref[...] + _rms(_dot(o.astype(bf16), wo_ref[...]), gm_ref[...])
    refs[-1][...] = _ffn_core(x, g_ref, win_ref, wout_ref)


def _mix_ffn(x, g_post, w_o, ffn, *branches):
    g2, w_in, w_out = ffn
    m = x.shape[0]
    tm = min(m, 512)
    row = lambda n: pl.BlockSpec((tm, n), lambda i: (i, 0))
    return pl.pallas_call(
        _mix_ffn_body, out_shape=SDS((m, D), f32), grid=(m // tm,),
        in_specs=[row(D), _resident((1, D)), _resident((ATT, D)), _resident((2, D)), _resident((D, 2 * FF)),
                  _resident((FF, D))] + [row(ATT)] * len(branches),
        out_specs=row(D), compiler_params=_cp(("parallel",), 56), name="mix_ffn")(
            x, g_post, w_o, g2, w_in, w_out, *branches)


def _half_ffn(x, g2, w_in, w_out):
    m = x.shape[0]
    tm = min(m, 512)
    return pl.pallas_call(
        _ffn_body, out_shape=SDS((m, D), f32), grid=(m // tm,),
        in_specs=[pl.BlockSpec((tm, D), lambda i: (i, 0)), _resident((2, D)),
                  _resident((D, 2 * FF)), _resident((FF, D))],
        out_specs=pl.BlockSpec((tm, D), lambda i: (i, 0)),
        compiler_params=_cp(("parallel",), 48), name="half_ffn")(x, g2, w_in, w_out)


def _proj_body(x_ref, g_ref, w_ref, o_ref):
    h = _rms(x_ref[...], g_ref[...]).astype(bf16)
    o_ref[...] = _dot(h, w_ref[...])


def _proj(x, g, w):
    m, n = x.shape[0], w.shape[1]
    return pl.pallas_call(
        _proj_body, out_shape=SDS((m, n), f32), grid=(1,),
        in_specs=[_resident((m, D)), _resident((1, D)), _resident((D, n))],
        out_specs=pl.BlockSpec((m, n), lambda i: (0, 0)),
        compiler_params=_cp(("arbitrary",), 32), name="proj_sample")(x, g, w)


def _nsa_proj_body(x_ref, g_ref, w_ref, q_ref, q2_ref, kv_ref, kwin_ref, ks_ref, vs_ref, kw_ref, vw_ref, gt_ref):
    h = _rms(x_ref[...], g_ref[...]).astype(bf16)
    zq = _dot(h, w_ref[:, 0:ATT]) * (DH ** -0.5)
    zq2 = zq * LOG2E
    for hh in range(H):
        q_ref[0, hh] = zq[:, hh * DH:(hh + 1) * DH].astype(bf16)
        q2_ref[0, hh] = zq2[:, hh * DH:(hh + 1) * DH].astype(bf16)
    zc = _dot(h, w_ref[:, ATT:ATT + 512])
    kv_ref[0, 0:2 * G] = zc.T.reshape(2 * G, DH, zc.shape[0])
    zs = _dot(h, w_ref[:, ATT + 512:ATT + 1024])
    kv_ref[0, 2 * G:4 * G] = zs.T.reshape(2 * G, DH, zs.shape[0])
    zw = _dot(h, w_ref[:, ATT + 1024:ATT + 1536])
    kwin_ref[...] = zw
    ones = jnp.ones((zs.shape[0], DH), bf16)
    for g in range(G):
        ks_ref[0, g] = zs[:, g * DH:(g + 1) * DH].astype(bf16)
        vs_ref[0, g] = jnp.concatenate([zs[:, 256 + g * DH:256 + (g + 1) * DH].astype(bf16), ones], axis=1)
        kw_ref[0, g] = zw[:, g * DH:(g + 1) * DH].astype(bf16)
        vw_ref[0, g] = jnp.concatenate([zw[:, 256 + g * DH:256 + (g + 1) * DH].astype(bf16), ones], axis=1)
    gt_ref[...] = _dot(h, w_ref[:, ATT + 1536:NSA_PAD])


def _nsa_proj_prompt(x, g, w, b, s):
    tm = 512
    nt = s // tm
    hm = lambda n, w_: pl.BlockSpec((1, n, tm, w_), lambda i: (i // nt, 0, i % nt, 0))
    rows = lambda n: pl.BlockSpec((tm, n), lambda i: (i, 0))
    return pl.pallas_call(
        _nsa_proj_body,
        out_shape=(SDS((b, H, s, DH), bf16), SDS((b, H, s, DH), bf16), SDS((b, 4 * G, DH, s), f32),
                   SDS((b * s, 512), f32), SDS((b, G, s, DH), bf16), SDS((b, G, s, 2 * DH), bf16),
                   SDS((b, G, s, DH), bf16), SDS((b, G, s, 2 * DH), bf16), SDS((b * s, 128), f32)),
        grid=(b * s // tm,),
        in_specs=[rows(D), _resident((1, D)), _resident((D, NSA_PAD))],
        out_specs=(hm(H, DH), hm(H, DH), pl.BlockSpec((1, 4 * G, DH, tm), lambda i: (i // nt, 0, 0, i % nt)),
                   rows(512), hm(G, DH), hm(G, 2 * DH), hm(G, DH), hm(G, 2 * DH), rows(128)),
        compiler_params=_cp(("parallel",), 48), name="nsa_proj_prompt")(x, g, w)


def _dil_proj_body(x_ref, g_ref, w_ref, o_ref, kvt_ref):
    h = _rms(x_ref[...], g_ref[...]).astype(bf16)
    o_ref[:, 0:ATT] = _dot(h, w_ref[:, 0:ATT]) * (DH ** -0.5)
    for part in (1, 2):
        z = _dot(h, w_ref[:, part * ATT:(part + 1) * ATT])
        o_ref[:, part * ATT:(part + 1) * ATT] = z
        kvt_ref[0, (part - 1) * H:part * H] = z.T.reshape(H, DH, z.shape[0])


def _dil_proj(x, g, w, b, s):
    tm = 512
    nt = s // tm
    return pl.pallas_call(
        _dil_proj_body, out_shape=(SDS((b * s, 3 * ATT), f32), SDS((b, 2 * H, DH, s), f32)), grid=(b * s // tm,),
        in_specs=[pl.BlockSpec((tm, D), lambda i: (i, 0)), _resident((1, D)), _resident((D, 3 * ATT))],
        out_specs=(pl.BlockSpec((tm, 3 * ATT), lambda i: (i, 0)),
                   pl.BlockSpec((1, 2 * H, DH, tm), lambda i: (i // nt, 0, 0, i % nt))),
        compiler_params=_cp(("parallel",), 48), name="dil_proj")(x, g, w)


def _compress_body(*refs, n_src, n_pages, paged, group=1):
    if paged:
        refs = refs[1:]
    srcs = refs[:n_src]
    perm_ref, w1_ref, pos_ref, w2_ref, kc_ref, vc_ref, lhs_ref, aprev_ref = refs[n_src:]
    nch = n_pages * PAGE // 16
    gn = group * nch
    step = pl.program_id(1) if paged else 0
    sub = step % group if paged else 0

    for p in range(n_pages):
        mats = srcs[p][0] if paged else srcs[0][0, :, :, p * PAGE:(p + 1) * PAGE]
        both = jnp.concatenate([jnp.concatenate([mats[2 * c], mats[2 * c + 1]], axis=1) for c in range(4)], axis=0)
        out = _dot(both.astype(bf16), perm_ref[...])
        for c in range(4):
            kind, pair = c // 2, c % 2
            o = out[c * DH:(c + 1) * DH]
            zt = jnp.concatenate([o[:, 0:128], o[:, 128:256]], axis=0).T
            for gl in range(2):
                start = (2 * pair + gl) * gn + sub * nch + p * 8
                dst = pl.ds(pl.multiple_of(start, 8), 8) if paged else pl.ds(start, 8)
                for r2 in range(8):
                    lhs_ref[kind, dst, r2 * 128:(r2 + 1) * 128] = zt[(gl * 8 + r2) * 8:(gl * 8 + r2 + 1) * 8, :]

    def mlp():
        first = (step < group) if paged else True
        if paged:
            @pl.when(step == group - 1)
            def _():
                aprev_ref[...] = jnp.zeros(aprev_ref.shape, f32)

        row = lax.broadcasted_iota(jnp.int32, (gn, 256), 0)
        row_o = lax.broadcasted_iota(jnp.int32, (gn, DH), 0)
        for kind, out_ref in ((0, kc_ref), (1, vc_ref)):
            pp = _dot(lhs_ref[kind].astype(bf16), w1_ref[kind])
            pos = _dot(pos_ref[kind].astype(bf16), w1_ref[kind])
            posterm = pos[0:1, 0:256] + pos[1:2, 256:512]
            for g in range(G):
                a = pp[g * gn:(g + 1) * gn, 0:256]
                bm = pp[g * gn:(g + 1) * gn, 256:512]
                shifted = pltpu.roll(a, 1, 0)
                if paged:
                    carry = aprev_ref[kind, (g + 1) * gn - 1:(g + 1) * gn, :]
                    shifted = jnp.where(row == 0, carry, shifted)
                    aprev_ref[kind, g * gn:(g + 1) * gn, :] = a
                pre = shifted + bm + posterm
                hid = (pre * (0.5 + 0.5 * jnp.tanh(0.5 * pre))).astype(bf16)
                out = _dot(hid, w2_ref[kind])
                out = jnp.where((row_o == 0) & first, 0.0, out)
                out_ref[0, g] = out.astype(bf16)

    if paged:
        pl.when(sub == group - 1)(mlp)
    else:
        mlp()


def _chunk_perm():
    t = jnp.arange(PAGE)
    col = (t % 2) * 128 + ((t % 16) // 2) * 8 + t // 16
    src = jnp.concatenate([col, col + DH])
    return (src[:, None] == jnp.arange(256)[None, :]).astype(bf16)


def _compress_prompt(kvt, perm, w1cat, pos2, w2, b, s):
    nch = s // 16
    out = SDS((b, G, nch, DH), bf16)
    ospec = pl.BlockSpec((1, G, nch, DH), lambda i: (i, 0, 0, 0))
    return pl.pallas_call(
        functools.partial(_compress_body, n_src=1, n_pages=s // PAGE, paged=False),
        out_shape=(out, out), grid=(b,),
        in_specs=[pl.BlockSpec((1, 8, DH, s), lambda i: (i, 0, 0, 0)), _resident((256, 256)),
                  _resident((2, 1024, 512)), _resident((2, 2, 1024)), _resident((2, 256, DH))],
        out_specs=(ospec, ospec),
        scratch_shapes=[pltpu.VMEM((2, G * nch, 1024), f32), pltpu.VMEM((2, G * nch, 256), f32)],
        compiler_params=_cp(("parallel",), 48), name="compress_prompt")(kvt, perm, w1cat, pos2, w2)


def _compress_sample(pool_t, page_table, perm, w1cat, pos2, w2):
    db, n_pages = page_table.shape
    steps = n_pages // CHUNK_PAGES
    gn = CMP_GROUP * CHUNK_PAGES * PAGE // 16
    out = SDS((db, G, n_pages * PAGE // 16, DH), bf16)
    ospec = pl.BlockSpec((1, G, gn, DH), lambda i, j, pt: (i, 0, j // CMP_GROUP, 0))
    const = lambda shape: pl.BlockSpec(shape, lambda i, j, pt: (0,) * len(shape), pipeline_mode=pl.Buffered(1))
    page_specs = [pl.BlockSpec((1, 8, DH, PAGE), lambda i, j, pt, k=k: (pt[i, j * CHUNK_PAGES + k], 0, 0, 0))
                  for k in range(CHUNK_PAGES)]
    return pl.pallas_call(
        functools.partial(_compress_body, n_src=CHUNK_PAGES, n_pages=CHUNK_PAGES, paged=True, group=CMP_GROUP),
        out_shape=(out, out),
        grid_spec=pltpu.PrefetchScalarGridSpec(
            num_scalar_prefetch=1, grid=(db, steps),
            in_specs=page_specs + [const((256, 256)), const((2, 1024, 512)), const((2, 2, 1024)),
                                   const((2, 256, DH))],
            out_specs=(ospec, ospec),
            scratch_shapes=[pltpu.VMEM((2, G * gn, 1024), f32), pltpu.VMEM((2, G * gn, 256), f32)]),
        compiler_params=_cp(("arbitrary", "arbitrary"), 48), name="compress_sample")(
            page_table, *([pool_t] * CHUNK_PAGES), perm, w1cat, pos2, w2)


def _block_scores(impt_ref, n_blocks):
    return (impt_ref[pl.ds(1, n_blocks, stride=4), :] + impt_ref[pl.ds(2, n_blocks, stride=4), :]
            + impt_ref[pl.ds(3, n_blocks, stride=4), :])


def _rank_desc(val):
    n = val.shape[0]
    tiles = [val[8 * t:8 * t + 8] for t in range(n // 8)]
    ranks = [jnp.zeros((8, val.shape[1]), jnp.int32) for _ in tiles]
    row = lax.broadcasted_iota(jnp.int32, (8, val.shape[1]), 0)
    for k in range(n):
        vk = val[k:k + 1, :]
        for t, blk in enumerate(tiles):
            if 8 * t > k:
                beats = vk >= blk
            elif 8 * t + 7 < k:
                beats = vk > blk
            else:
                beats = (vk > blk) | ((vk == blk) & (row + 8 * t > k))
            ranks[t] = ranks[t] + jnp.where(beats, 1, 0)
    return jnp.concatenate(ranks, axis=0)


def _force_scores(v, q_pos):
    j = lax.broadcasted_iota(jnp.int32, v.shape, 0)
    cur = q_pos >> 6
    forced = (j == 0) | (j == cur) | (j == cur - 1)
    return jnp.where(forced, FORCE, jnp.where(j > cur, -FORCE, v)), j


def _cmp_sel_prompt_body(q_ref, kc_ref, vc_ref, gt_ref, o_ref, sel_ref, impt_ref, *, n_sel):
    qt = pl.program_id(1)
    sig = jax.nn.sigmoid(gt_ref[...])
    t = qt * TQ + lax.broadcasted_iota(jnp.int32, (TQ, 128), 0)
    mp = lax.broadcasted_iota(jnp.int32, (TQ, 128), 1)
    valid = (mp >= 1) & (16 * mp + 15 <= t)
    for g in range(G):
        kc, vc = kc_ref[0, g], vc_ref[0, g]
        imp = jnp.zeros((TQ, 128), f32)
        for r in range(R):
            hh = g * R + r
            s = jnp.where(valid, _nt(q_ref[0, hh], kc), NEG)
            m = jnp.max(s, axis=-1, keepdims=True)
            e = jnp.where(valid, jnp.exp(s - m), 0.0)
            p = e / jnp.maximum(jnp.sum(e, axis=-1, keepdims=True), 1e-30)
            imp = imp + p
            o_ref[0, :, hh * DH:(hh + 1) * DH] = _dot(p.astype(bf16), vc) * sig[:, hh:hh + 1]
        for half in range(TQ // 128):
            impt_ref[...] = imp[half * 128:(half + 1) * 128, :].T
            v = _block_scores(impt_ref, n_sel)
            q_pos = qt * TQ + half * 128 + lax.broadcasted_iota(jnp.int32, v.shape, 1)
            val, _ = _force_scores(v, q_pos)
            neg = jnp.where(_rank_desc(val) < TOPK, 0.0, NEG)
            full = jnp.concatenate([neg, jnp.zeros((128 - n_sel, 128), f32)], axis=0)
            sel_ref[0, g, half * 128:(half + 1) * 128, :] = full.T.astype(bf16)


def _cmp_sel_prompt(q_hm, kc, vc, gates, b, s):
    nq = s // TQ
    return pl.pallas_call(
        functools.partial(_cmp_sel_prompt_body, n_sel=s // SEL_BLOCK),
        out_shape=(SDS((b, s, ATT), f32), SDS((b, G, s, 128), bf16)), grid=(b, nq),
        in_specs=[pl.BlockSpec((1, H, TQ, DH), lambda i, j: (i, 0, j, 0)),
                  pl.BlockSpec((1, G, 128, DH), lambda i, j: (i, 0, 0, 0)),
                  pl.BlockSpec((1, G, 128, DH), lambda i, j: (i, 0, 0, 0)),
                  pl.BlockSpec((TQ, 128), lambda i, j: (i * nq + j, 0))],
        out_specs=(pl.BlockSpec((1, TQ, ATT), lambda i, j: (i, j, 0)),
                   pl.BlockSpec((1, G, TQ, 128), lambda i, j: (i, 0, j, 0))),
        scratch_shapes=[pltpu.VMEM((128, 128), f32)],
        compiler_params=_cp(("parallel", "parallel"), 32), name="cmp_sel_prompt")(q_hm, kc, vc, gates)


def _group_update(qa, ka, va, bias_ref, delta, h0, m_ref, acc_ref, idx0, s_ref, p_ref, a_ref):
    s_ref[...] = _nt(qa, ka)
    for r in range(R):
        for c in range(TQ // RC):
            lo = c * RC
            s = s_ref[r * TQ + lo:r * TQ + lo + RC, :] + bias_ref[delta, h0 + r, lo:lo + RC, :]
            m_old = m_ref[idx0 + r, lo:lo + RC, :]
            m_new = jnp.maximum(m_old, jnp.max(s, axis=-1, keepdims=True))
            p_ref[r * TQ + lo:r * TQ + lo + RC, :] = jnp.exp2(
                s - jnp.concatenate([m_new] * (TQ // 128), axis=1)).astype(bf16)
            m_ref[idx0 + r, lo:lo + RC, :] = m_new
            a_ref[r * TQ + lo:r * TQ + lo + RC, :] = jnp.exp2(m_old - m_new)
    pv = _dot(p_ref[...], va)
    for r in range(R):
        acc_ref[idx0 + r] = a_ref[r * TQ:(r + 1) * TQ, :] * acc_ref[idx0 + r] + pv[r * TQ:(r + 1) * TQ, :]


def _flash_prompt_body(qt_ref, kt_ref, q_ref, ks_ref, vs_ref, kw_ref, vw_ref, bs_ref, bw_ref, sel_ref, et_ref, gt_ref,
                       o_ref, m_ref, acc_ref, s_ref, p_ref, a_ref):
    qt, kt = qt_ref[pl.program_id(1)], kt_ref[pl.program_id(1)]
    delta = qt - kt
    scratch = (s_ref, p_ref, a_ref)

    @pl.when(kt == 0)
    def _():
        m_ref[...] = jnp.full(m_ref.shape, NEG, f32)
        acc_ref[...] = jnp.zeros(acc_ref.shape, f32)

    for g in range(G):
        q3 = q_ref[0, g * R:(g + 1) * R].reshape(R * TQ, DH)
        qa = jnp.concatenate([jnp.concatenate([sel_ref[0, g]] * R, axis=0), q3], axis=1)
        ka = jnp.concatenate([et_ref[0], ks_ref[0, g]], axis=1)
        _group_update(qa, ka, vs_ref[0, g], bs_ref, delta, g * R, m_ref, acc_ref, g * R, *scratch)

    @pl.when(kt >= qt - (WIN // TQ))
    def _():
        for g in range(G):
            q3 = q_ref[0, g * R:(g + 1) * R].reshape(R * TQ, DH)
            _group_update(q3, kw_ref[0, g], vw_ref[0, g], bw_ref, delta, g * R, m_ref, acc_ref, H + g * R, *scratch)

    @pl.when(kt == qt)
    def _():
        sig = jax.nn.sigmoid(gt_ref[...])
        for hh in range(H):
            a_s, a_w = acc_ref[hh], acc_ref[H + hh]
            o_s = a_s / jnp.maximum(pltpu.roll(a_s, DH, 1), 1e-30)
            o_w = a_w / jnp.maximum(pltpu.roll(a_w, DH, 1), 1e-30)
            o = sig[:, H + hh:H + hh + 1] * o_s + sig[:, 2 * H + hh:2 * H + hh + 1] * o_w
            o_ref[0, :, hh * DH:(hh + 1) * DH] = o[:, 0:DH]


def _flash_prompt(q2, ks, vs, kw, vw, bias_s, bias_w, selneg, et_tiles, gates, b, s):
    nq = s // TQ
    nw = WIN // TQ
    pairs = [(j, k) for j in range(nq) for k in range(j + 1)]
    qt_of = jnp.array([p[0] for p in pairs], jnp.int32)
    kt_of = jnp.array([p[1] for p in pairs], jnp.int32)
    const = lambda shape: pl.BlockSpec(shape, lambda i, p, qt, kt: (0,) * len(shape), pipeline_mode=pl.Buffered(1))
    kv = lambda w_, f: pl.BlockSpec((1, G, TQ, w_), lambda i, p, qt, kt: (i, 0, f(qt[p], kt[p]), 0))
    sel_kt = lambda j, k: k
    win_kt = lambda j, k: jnp.maximum(k, jnp.maximum(j - nw, 0))
    return pl.pallas_call(
        _flash_prompt_body, out_shape=SDS((b, s, ATT), f32),
        grid_spec=pltpu.PrefetchScalarGridSpec(
            num_scalar_prefetch=2, grid=(b, len(pairs)),
            in_specs=[pl.BlockSpec((1, H, TQ, DH), lambda i, p, qt, kt: (i, 0, qt[p], 0)),
                      kv(DH, sel_kt), kv(2 * DH, sel_kt), kv(DH, win_kt), kv(2 * DH, win_kt),
                      const(bias_s.shape), const(bias_w.shape),
                      pl.BlockSpec((1, G, TQ, 128), lambda i, p, qt, kt: (i, 0, qt[p], 0)),
                      pl.BlockSpec((1, TQ, 128), lambda i, p, qt, kt: (kt[p], 0, 0)),
                      pl.BlockSpec((TQ, 128), lambda i, p, qt, kt: (i * nq + qt[p], 0))],
            out_specs=pl.BlockSpec((1, TQ, ATT), lambda i, p, qt, kt: (i, qt[p], 0)),
            scratch_shapes=[pltpu.VMEM((2 * H, TQ, 128), f32), pltpu.VMEM((2 * H, TQ, 128), f32),
                            pltpu.VMEM((R * TQ, TQ), f32), pltpu.VMEM((R * TQ, TQ), bf16),
                            pltpu.VMEM((R * TQ, 128), f32)]),
        compiler_params=_cp(("parallel", "arbitrary"), 58), name="flash_prompt")(
            qt_of, kt_of, q2, ks, vs, kw, vw, bias_s, bias_w, selneg, et_tiles, gates)


def _cmp_sel_sample_body(q_ref, kc_ref, vc_ref, gt_ref, e_ref, o_ref, nm_ref, impt_ref, *, past, n_q):
    n_cmp = kc_ref.shape[2]
    n_sel = n_cmp // 4
    sig = jax.nn.sigmoid(gt_ref[0])
    rows = R * n_q
    mp = lax.broadcasted_iota(jnp.int32, (rows, n_cmp), 1)
    i_q = lax.broadcasted_iota(jnp.int32, (rows, n_cmp), 0) % n_q
    valid = (mp >= 1) & (16 * mp + 15 <= past + i_q)
    imps = []
    for g in range(G):
        s = jnp.where(valid, _nt(q_ref[0, g], kc_ref[0, g]), NEG)
        m = jnp.max(s, axis=-1, keepdims=True)
        e = jnp.where(valid, jnp.exp(s - m), 0.0)
        p = e / jnp.maximum(jnp.sum(e, axis=-1, keepdims=True), 1e-30)
        o = _dot(p.astype(bf16), vc_ref[0, g]) * sig[g * rows:(g + 1) * rows, 0:1]
        for r in range(R):
            hh = g * R + r
            o_ref[0, :, hh * DH:(hh + 1) * DH] = o[r * n_q:(r + 1) * n_q, :]
        imps.append(p[0:n_q] + p[n_q:2 * n_q] + p[2 * n_q:3 * n_q])
    imp = jnp.concatenate(imps + [jnp.zeros((128 - G * n_q, n_cmp), f32)], axis=0)
    impt_ref[...] = imp.T
    v = _block_scores(impt_ref, n_sel)
    q_pos = past + lax.broadcasted_iota(jnp.int32, v.shape, 1) % n_q
    val, _ = _force_scores(v, q_pos)
    rank = _rank_desc(val) + jnp.where(FORCE > val, 1, 0)
    neg = jnp.where(rank < TOPK, 0.0, NEG).T
    nm_ref[0] = _dot(neg[0:G * n_q, :].astype(bf16), e_ref[...])


def _cmp_sel_sample(qg, kc, vc, gsw, e_full, past):
    db, n_q = qg.shape[0], qg.shape[2] // R
    n_cmp = kc.shape[2]
    return pl.pallas_call(
        functools.partial(_cmp_sel_sample_body, past=past, n_q=n_q),
        out_shape=(SDS((db, n_q, ATT), f32), SDS((db, G * n_q, past), f32)), grid=(db,),
        in_specs=[pl.BlockSpec((1, G, R * n_q, DH), lambda i: (i, 0, 0, 0)),
                  pl.BlockSpec((1, G, n_cmp, DH), lambda i: (i, 0, 0, 0)),
                  pl.BlockSpec((1, G, n_cmp, DH), lambda i: (i, 0, 0, 0)),
                  pl.BlockSpec((1, H * n_q, 128), lambda i: (i, 0, 0)),
                  _resident((n_cmp // 4, past))],
        out_specs=(pl.BlockSpec((1, n_q, ATT), lambda i: (i, 0, 0)),
                   pl.BlockSpec((1, G * n_q, past), lambda i: (i, 0, 0))),
        scratch_shapes=[pltpu.VMEM((n_cmp, 128), f32)],
        compiler_params=_cp(("parallel",), 32), name="cmp_sel_sample")(qg, kc, vc, gsw, e_full)


def _update_parts(q, parts, bias, m_ref, l_ref, acc_ref, idx):
    s = jnp.concatenate([_dot(q, kt.astype(bf16)) for kt, _ in parts], axis=1) + bias
    m_old = m_ref[idx]
    m_new = jnp.maximum(m_old, jnp.max(s, axis=-1, keepdims=True))
    alpha = jnp.exp(m_old - m_new)
    p = jnp.exp(s - m_new)
    l_ref[idx] = alpha * l_ref[idx] + jnp.sum(p, axis=-1, keepdims=True)
    p = p.astype(bf16)
    pv = acc_ref[idx] * alpha
    off = 0
    for _, vt in parts:
        pv = pv + _nt(p[:, off:off + vt.shape[1]], vt.astype(bf16))
        off += vt.shape[1]
    acc_ref[idx] = pv
    m_ref[idx] = m_new


def _flash_sample_body(*refs, n_q):
    refs = refs[1:]
    pages = refs[:CHUNK_PAGES]
    (q_ref, bias_ref, nm_ref, new_ref, bnew_ref, win_ref, wnew_ref, bwin_ref, gt_ref, o_ref, nwin_ref,
     m_ref, l_ref, acc_ref) = refs[CHUNK_PAGES:]
    step = pl.program_id(1)
    q = q_ref[0]

    @pl.when(step == 0)
    def _():
        m_ref[...] = jnp.full(m_ref.shape, NEG, f32)
        l_ref[...] = jnp.zeros(l_ref.shape, f32)
        acc_ref[...] = jnp.zeros(acc_ref.shape, f32)

    nm = nm_ref[0]
    nm = jnp.concatenate([nm[g * n_q:(g + 1) * n_q] for g in range(G) for _ in range(R)], axis=0)
    parts = [(pages[p][0, 0:G].reshape(G * DH, PAGE), pages[p][0, G:2 * G].reshape(G * DH, PAGE))
             for p in range(CHUNK_PAGES)]
    _update_parts(q, parts, bias_ref[...] + nm, m_ref, l_ref, acc_ref, 0)

    @pl.when(step == pl.num_programs(1) - 1)
    def _():
        new = new_ref[0]
        _update_parts(q, [(new[0:256], new[256:512])], bnew_ref[...], m_ref, l_ref, acc_ref, 0)
        old, wnew = win_ref[0, 0].reshape(8 * DH, WIN), wnew_ref[0]
        _update_parts(q, [(old[0:256], old[256:512]), (wnew[0:256], wnew[256:512])], bwin_ref[...],
                      m_ref, l_ref, acc_ref, 1)
        head, tail = _shifted_state(old, wnew, n_q)
        nwin_ref[0, :, :, 0:WIN - PAGE] = head.reshape(8, DH, WIN - PAGE)
        nwin_ref[0, :, :, WIN - PAGE:WIN] = tail.reshape(8, DH, PAGE)
        sig = jax.nn.sigmoid(gt_ref[0])
        res = (sig[:, 1:2] * acc_ref[0] / jnp.maximum(l_ref[0], 1e-30)
               + sig[:, 2:3] * acc_ref[1] / jnp.maximum(l_ref[1], 1e-30))
        for hh in range(H):
            g = hh // R
            o_ref[0, :, hh * DH:(hh + 1) * DH] = res[hh * n_q:(hh + 1) * n_q, g * DH:(g + 1) * DH]


def _flash_sample(pool, page_table, qbd, bias_slc, negmask, new_slc, bias_new, win_states, li, win_new, bias_win,
                  gsw, n_q):
    db, n_pages = page_table.shape
    steps = n_pages // CHUNK_PAGES
    rows = H * n_q
    keys = CHUNK_PAGES * PAGE
    cm = lambda f: (lambda i, j, pt: f(i, j))
    const = lambda shape: pl.BlockSpec(shape, lambda i, j, pt: (0,) * len(shape), pipeline_mode=pl.Buffered(1))
    per_b = lambda shape: pl.BlockSpec((1,) + shape, lambda i, j, pt: (i,) + (0,) * len(shape))
    page_specs = [pl.BlockSpec((1, 8, DH, PAGE), lambda i, j, pt, k=k: (pt[i, j * CHUNK_PAGES + k], 1, 0, 0))
                  for k in range(CHUNK_PAGES)]
    return pl.pallas_call(
        functools.partial(_flash_sample_body, n_q=n_q),
        out_shape=(SDS((db, n_q, ATT), f32), SDS((db, 8, DH, WIN), f32)),
        grid_spec=pltpu.PrefetchScalarGridSpec(
            num_scalar_prefetch=1, grid=(db, steps),
            in_specs=page_specs + [
                per_b((rows, 256)),
                pl.BlockSpec((rows, keys), cm(lambda i, j: (0, j))),
                pl.BlockSpec((1, G * n_q, keys), cm(lambda i, j: (i, 0, j))),
                per_b((8 * DH, PAGE)), const((rows, PAGE)),
                pl.BlockSpec((1, 1, 8, DH, WIN), lambda i, j, pt: (li, i, 0, 0, 0)),
                per_b((8 * DH, PAGE)), const((rows, WIN + PAGE)),
                per_b((rows, 128))],
            out_specs=(per_b((n_q, ATT)), per_b((8, DH, WIN))),
            scratch_shapes=[pltpu.VMEM((2, rows, 1), f32), pltpu.VMEM((2, rows, 1), f32),
                            pltpu.VMEM((2, rows, 256), f32)]),
        compiler_params=_cp(("arbitrary", "arbitrary"), 48), name="flash_sample")(
            page_table, *([pool] * CHUNK_PAGES), qbd, bias_slc, negmask, new_slc, bias_new,
            win_states, win_new, bias_win, gsw)


def _dil_tile(q, kc, vc, kp, vp, bias_ref, gi, prev_idx):
    lane = lax.broadcasted_iota(jnp.int32, (DT, 128), 1)
    low = lane < DH
    if kp is None:
        kk, vv = kc.astype(bf16), vc.astype(bf16)
    else:
        kk = jnp.concatenate([kp, kc], axis=0).astype(bf16)
        vv = jnp.concatenate([vp, vc], axis=0).astype(bf16)
    outs, lses = [], []
    for e in range(2):
        qe = jnp.where(low if e == 0 else jnp.logical_not(low), q, 0.0).astype(bf16)
        bias = bias_ref[gi, 0, e, 0]
        if kp is not None:
            bias = jnp.concatenate([bias_ref[gi, 0, e, prev_idx], bias], axis=1)
        s = _nt(qe, kk) + bias
        m = jnp.max(s, axis=-1, keepdims=True)
        ex = jnp.exp(s - m)
        l = jnp.maximum(jnp.sum(ex, axis=-1, keepdims=True), 1e-30)
        outs.append(_dot(ex.astype(bf16), vv) / l)
        lses.append(m + jnp.log(l))
    return jnp.where(low, outs[0], outs[1]), jnp.where(low, lses[0], lses[1])


def _dil_combine(os_, ls_):
    mx = jnp.maximum(jnp.maximum(ls_[0], ls_[1]), ls_[2])
    ws = [jnp.exp(l - mx) for l in ls_]
    den = ws[0] + ws[1] + ws[2]
    return [o * w / den for o, w in zip(os_, ws)]


def _dil_prompt_body(q0, q1, q2, k0, k1, k2, v0, v1, v2, bias_ref, y_ref, osc, lsc, *, s):
    qs, ks, vs = (q0, q1, q2), (k0, k1, k2), (v0, v1, v2)
    for gi, (_, dil) in enumerate(DIL):
        n_t = s // dil // DT
        q_r, k_r, v_r = qs[gi], ks[gi], vs[gi]

        def tile(start, prev_start, prev_idx, gi=gi, dil=dil, q_r=q_r, k_r=k_r, v_r=v_r, n_t=n_t):
            rows = lambda st: pl.ds(st, DT, stride=dil) if dil > 1 else pl.ds(pl.multiple_of(st, DT), DT)
            kp = k_r[0, rows(prev_start), :] if n_t > 1 else None
            vp = v_r[0, rows(prev_start), :] if n_t > 1 else None
            o, lse = _dil_tile(q_r[0, rows(start), :], k_r[0, rows(start), :], v_r[0, rows(start), :],
                               kp, vp, bias_ref, gi, prev_idx)
            osc[gi, rows(start), :] = o
            lsc[gi, rows(start), :] = lse

        def body(it, carry, dil=dil, n_t=n_t, tile=tile):
            rho, t = it // n_t, it % n_t
            start = rho + t * (DT * dil)
            prev = rho + jnp.maximum(t - 1, 0) * (DT * dil)
            tile(start, prev, jnp.where(t == 0, 2, 1))
            return carry

        lax.fori_loop(0, dil * n_t, body, 0, unroll=DIL_UNROLL)
    outs = _dil_combine([osc[0], osc[1], osc[2]], [lsc[0], lsc[1], lsc[2]])
    for pair in range(2):
        @pl.when(pl.program_id(1) == pair)
        def _(pair=pair):
            for gi, o in enumerate(outs):
                c = 2 * gi + pair
                y_ref[0, :, c * 128:(c + 1) * 128] = o


def _dil_prompt(zd, bias_tiles, b, s):
    col = lambda base: [pl.BlockSpec((1, s, 128), lambda i, j, c=base + 2 * gi: (i, 0, c + j)) for gi in range(3)]
    return pl.pallas_call(
        functools.partial(_dil_prompt_body, s=s),
        out_shape=SDS((b, s, ATT), f32), grid=(b, 2),
        in_specs=col(0) + col(6) + col(12) + [pl.BlockSpec((3, 1, 2, 3, DT, DT), lambda i, j: (0, j, 0, 0, 0, 0))],
        out_specs=pl.BlockSpec((1, s, ATT), lambda i, j: (i, 0, 0)),
        scratch_shapes=[pltpu.VMEM((3, s, 128), f32), pltpu.VMEM((3, s, 128), f32)],
        compiler_params=_cp(("parallel", "arbitrary"), 48), name="dil_prompt")(*([zd] * 9), bias_tiles)


def _shifted_state(old, new, n_q):
    w = old.shape[1]
    rolled = pltpu.roll(old, w - n_q, 1)
    lane = lax.broadcasted_iota(jnp.int32, new.shape, 1)
    tail = jnp.where(lane >= PAGE - n_q, pltpu.roll(new, PAGE - n_q, 1), rolled[:, w - PAGE:])
    return rolled[:, :w - PAGE], tail


def _dil_sample_body(q_ref, s0, s1, s2, new_ref, b0, b1, b2, y_ref, n0, n1, n2, *, n_q):
    lane = lax.broadcasted_iota(jnp.int32, (n_q, 256), 1)
    os_, ls_ = [], []
    for gi, (st, bias, nst) in enumerate(((s0, b0, n0), (s1, b1, n1), (s2, b2, n2))):
        w = st.shape[4]
        old = st[0, 0].reshape(8 * DH, w)
        new = new_ref[0, gi]
        head, tail = _shifted_state(old, new, n_q)
        if w > PAGE:
            nst[0, :, :, 0:w - PAGE] = head.reshape(8, DH, w - PAGE)
        nst[0, :, :, w - PAGE:w] = tail.reshape(8, DH, PAGE)
        q = q_ref[0, gi]
        s = jnp.concatenate([_dot(q, old[0:256].astype(bf16)), _dot(q, new[0:256].astype(bf16))], axis=1) + bias[...]
        m = jnp.max(s, axis=-1, keepdims=True)
        e = jnp.exp(s - m)
        l = jnp.maximum(jnp.sum(e, axis=-1, keepdims=True), 1e-30)
        eb = e.astype(bf16)
        res = (_nt(eb[:, 0:w], old[256:512].astype(bf16)) + _nt(eb[:, w:], new[256:512].astype(bf16))) / l
        lse = m + jnp.log(l)
        o = jnp.zeros((n_q, 256), f32)
        lm = jnp.zeros((n_q, 256), f32)
        for hg in range(4):
            inb = (lane >= hg * DH) & (lane < (hg + 1) * DH)
            o = jnp.where(inb, res[hg * n_q:(hg + 1) * n_q, :], o)
            lm = jnp.where(inb, lse[hg * n_q:(hg + 1) * n_q, :], lm)
        os_.append(o)
        ls_.append(lm)
    for gi, o in enumerate(_dil_combine(os_, ls_)):
        y_ref[0, :, gi * 256:(gi + 1) * 256] = o


def _dil_sample(qbd, states, li, new_t, biases, n_q):
    db = new_t.shape[0]
    per_b = lambda shape: pl.BlockSpec((1,) + shape, lambda i: (i,) + (0,) * len(shape))
    wins = [st.shape[4] for st in states]
    return pl.pallas_call(
        functools.partial(_dil_sample_body, n_q=n_q),
        out_shape=(SDS((db, n_q, ATT), f32),) + tuple(SDS((db, 8, DH, w), f32) for w in wins), grid=(db,),
        in_specs=[per_b((3, 4 * n_q, 256))]
                 + [pl.BlockSpec((1, 1, 8, DH, w), lambda i: (li, i, 0, 0, 0)) for w in wins]
                 + [per_b((3, 8 * DH, PAGE))] + [_resident(bb.shape) for bb in biases],
        out_specs=(per_b((n_q, ATT)),) + tuple(per_b((8, DH, w)) for w in wins),
        compiler_params=_cp(("parallel",), 48), name="dil_sample")(qbd, *states, new_t, *biases)


def _rel_bucket(dist):
    dist = jnp.maximum(dist, 0)
    d32 = jnp.maximum(dist, 1).astype(f32)
    large = 16 + (jnp.log(d32 / 16) / math.log(2048 / 16) * 16).astype(jnp.int32)
    return jnp.where(dist < 16, dist, jnp.minimum(large, 31))


def _bias_rev(tbl, n, keep, scale=1):
    dist = (n - 1 - jnp.arange(n)) * scale
    return jnp.where(keep(dist), tbl[_rel_bucket(dist)].T, NEG)


def _toeplitz_tiles(bdr, n_delta, t):
    hn = bdr.shape[0]
    bdp = jnp.concatenate([bdr, jnp.full((hn, t - 1), NEG, f32)], axis=1)
    lp = bdp.shape[1]
    gq = jnp.stack([bdp[:, lp + 1 - d * t - 2 * t:lp - d * t] for d in range(n_delta)])
    gq = jnp.pad(gq, ((0, 0), (0, 0), (0, 1)))
    x = jnp.broadcast_to(gq[:, :, None, :], (n_delta, hn, t, 2 * t)).reshape(n_delta, hn, t * 2 * t)
    y = x[..., :t * (2 * t - 1)].reshape(n_delta, hn, t, 2 * t - 1)
    return y[..., t - 1:]


def _rows_by_query(bdr, n_q, n_keys, first_dist):
    last = bdr.shape[1] - 1
    return jnp.stack([bdr[:, last - first_dist - i:last - first_dist - i + n_keys] for i in range(n_q)], axis=1)


def _new_token_bias(tbl, n_q, keep):
    tri = jnp.arange(n_q)[:, None] - jnp.arange(PAGE)[None, :]
    ok = (tri >= 0) & keep(tri)
    return jnp.where(ok[None], jnp.transpose(tbl[_rel_bucket(tri)], (2, 0, 1)), NEG)


def _nsa_tables(rel_bias, s, n_q, past):
    everywhere = lambda d: d >= 0
    in_window = lambda d: d < WIN
    n = past + n_q
    key_blk = jnp.arange(s // TQ)[:, None, None] * (TQ // SEL_BLOCK) + jnp.arange(TQ)[None, :, None] // SEL_BLOCK
    bias_new = _new_token_bias(rel_bias, n_q, everywhere).reshape(H * n_q, PAGE)
    bias_win = _rows_by_query(_bias_rev(rel_bias, n, in_window), n_q, WIN, WIN).reshape(H * n_q, WIN)
    return dict(
        perm=_chunk_perm(),
        bias_s=_toeplitz_tiles(_bias_rev(rel_bias, s, everywhere), s // TQ, TQ) * LOG2E,
        bias_w=_toeplitz_tiles(_bias_rev(rel_bias, WIN + TQ, in_window), WIN // TQ + 1, TQ) * LOG2E,
        et_tiles=(key_blk == jnp.arange(128)[None, None, :]).astype(bf16),
        e_full=(jnp.arange(past // SEL_BLOCK)[:, None] == jnp.arange(past)[None, :] // SEL_BLOCK).astype(bf16),
        bias_slc=_rows_by_query(_bias_rev(rel_bias, n, everywhere), n_q, past, past).reshape(H * n_q, past),
        bias_new=bias_new,
        bias_win=jnp.concatenate([bias_win, bias_new], axis=1))


def _nsa_layer(yp, ys, g_pre, g_post, w_in, w_out, ffn, cmp_pos, cmp_w1, cmp_w2, tb, pool, page_table, win_states,
               li, b, s, db, n_q, past):
    w_pad = jnp.pad(w_in, ((0, 0), (0, NSA_PAD - w_in.shape[1]))).astype(bf16)
    w_out = w_out.astype(bf16)
    w1cat = jnp.concatenate([cmp_w1[:, :1024], cmp_w1[:, 1024:]], axis=-1).astype(bf16)
    pos2 = cmp_pos.reshape(2, 2, 1024)
    w2 = cmp_w2.astype(bf16)

    q_hm, q2_hm, kvt, kwin, ks, vs, kw, vw, gates = _nsa_proj_prompt(yp, g_pre, w_pad, b, s)
    kc, vc = _compress_prompt(kvt, tb["perm"], w1cat, pos2, w2, b, s)
    o_cmp, selneg = _cmp_sel_prompt(q_hm, kc, vc, gates, b, s)
    o_sw = _flash_prompt(q2_hm, ks, vs, kw, vw, tb["bias_s"], tb["bias_w"], selneg, tb["et_tiles"], gates, b, s)
    yp = _mix_ffn(yp, g_post, w_out, ffn, o_cmp.reshape(b * s, ATT), o_sw.reshape(b * s, ATT))
    kv_p = jnp.transpose(kvt.reshape(b, 4, G, DH, s), (0, 4, 1, 2, 3))
    win_p = kwin.reshape(b, s, 2, G, DH)[:, s - min(WIN, s):]

    z = _proj(ys, g_pre, w_pad)
    q = (z[:, :ATT] * (DH ** -0.5)).reshape(db, n_q, H, DH).transpose(0, 2, 1, 3)
    qg = q.reshape(db, G, R * n_q, DH).astype(bf16)
    eye = jnp.eye(G, dtype=f32)
    qbd = (q.reshape(db, G, R * n_q, 1, DH) * eye[None, :, None, :, None]).reshape(db, H * n_q, G * DH).astype(bf16)
    kv_new = z[:, ATT:ATT + 1024].reshape(db, n_q, 1024)
    win_new = z[:, ATT + 1024:ATT + 1536].reshape(db, n_q, 512)
    gt = z[:, ATT + 1536:ATT + 1536 + 3 * H].reshape(db, n_q, 3, H).transpose(0, 3, 1, 2)
    gsw = jnp.pad(gt.reshape(db, H * n_q, 3), ((0, 0), (0, 0), (0, 125)))
    kc_s, vc_s = _compress_sample(pool, page_table, tb["perm"], w1cat, pos2, w2)
    o_cmp_s, negmask = _cmp_sel_sample(qg, kc_s, vc_s, gsw, tb["e_full"], past)
    page_t = lambda x: jnp.pad(jnp.transpose(x, (0, 2, 1)), ((0, 0), (0, 0), (0, PAGE - n_q)))
    win_view = jnp.transpose(win_states, (0, 1, 3, 4, 5, 2)).reshape(win_states.shape[0], db, 8, DH, WIN)
    o_sw_s, win_t = _flash_sample(pool, page_table, qbd, tb["bias_slc"], negmask, page_t(kv_new[:, :, 512:]),
                                  tb["bias_new"], win_view, li, page_t(win_new), tb["bias_win"], gsw, n_q)
    ys = _mix_ffn(ys, g_post, w_out, ffn, o_cmp_s.reshape(db * n_q, ATT), o_sw_s.reshape(db * n_q, ATT))
    kv_s = kv_new.reshape(db, n_q, 4, G, DH)
    win_s = jnp.transpose(win_t.reshape(db, 2, G, DH, WIN), (0, 4, 1, 2, 3))
    return yp, ys, kv_p, win_p, kv_s, win_s


def _dil_tables(rel_bias, n_q):
    tiles, biases = [], []
    for gi, (win, dil) in enumerate(DIL):
        tbl = rel_bias[:, gi * 4:(gi + 1) * 4]
        on_grid = lambda d, win=win, dil=dil: (d % dil == 0) & (d <= win)
        tl = _toeplitz_tiles(_bias_rev(tbl, 2 * DT, on_grid, scale=dil), 2, DT)
        tiles.append(jnp.concatenate([tl, jnp.full_like(tl[:1], NEG)]))
        b_all = jnp.concatenate([_rows_by_query(_bias_rev(tbl, win + n_q, on_grid), n_q, win, win),
                                 _new_token_bias(tbl, n_q, on_grid)], axis=2)
        biases.append(b_all.reshape(4 * n_q, win + PAGE))
    return dict(tiles=jnp.stack(tiles).transpose(0, 2, 1, 3, 4).reshape(3, 2, 2, 3, DT, DT), sample=biases)


def _dil_layer(yp, ys, g_pre, g_post, w_in, w_out, ffn, tb, states, li, b, s, db, n_q):
    w_in = w_in.astype(bf16)
    w_out = w_out.astype(bf16)

    zd, kvt = _dil_proj(yp, g_pre, w_in, b, s)
    y = _dil_prompt(zd.reshape(b, s, 3 * ATT), tb["tiles"], b, s)
    yp = _mix_ffn(yp, g_post, w_out, ffn, y.reshape(b * s, ATT))
    kvt = kvt.reshape(b, 2, H, DH, s)
    st_p = []
    for gi, (win, _) in enumerate(DIL):
        keep = min(win, s)
        st_p.append(jnp.transpose(kvt[:, :, gi * 4:(gi + 1) * 4, :, s - keep:], (0, 4, 1, 2, 3)))

    z = _proj(ys, g_pre, w_in).reshape(db, n_q, 3, 3, 4, DH)
    q = (z[:, :, 0] * (DH ** -0.5)).transpose(0, 2, 3, 1, 4)
    eye = jnp.eye(4, dtype=f32)
    qbd = (q[:, :, :, :, None, :] * eye[None, None, :, None, :, None]).reshape(db, 3, 4 * n_q, 256).astype(bf16)
    new_t = jnp.transpose(z[:, :, 1:3], (0, 3, 2, 4, 5, 1)).reshape(db, 3, 8 * DH, n_q)
    new_t = jnp.pad(new_t, ((0, 0), (0, 0), (0, 0), (0, PAGE - n_q)))
    views = [jnp.transpose(st, (0, 1, 3, 4, 5, 2)).reshape(st.shape[0], db, 8, DH, st.shape[2]) for st in states]
    y_s, n0, n1, n2 = _dil_sample(qbd, views, li, new_t, tb["sample"], n_q)
    ys = _mix_ffn(ys, g_post, w_out, ffn, y_s.reshape(db * n_q, ATT))
    st_s = [jnp.transpose(n.reshape(db, 2, 4, DH, n.shape[3]), (0, 4, 1, 2, 3)) for n in (n0, n1, n2)]
    return yp, ys, st_p, st_s


def kernel(x_prompt, x_sample, cache_nsa_kv, page_table, state_nsa_win, state_dil_w128, state_dil_w512,
           state_dil_w2048, norm_g, ffn_w_in, ffn_w_out, nsa_w_in, nsa_w_out, nsa_cmp_pos, nsa_cmp_w1,
           nsa_cmp_w2, dil_w_in, dil_w_out, rel_bias):
    b, s, _ = x_prompt.shape
    db, n_q, _ = x_sample.shape
    depth = norm_g.shape[0]
    past = page_table.shape[1] * PAGE
    yp = x_prompt.reshape(b * s, D)
    ys = x_sample.reshape(db * n_q, D)
    nsa_kv_p, nsa_win_p, nsa_kv_s, nsa_win_s = [], [], [], []
    dil_p, dil_s = [[], [], []], [[], [], []]
    nsa_tb = _nsa_tables(rel_bias, s, n_q, past)
    dil_tb = _dil_tables(rel_bias, n_q)
    for i in range(depth):
        li = i // 2
        g = norm_g[i]
        w_i, w_o = ffn_w_in[i, 0].astype(bf16), ffn_w_out[i, 0].astype(bf16)
        yp = _half_ffn(yp, g[0:2], w_i, w_o)
        ys = _half_ffn(ys, g[0:2], w_i, w_o)
        ffn_b = (g[4:6], ffn_w_in[i, 1].astype(bf16), ffn_w_out[i, 1].astype(bf16))
        if i % 2 == 0:
            n_pool = cache_nsa_kv.shape[1]
            pool = jnp.transpose(cache_nsa_kv, (0, 1, 3, 4, 5, 2)).reshape(
                cache_nsa_kv.shape[0] * n_pool, 4 * G, DH, PAGE)
            yp, ys, kvp, winp, kvs, wins = _nsa_layer(
                yp, ys, g[2:3], g[3:4], nsa_w_in[li], nsa_w_out[li], ffn_b, nsa_cmp_pos[li], nsa_cmp_w1[li],
                nsa_cmp_w2[li], nsa_tb, pool, page_table + li * n_pool, state_nsa_win, li, b, s, db, n_q, past)
            nsa_kv_p.append(kvp)
            nsa_win_p.append(winp)
            nsa_kv_s.append(kvs)
            nsa_win_s.append(wins)
        else:
            states = (state_dil_w128, state_dil_w512, state_dil_w2048)
            yp, ys, stp, sts = _dil_layer(yp, ys, g[2:3], g[3:4], dil_w_in[li], dil_w_out[li], ffn_b, dil_tb, states,
                                          li, b, s, db, n_q)
            for gi in range(3):
                dil_p[gi].append(stp[gi])
                dil_s[gi].append(sts[gi])
    return (yp.reshape(b, s, D), ys.reshape(db, n_q, D),
            jnp.stack(nsa_kv_p), jnp.stack(nsa_win_p),
            jnp.stack(dil_p[0]), jnp.stack(dil_p[1]), jnp.stack(dil_p[2]),
            jnp.stack(nsa_kv_s), jnp.stack(nsa_win_s),
            jnp.stack(dil_s[0]), jnp.stack(dil_s[1]), jnp.stack(dil_s[2]))
```

```python
import functools
import math

import jax
import jax.numpy as jnp
from jax import lax
from jax.experimental import pallas as pl
from jax.experimental.pallas import tpu as pltpu

f32, bf16 = jnp.float32, jnp.bfloat16
SDS = jax.ShapeDtypeStruct

D = 1024
FF = 2816
FF_CHUNK = 256
H, DH = 12, 64
G, R = 4, 3
ATT = H * DH
EPS = 1e-6
NEG = -1e30
LOG2E = math.log2(math.e)
FORCE = 1e4
SEL_BLOCK, TOPK = 64, 16
WIN = 512
NSA_PAD = 2432
TQ = 256
RC = 64
PAGE = 128
CHUNK_PAGES = 32
CMP_GROUP = 1
DIL = ((128, 1), (512, 4), (2048, 16))
DT = 128
DIL_UNROLL = 8
MIB = 1024 * 1024


def _cp(sem, vmem_mib):
    return pltpu.CompilerParams(dimension_semantics=sem, vmem_limit_bytes=vmem_mib * MIB)


def _rms(x, g):
    return x * lax.rsqrt(jnp.mean(x * x, axis=-1, keepdims=True) + EPS) * g


def _nt(a, b):
    return lax.dot_general(a, b, (((1,), (1,)), ((), ())), preferred_element_type=f32)


def _dot(a, b):
    return jnp.dot(a, b, preferred_element_type=f32)


def _resident(shape):
    nd = len(shape)
    return pl.BlockSpec(shape, lambda *_: (0,) * nd, pipeline_mode=pl.Buffered(1))


def _ffn_core(x, g_ref, win_ref, wout_ref):
    h = _rms(x, g_ref[0:1, :]).astype(bf16)
    acc = jnp.zeros(x.shape, f32)
    for c in range(FF // FF_CHUNK):
        lo, hi = c * FF_CHUNK, (c + 1) * FF_CHUNK
        gate = _dot(h, win_ref[:, lo:hi])
        up = _dot(h, win_ref[:, FF + lo:FF + hi])
        act = (gate * jax.nn.sigmoid(gate) * up).astype(bf16)
        acc = acc + _dot(act, wout_ref[lo:hi, :])
    return x + 0.5 * _rms(acc, g_ref[1:2, :])


def _ffn_body(x_ref, g_ref, win_ref, wout_ref, o_ref):
    o_ref[...] = _ffn_core(x_ref[...], g_ref, win_ref, wout_ref)


def _mix_ffn_body(x_ref, gm_ref, wo_ref, g_ref, win_ref, wout_ref, *refs):
    o = refs[0][...]
    for r in refs[1:-1]:
        o = o + r[...]
    x = x_ref[...] + _rms(_dot(o.astype(bf16), wo_ref[...]), gm_ref[...])
    refs[-1][...] = _ffn_core(x, g_ref, win_ref, wout_ref)


def _mix_ffn(x, g_post, w_o, ffn, *branches):
    g2, w_in, w_out = ffn
    m = x.shape[0]
    tm = min(m, 512)
    row = lambda n: pl.BlockSpec((tm, n), lambda i: (i, 0))
    return pl.pallas_call(
        _mix_ffn_body, out_shape=SDS((m, D), f32), grid=(m // tm,),
        in_specs=[row(D), _resident((1, D)), _resident((ATT, D)), _resident((2, D)), _resident((D, 2 * FF)),
                  _resident((FF, D))] + [row(ATT)] * len(branches),
        out_specs=row(D), compiler_params=_cp(("parallel",), 56), name="mix_ffn")(
            x, g_post, w_o, g2, w_in, w_out, *branches)


def _half_ffn(x, g2, w_in, w_out):
    m = x.shape[0]
    tm = min(m, 512)
    return pl.pallas_call(
        _ffn_body, out_shape=SDS((m, D), f32), grid=(m // tm,),
        in_specs=[pl.BlockSpec((tm, D), lambda i: (i, 0)), _resident((2, D)),
                  _resident((D, 2 * FF)), _resident((FF, D))],
        out_specs=pl.BlockSpec((tm, D), lambda i: (i, 0)),
        compiler_params=_cp(("parallel",), 48), name="half_ffn")(x, g2, w_in, w_out)


def _proj_body(x_ref, g_ref, w_ref, o_ref):
    h = _rms(x_ref[...], g_ref[...]).astype(bf16)
    o_ref[...] = _dot(h, w_ref[...])


def _proj(x, g, w):
    m, n = x.shape[0], w.shape[1]
    return pl.pallas_call(
        _proj_body, out_shape=SDS((m, n), f32), grid=(1,),
        in_specs=[_resident((m, D)), _resident((1, D)), _resident((D, n))],
        out_specs=pl.BlockSpec((m, n), lambda i: (0, 0)),
        compiler_params=_cp(("arbitrary",), 32), name="proj_sample")(x, g, w)


def _nsa_proj_body(x_ref, g_ref, w_ref, q_ref, q2_ref, kv_ref, kwin_ref, ks_ref, vs_ref, kw_ref, vw_ref, gt_ref):
    h = _rms(x_ref[...], g_ref[...]).astype(bf16)
    zq = _dot(h, w_ref[:, 0:ATT]) * (DH ** -0.5)
    zq2 = zq * LOG2E
    for hh in range(H):
        q_ref[0, hh] = zq[:, hh * DH:(hh + 1) * DH].astype(bf16)
        q2_ref[0, hh] = zq2[:, hh * DH:(hh + 1) * DH].astype(bf16)
    zc = _dot(h, w_ref[:, ATT:ATT + 512])
    kv_ref[0, 0:2 * G] = zc.T.reshape(2 * G, DH, zc.shape[0])
    zs = _dot(h, w_ref[:, ATT + 512:ATT + 1024])
    kv_ref[0, 2 * G:4 * G] = zs.T.reshape(2 * G, DH, zs.shape[0])
    zw = _dot(h, w_ref[:, ATT + 1024:ATT + 1536])
    kwin_ref[...] = zw
    ones = jnp.ones((zs.shape[0], DH), bf16)
    for g in range(G):
        ks_ref[0, g] = zs[:, g * DH:(g + 1) * DH].astype(bf16)
        vs_ref[0, g] = jnp.concatenate([zs[:, 256 + g * DH:256 + (g + 1) * DH].astype(bf16), ones], axis=1)
        kw_ref[0, g] = zw[:, g * DH:(g + 1) * DH].astype(bf16)
        vw_ref[0, g] = jnp.concatenate([zw[:, 256 + g * DH:256 + (g + 1) * DH].astype(bf16), ones], axis=1)
    gt_ref[...] = _dot(h, w_ref[:, ATT + 1536:NSA_PAD])


def _nsa_proj_prompt(x, g, w, b, s):
    tm = 512
    nt = s // tm
    hm = lambda n, w_: pl.BlockSpec((1, n, tm, w_), lambda i: (i // nt, 0, i % nt, 0))
    rows = lambda n: pl.BlockSpec((tm, n), lambda i: (i, 0))
    return pl.pallas_call(
        _nsa_proj_body,
        out_shape=(SDS((b, H, s, DH), bf16), SDS((b, H, s, DH), bf16), SDS((b, 4 * G, DH, s), f32),
                   SDS((b * s, 512), f32), SDS((b, G, s, DH), bf16), SDS((b, G, s, 2 * DH), bf16),
                   SDS((b, G, s, DH), bf16), SDS((b, G, s, 2 * DH), bf16), SDS((b * s, 128), f32)),
        grid=(b * s // tm,),
        in_specs=[rows(D), _resident((1, D)), _resident((D, NSA_PAD))],
        out_specs=(hm(H, DH), hm(H, DH), pl.BlockSpec((1, 4 * G, DH, tm), lambda i: (i // nt, 0, 0, i % nt)),
                   rows(512), hm(G, DH), hm(G, 2 * DH), hm(G, DH), hm(G, 2 * DH), rows(128)),
        compiler_params=_cp(("parallel",), 48), name="nsa_proj_prompt")(x, g, w)


def _dil_proj_body(x_ref, g_ref, w_ref, o_ref, kvt_ref):
    h = _rms(x_ref[...], g_ref[...]).astype(bf16)
    o_ref[:, 0:ATT] = _dot(h, w_ref[:, 0:ATT]) * (DH ** -0.5)
    for part in (1, 2):
        z = _dot(h, w_ref[:, part * ATT:(part + 1) * ATT])
        o_ref[:, part * ATT:(part + 1) * ATT] = z
        kvt_ref[0, (part - 1) * H:part * H] = z.T.reshape(H, DH, z.shape[0])


def _dil_proj(x, g, w, b, s):
    tm = 512
    nt = s // tm
    return pl.pallas_call(
        _dil_proj_body, out_shape=(SDS((b * s, 3 * ATT), f32), SDS((b, 2 * H, DH, s), f32)), grid=(b * s // tm,),
        in_specs=[pl.BlockSpec((tm, D), lambda i: (i, 0)), _resident((1, D)), _resident((D, 3 * ATT))],
        out_specs=(pl.BlockSpec((tm, 3 * ATT), lambda i: (i, 0)),
                   pl.BlockSpec((1, 2 * H, DH, tm), lambda i: (i // nt, 0, 0, i % nt))),
        compiler_params=_cp(("parallel",), 48), name="dil_proj")(x, g, w)


def _compress_body(*refs, n_src, n_pages, paged, group=1):
    if paged:
        refs = refs[1:]
    srcs = refs[:n_src]
    perm_ref, w1_ref, pos_ref, w2_ref, kc_ref, vc_ref, lhs_ref, aprev_ref = refs[n_src:]
    nch = n_pages * PAGE // 16
    gn = group * nch
    step = pl.program_id(1) if paged else 0
    sub = step % group if paged else 0

    for p in range(n_pages):
        mats = srcs[p][0] if paged else srcs[0][0, :, :, p * PAGE:(p + 1) * PAGE]
        both = jnp.concatenate([jnp.concatenate([mats[2 * c], mats[2 * c + 1]], axis=1) for c in range(4)], axis=0)
        out = _dot(both.astype(bf16), perm_ref[...])
        for c in range(4):
            kind, pair = c // 2, c % 2
            o = out[c * DH:(c + 1) * DH]
            zt = jnp.concatenate([o[:, 0:128], o[:, 128:256]], axis=0).T
            for gl in range(2):
                start = (2 * pair + gl) * gn + sub * nch + p * 8
                dst = pl.ds(pl.multiple_of(start, 8), 8) if paged else pl.ds(start, 8)
                for r2 in range(8):
                    lhs_ref[kind, dst, r2 * 128:(r2 + 1) * 128] = zt[(gl * 8 + r2) * 8:(gl * 8 + r2 + 1) * 8, :]

    def mlp():
        first = (step < group) if paged else True
        if paged:
            @pl.when(step == group - 1)
            def _():
                aprev_ref[...] = jnp.zeros(aprev_ref.shape, f32)

        row = lax.broadcasted_iota(jnp.int32, (gn, 256), 0)
        row_o = lax.broadcasted_iota(jnp.int32, (gn, DH), 0)
        for kind, out_ref in ((0, kc_ref), (1, vc_ref)):
            pp = _dot(lhs_ref[kind].astype(bf16), w1_ref[kind])
            pos = _dot(pos_ref[kind].astype(bf16), w1_ref[kind])
            posterm = pos[0:1, 0:256] + pos[1:2, 256:512]
            for g in range(G):
                a = pp[g * gn:(g + 1) * gn, 0:256]
                bm = pp[g * gn:(g + 1) * gn, 256:512]
                shifted = pltpu.roll(a, 1, 0)
                if paged:
                    carry = aprev_ref[kind, (g + 1) * gn - 1:(g + 1) * gn, :]
                    shifted = jnp.where(row == 0, carry, shifted)
                    aprev_ref[kind, g * gn:(g + 1) * gn, :] = a
                pre = shifted + bm + posterm
                hid = (pre * (0.5 + 0.5 * jnp.tanh(0.5 * pre))).astype(bf16)
                out = _dot(hid, w2_ref[kind])
                out = jnp.where((row_o == 0) & first, 0.0, out)
                out_ref[0, g] = out.astype(bf16)

    if paged:
        pl.when(sub == group - 1)(mlp)
    else:
        mlp()


def _chunk_perm():
    t = jnp.arange(PAGE)
    col = (t % 2) * 128 + ((t % 16) // 2) * 8 + t // 16
    src = jnp.concatenate([col, col + DH])
    return (src[:, None] == jnp.arange(256)[None, :]).astype(bf16)


def _compress_prompt(kvt, perm, w1cat, pos2, w2, b, s):
    nch = s // 16
    out = SDS((b, G, nch, DH), bf16)
    ospec = pl.BlockSpec((1, G, nch, DH), lambda i: (i, 0, 0, 0))
    return pl.pallas_call(
        functools.partial(_compress_body, n_src=1, n_pages=s // PAGE, paged=False),
        out_shape=(out, out), grid=(b,),
        in_specs=[pl.BlockSpec((1, 8, DH, s), lambda i: (i, 0, 0, 0)), _resident((256, 256)),
                  _resident((2, 1024, 512)), _resident((2, 2, 1024)), _resident((2, 256, DH))],
        out_specs=(ospec, ospec),
        scratch_shapes=[pltpu.VMEM((2, G * nch, 1024), f32), pltpu.VMEM((2, G * nch, 256), f32)],
        compiler_params=_cp(("parallel",), 48), name="compress_prompt")(kvt, perm, w1cat, pos2, w2)


def _compress_sample(pool_t, page_table, perm, w1cat, pos2, w2):
    db, n_pages = page_table.shape
    steps = n_pages // CHUNK_PAGES
    gn = CMP_GROUP * CHUNK_PAGES * PAGE // 16
    out = SDS((db, G, n_pages * PAGE // 16, DH), bf16)
    ospec = pl.BlockSpec((1, G, gn, DH), lambda i, j, pt: (i, 0, j // CMP_GROUP, 0))
    const = lambda shape: pl.BlockSpec(shape, lambda i, j, pt: (0,) * len(shape), pipeline_mode=pl.Buffered(1))
    page_specs = [pl.BlockSpec((1, 8, DH, PAGE), lambda i, j, pt, k=k: (pt[i, j * CHUNK_PAGES + k], 0, 0, 0))
                  for k in range(CHUNK_PAGES)]
    return pl.pallas_call(
        functools.partial(_compress_body, n_src=CHUNK_PAGES, n_pages=CHUNK_PAGES, paged=True, group=CMP_GROUP),
        out_shape=(out, out),
        grid_spec=pltpu.PrefetchScalarGridSpec(
            num_scalar_prefetch=1, grid=(db, steps),
            in_specs=page_specs + [const((256, 256)), const((2, 1024, 512)), const((2, 2, 1024)),
                                   const((2, 256, DH))],
            out_specs=(ospec, ospec),
            scratch_shapes=[pltpu.VMEM((2, G * gn, 1024), f32), pltpu.VMEM((2, G * gn, 256), f32)]),
        compiler_params=_cp(("arbitrary", "arbitrary"), 48), name="compress_sample")(
            page_table, *([pool_t] * CHUNK_PAGES), perm, w1cat, pos2, w2)


def _block_scores(impt_ref, n_blocks):
    return (impt_ref[pl.ds(1, n_blocks, stride=4), :] + impt_ref[pl.ds(2, n_blocks, stride=4), :]
            + impt_ref[pl.ds(3, n_blocks, stride=4), :])


def _rank_desc(val):
    n = val.shape[0]
    tiles = [val[8 * t:8 * t + 8] for t in range(n // 8)]
    ranks = [jnp.zeros((8, val.shape[1]), jnp.int32) for _ in tiles]
    row = lax.broadcasted_iota(jnp.int32, (8, val.shape[1]), 0)
    for k in range(n):
        vk = val[k:k + 1, :]
        for t, blk in enumerate(tiles):
            if 8 * t > k:
                beats = vk >= blk
            elif 8 * t + 7 < k:
                beats = vk > blk
            else:
                beats = (vk > blk) | ((vk == blk) & (row + 8 * t > k))
            ranks[t] = ranks[t] + jnp.where(beats, 1, 0)
    return jnp.concatenate(ranks, axis=0)


def _force_scores(v, q_pos):
    j = lax.broadcasted_iota(jnp.int32, v.shape, 0)
    cur = q_pos >> 6
    forced = (j == 0) | (j == cur) | (j == cur - 1)
    return jnp.where(forced, FORCE, jnp.where(j > cur, -FORCE, v)), j


def _cmp_sel_prompt_body(q_ref, kc_ref, vc_ref, gt_ref, o_ref, sel_ref, impt_ref, *, n_sel):
    qt = pl.program_id(1)
    sig = jax.nn.sigmoid(gt_ref[...])
    t = qt * TQ + lax.broadcasted_iota(jnp.int32, (TQ, 128), 0)
    mp = lax.broadcasted_iota(jnp.int32, (TQ, 128), 1)
    valid = (mp >= 1) & (16 * mp + 15 <= t)
    for g in range(G):
        kc, vc = kc_ref[0, g], vc_ref[0, g]
        imp = jnp.zeros((TQ, 128), f32)
        for r in range(R):
            hh = g * R + r
            s = jnp.where(valid, _nt(q_ref[0, hh], kc), NEG)
            m = jnp.max(s, axis=-1, keepdims=True)
            e = jnp.where(valid, jnp.exp(s - m), 0.0)
            p = e / jnp.maximum(jnp.sum(e, axis=-1, keepdims=True), 1e-30)
            imp = imp + p
            o_ref[0, :, hh * DH:(hh + 1) * DH] = _dot(p.astype(bf16), vc) * sig[:, hh:hh + 1]
        for half in range(TQ // 128):
            impt_ref[...] = imp[half * 128:(half + 1) * 128, :].T
            v = _block_scores(impt_ref, n_sel)
            q_pos = qt * TQ + half * 128 + lax.broadcasted_iota(jnp.int32, v.shape, 1)
            val, _ = _force_scores(v, q_pos)
            neg = jnp.where(_rank_desc(val) < TOPK, 0.0, NEG)
            full = jnp.concatenate([neg, jnp.zeros((128 - n_sel, 128), f32)], axis=0)
            sel_ref[0, g, half * 128:(half + 1) * 128, :] = full.T.astype(bf16)


def _cmp_sel_prompt(q_hm, kc, vc, gates, b, s):
    nq = s // TQ
    return pl.pallas_call(
        functools.partial(_cmp_sel_prompt_body, n_sel=s // SEL_BLOCK),
        out_shape=(SDS((b, s, ATT), f32), SDS((b, G, s, 128), bf16)), grid=(b, nq),
        in_specs=[pl.BlockSpec((1, H, TQ, DH), lambda i, j: (i, 0, j, 0)),
                  pl.BlockSpec((1, G, 128, DH), lambda i, j: (i, 0, 0, 0)),
                  pl.BlockSpec((1, G, 128, DH), lambda i, j: (i, 0, 0, 0)),
                  pl.BlockSpec((TQ, 128), lambda i, j: (i * nq + j, 0))],
        out_specs=(pl.BlockSpec((1, TQ, ATT), lambda i, j: (i, j, 0)),
                   pl.BlockSpec((1, G, TQ, 128), lambda i, j: (i, 0, j, 0))),
        scratch_shapes=[pltpu.VMEM((128, 128), f32)],
        compiler_params=_cp(("parallel", "parallel"), 32), name="cmp_sel_prompt")(q_hm, kc, vc, gates)


def _group_update(qa, ka, va, bias_ref, delta, h0, m_ref, acc_ref, idx0, s_ref, p_ref, a_ref):
    s_ref[...] = _nt(qa, ka)
    for r in range(R):
        for c in range(TQ // RC):
            lo = c * RC
            s = s_ref[r * TQ + lo:r * TQ + lo + RC, :] + bias_ref[delta, h0 + r, lo:lo + RC, :]
            m_old = m_ref[idx0 + r, lo:lo + RC, :]
            m_new = jnp.maximum(m_old, jnp.max(s, axis=-1, keepdims=True))
            p_ref[r * TQ + lo:r * TQ + lo + RC, :] = jnp.exp2(
                s - jnp.concatenate([m_new] * (TQ // 128), axis=1)).astype(bf16)
            m_ref[idx0 + r, lo:lo + RC, :] = m_new
            a_ref[r * TQ + lo:r * TQ + lo + RC, :] = jnp.exp2(m_old - m_new)
    pv = _dot(p_ref[...], va)
    for r in range(R):
        acc_ref[idx0 + r] = a_ref[r * TQ:(r + 1) * TQ, :] * acc_ref[idx0 + r] + pv[r * TQ:(r + 1) * TQ, :]


def _flash_prompt_body(qt_ref, kt_ref, q_ref, ks_ref, vs_ref, kw_ref, vw_ref, bs_ref, bw_ref, sel_ref, et_ref, gt_ref,
                       o_ref, m_ref, acc_ref, s_ref, p_ref, a_ref):
    qt, kt = qt_ref[pl.program_id(1)], kt_ref[pl.program_id(1)]
    delta = qt - kt
    scratch = (s_ref, p_ref, a_ref)

    @pl.when(kt == 0)
    def _():
        m_ref[...] = jnp.full(m_ref.shape, NEG, f32)
        acc_ref[...] = jnp.zeros(acc_ref.shape, f32)

    for g in range(G):
        q3 = q_ref[0, g * R:(g + 1) * R].reshape(R * TQ, DH)
        qa = jnp.concatenate([jnp.concatenate([sel_ref[0, g]] * R, axis=0), q3], axis=1)
        ka = jnp.concatenate([et_ref[0], ks_ref[0, g]], axis=1)
        _group_update(qa, ka, vs_ref[0, g], bs_ref, delta, g * R, m_ref, acc_ref, g * R, *scratch)

    @pl.when(kt >= qt - (WIN // TQ))
    def _():
        for g in range(G):
            q3 = q_ref[0, g * R:(g + 1) * R].reshape(R * TQ, DH)
            _group_update(q3, kw_ref[0, g], vw_ref[0, g], bw_ref, delta, g * R, m_ref, acc_ref, H + g * R, *scratch)

    @pl.when(kt == qt)
    def _():
        sig = jax.nn.sigmoid(gt_ref[...])
        for hh in range(H):
            a_s, a_w = acc_ref[hh], acc_ref[H + hh]
            o_s = a_s / jnp.maximum(pltpu.roll(a_s, DH, 1), 1e-30)
            o_w = a_w / jnp.maximum(pltpu.roll(a_w, DH, 1), 1e-30)
            o = sig[:, H + hh:H + hh + 1] * o_s + sig[:, 2 * H + hh:2 * H + hh + 1] * o_w
            o_ref[0, :, hh * DH:(hh + 1) * DH] = o[:, 0:DH]


def _flash_prompt(q2, ks, vs, kw, vw, bias_s, bias_w, selneg, et_tiles, gates, b, s):
    nq = s // TQ
    nw = WIN // TQ
    pairs = [(j, k) for j in range(nq) for k in range(j + 1)]
    qt_of = jnp.array([p[0] for p in pairs], jnp.int32)
    kt_of = jnp.array([p[1] for p in pairs], jnp.int32)
    const = lambda shape: pl.BlockSpec(shape, lambda i, p, qt, kt: (0,) * len(shape), pipeline_mode=pl.Buffered(1))
    kv = lambda w_, f: pl.BlockSpec((1, G, TQ, w_), lambda i, p, qt, kt: (i, 0, f(qt[p], kt[p]), 0))
    sel_kt = lambda j, k: k
    win_kt = lambda j, k: jnp.maximum(k, jnp.maximum(j - nw, 0))
    return pl.pallas_call(
        _flash_prompt_body, out_shape=SDS((b, s, ATT), f32),
        grid_spec=pltpu.PrefetchScalarGridSpec(
            num_scalar_prefetch=2, grid=(b, len(pairs)),
            in_specs=[pl.BlockSpec((1, H, TQ, DH), lambda i, p, qt, kt: (i, 0, qt[p], 0)),
                      kv(DH, sel_kt), kv(2 * DH, sel_kt), kv(DH, win_kt), kv(2 * DH, win_kt),
                      const(bias_s.shape), const(bias_w.shape),
                      pl.BlockSpec((1, G, TQ, 128), lambda i, p, qt, kt: (i, 0, qt[p], 0)),
                      pl.BlockSpec((1, TQ, 128), lambda i, p, qt, kt: (kt[p], 0, 0)),
                      pl.BlockSpec((TQ, 128), lambda i, p, qt, kt: (i * nq + qt[p], 0))],
            out_specs=pl.BlockSpec((1, TQ, ATT), lambda i, p, qt, kt: (i, qt[p], 0)),
            scratch_shapes=[pltpu.VMEM((2 * H, TQ, 128), f32), pltpu.VMEM((2 * H, TQ, 128), f32),
                            pltpu.VMEM((R * TQ, TQ), f32), pltpu.VMEM((R * TQ, TQ), bf16),
                            pltpu.VMEM((R * TQ, 128), f32)]),
        compiler_params=_cp(("parallel", "arbitrary"), 58), name="flash_prompt")(
            qt_of, kt_of, q2, ks, vs, kw, vw, bias_s, bias_w, selneg, et_tiles, gates)


def _cmp_sel_sample_body(q_ref, kc_ref, vc_ref, gt_ref, e_ref, o_ref, nm_ref, impt_ref, *, past, n_q):
    n_cmp = kc_ref.shape[2]
    n_sel = n_cmp // 4
    sig = jax.nn.sigmoid(gt_ref[0])
    rows = R * n_q
    mp = lax.broadcasted_iota(jnp.int32, (rows, n_cmp), 1)
    i_q = lax.broadcasted_iota(jnp.int32, (rows, n_cmp), 0) % n_q
    valid = (mp >= 1) & (16 * mp + 15 <= past + i_q)
    imps = []
    for g in range(G):
        s = jnp.where(valid, _nt(q_ref[0, g], kc_ref[0, g]), NEG)
        m = jnp.max(s, axis=-1, keepdims=True)
        e = jnp.where(valid, jnp.exp(s - m), 0.0)
        p = e / jnp.maximum(jnp.sum(e, axis=-1, keepdims=True), 1e-30)
        o = _dot(p.astype(bf16), vc_ref[0, g]) * sig[g * rows:(g + 1) * rows, 0:1]
        for r in range(R):
            hh = g * R + r
            o_ref[0, :, hh * DH:(hh + 1) * DH] = o[r * n_q:(r + 1) * n_q, :]
        imps.append(p[0:n_q] + p[n_q:2 * n_q] + p[2 * n_q:3 * n_q])
    imp = jnp.concatenate(imps + [jnp.zeros((128 - G * n_q, n_cmp), f32)], axis=0)
    impt_ref[...] = imp.T
    v = _block_scores(impt_ref, n_sel)
    q_pos = past + lax.broadcasted_iota(jnp.int32, v.shape, 1) % n_q
    val, _ = _force_scores(v, q_pos)
    rank = _rank_desc(val) + jnp.where(FORCE > val, 1, 0)
    neg = jnp.where(rank < TOPK, 0.0, NEG).T
    nm_ref[0] = _dot(neg[0:G * n_q, :].astype(bf16), e_ref[...])


def _cmp_sel_sample(qg, kc, vc, gsw, e_full, past):
    db, n_q = qg.shape[0], qg.shape[2] // R
    n_cmp = kc.shape[2]
    return pl.pallas_call(
        functools.partial(_cmp_sel_sample_body, past=past, n_q=n_q),
        out_shape=(SDS((db, n_q, ATT), f32), SDS((db, G * n_q, past), f32)), grid=(db,),
        in_specs=[pl.BlockSpec((1, G, R * n_q, DH), lambda i: (i, 0, 0, 0)),
                  pl.BlockSpec((1, G, n_cmp, DH), lambda i: (i, 0, 0, 0)),
                  pl.BlockSpec((1, G, n_cmp, DH), lambda i: (i, 0, 0, 0)),
                  pl.BlockSpec((1, H * n_q, 128), lambda i: (i, 0, 0)),
                  _resident((n_cmp // 4, past))],
        out_specs=(pl.BlockSpec((1, n_q, ATT), lambda i: (i, 0, 0)),
                   pl.BlockSpec((1, G * n_q, past), lambda i: (i, 0, 0))),
        scratch_shapes=[pltpu.VMEM((n_cmp, 128), f32)],
        compiler_params=_cp(("parallel",), 32), name="cmp_sel_sample")(qg, kc, vc, gsw, e_full)


def _update_parts(q, parts, bias, m_ref, l_ref, acc_ref, idx):
    s = jnp.concatenate([_dot(q, kt.astype(bf16)) for kt, _ in parts], axis=1) + bias
    m_old = m_ref[idx]
    m_new = jnp.maximum(m_old, jnp.max(s, axis=-1, keepdims=True))
    alpha = jnp.exp(m_old - m_new)
    p = jnp.exp(s - m_new)
    l_ref[idx] = alpha * l_ref[idx] + jnp.sum(p, axis=-1, keepdims=True)
    p = p.astype(bf16)
    pv = acc_ref[idx] * alpha
    off = 0
    for _, vt in parts:
        pv = pv + _nt(p[:, off:off + vt.shape[1]], vt.astype(bf16))
        off += vt.shape[1]
    acc_ref[idx] = pv
    m_ref[idx] = m_new


def _flash_sample_body(*refs, n_q):
    refs = refs[1:]
    pages = refs[:CHUNK_PAGES]
    (q_ref, bias_ref, nm_ref, new_ref, bnew_ref, win_ref, wnew_ref, bwin_ref, gt_ref, o_ref, nwin_ref,
     m_ref, l_ref, acc_ref) = refs[CHUNK_PAGES:]
    step = pl.program_id(1)
    q = q_ref[0]

    @pl.when(step == 0)
    def _():
        m_ref[...] = jnp.full(m_ref.shape, NEG, f32)
        l_ref[...] = jnp.zeros(l_ref.shape, f32)
        acc_ref[...] = jnp.zeros(acc_ref.shape, f32)

    nm = nm_ref[0]
    nm = jnp.concatenate([nm[g * n_q:(g + 1) * n_q] for g in range(G) for _ in range(R)], axis=0)
    parts = [(pages[p][0, 0:G].reshape(G * DH, PAGE), pages[p][0, G:2 * G].reshape(G * DH, PAGE))
             for p in range(CHUNK_PAGES)]
    _update_parts(q, parts, bias_ref[...] + nm, m_ref, l_ref, acc_ref, 0)

    @pl.when(step == pl.num_programs(1) - 1)
    def _():
        new = new_ref[0]
        _update_parts(q, [(new[0:256], new[256:512])], bnew_ref[...], m_ref, l_ref, acc_ref, 0)
        old, wnew = win_ref[0, 0].reshape(8 * DH, WIN), wnew_ref[0]
        _update_parts(q, [(old[0:256], old[256:512]), (wnew[0:256], wnew[256:512])], bwin_ref[...],
                      m_ref, l_ref, acc_ref, 1)
        head, tail = _shifted_state(old, wnew, n_q)
        nwin_ref[0, :, :, 0:WIN - PAGE] = head.reshape(8, DH, WIN - PAGE)
        nwin_ref[0, :, :, WIN - PAGE:WIN] = tail.reshape(8, DH, PAGE)
        sig = jax.nn.sigmoid(gt_ref[0])
        res = (sig[:, 1:2] * acc_ref[0] / jnp.maximum(l_ref[0], 1e-30)
               + sig[:, 2:3] * acc_ref[1] / jnp.maximum(l_ref[1], 1e-30))
        for hh in range(H):
            g = hh // R
            o_ref[0, :, hh * DH:(hh + 1) * DH] = res[hh * n_q:(hh + 1) * n_q, g * DH:(g + 1) * DH]


def _flash_sample(pool, page_table, qbd, bias_slc, negmask, new_slc, bias_new, win_states, li, win_new, bias_win,
                  gsw, n_q):
    db, n_pages = page_table.shape
    steps = n_pages // CHUNK_PAGES
    rows = H * n_q
    keys = CHUNK_PAGES * PAGE
    cm = lambda f: (lambda i, j, pt: f(i, j))
    const = lambda shape: pl.BlockSpec(shape, lambda i, j, pt: (0,) * len(shape), pipeline_mode=pl.Buffered(1))
    per_b = lambda shape: pl.BlockSpec((1,) + shape, lambda i, j, pt: (i,) + (0,) * len(shape))
    page_specs = [pl.BlockSpec((1, 8, DH, PAGE), lambda i, j, pt, k=k: (pt[i, j * CHUNK_PAGES + k], 1, 0, 0))
                  for k in range(CHUNK_PAGES)]
    return pl.pallas_call(
        functools.partial(_flash_sample_body, n_q=n_q),
        out_shape=(SDS((db, n_q, ATT), f32), SDS((db, 8, DH, WIN), f32)),
        grid_spec=pltpu.PrefetchScalarGridSpec(
            num_scalar_prefetch=1, grid=(db, steps),
            in_specs=page_specs + [
                per_b((rows, 256)),
                pl.BlockSpec((rows, keys), cm(lambda i, j: (0, j))),
                pl.BlockSpec((1, G * n_q, keys), cm(lambda i, j: (i, 0, j))),
                per_b((8 * DH, PAGE)), const((rows, PAGE)),
                pl.BlockSpec((1, 1, 8, DH, WIN), lambda i, j, pt: (li, i, 0, 0, 0)),
                per_b((8 * DH, PAGE)), const((rows, WIN + PAGE)),
                per_b((rows, 128))],
            out_specs=(per_b((n_q, ATT)), per_b((8, DH, WIN))),
            scratch_shapes=[pltpu.VMEM((2, rows, 1), f32), pltpu.VMEM((2, rows, 1), f32),
                            pltpu.VMEM((2, rows, 256), f32)]),
        compiler_params=_cp(("arbitrary", "arbitrary"), 48), name="flash_sample")(
            page_table, *([pool] * CHUNK_PAGES), qbd, bias_slc, negmask, new_slc, bias_new,
            win_states, win_new, bias_win, gsw)


def _dil_tile(q, kc, vc, kp, vp, bias_ref, gi, prev_idx):
    lane = lax.broadcasted_iota(jnp.int32, (DT, 128), 1)
    low = lane < DH
    if kp is None:
        kk, vv = kc.astype(bf16), vc.astype(bf16)
    else:
        kk = jnp.concatenate([kp, kc], axis=0).astype(bf16)
        vv = jnp.concatenate([vp, vc], axis=0).astype(bf16)
    outs, lses = [], []
    for e in range(2):
        qe = jnp.where(low if e == 0 else jnp.logical_not(low), q, 0.0).astype(bf16)
        bias = bias_ref[gi, 0, e, 0]
        if kp is not None:
            bias = jnp.concatenate([bias_ref[gi, 0, e, prev_idx], bias], axis=1)
        s = _nt(qe, kk) + bias
        m = jnp.max(s, axis=-1, keepdims=True)
        ex = jnp.exp(s - m)
        l = jnp.maximum(jnp.sum(ex, axis=-1, keepdims=True), 1e-30)
        outs.append(_dot(ex.astype(bf16), vv) / l)
        lses.append(m + jnp.log(l))
    return jnp.where(low, outs[0], outs[1]), jnp.where(low, lses[0], lses[1])


def _dil_combine(os_, ls_):
    mx = jnp.maximum(jnp.maximum(ls_[0], ls_[1]), ls_[2])
    ws = [jnp.exp(l - mx) for l in ls_]
    den = ws[0] + ws[1] + ws[2]
    return [o * w / den for o, w in zip(os_, ws)]


def _dil_prompt_body(q0, q1, q2, k0, k1, k2, v0, v1, v2, bias_ref, y_ref, osc, lsc, *, s):
    qs, ks, vs = (q0, q1, q2), (k0, k1, k2), (v0, v1, v2)
    for gi, (_, dil) in enumerate(DIL):
        n_t = s // dil // DT
        q_r, k_r, v_r = qs[gi], ks[gi], vs[gi]

        def tile(start, prev_start, prev_idx, gi=gi, dil=dil, q_r=q_r, k_r=k_r, v_r=v_r, n_t=n_t):
            rows = lambda st: pl.ds(st, DT, stride=dil) if dil > 1 else pl.ds(pl.multiple_of(st, DT), DT)
            kp = k_r[0, rows(prev_start), :] if n_t > 1 else None
            vp = v_r[0, rows(prev_start), :] if n_t > 1 else None
            o, lse = _dil_tile(q_r[0, rows(start), :], k_r[0, rows(start), :], v_r[0, rows(start), :],
                               kp, vp, bias_ref, gi, prev_idx)
            osc[gi, rows(start), :] = o
            lsc[gi, rows(start), :] = lse

        def body(it, carry, dil=dil, n_t=n_t, tile=tile):
            rho, t = it // n_t, it % n_t
            start = rho + t * (DT * dil)
            prev = rho + jnp.maximum(t - 1, 0) * (DT * dil)
            tile(start, prev, jnp.where(t == 0, 2, 1))
            return carry

        lax.fori_loop(0, dil * n_t, body, 0, unroll=DIL_UNROLL)
    outs = _dil_combine([osc[0], osc[1], osc[2]], [lsc[0], lsc[1], lsc[2]])
    for pair in range(2):
        @pl.when(pl.program_id(1) == pair)
        def _(pair=pair):
            for gi, o in enumerate(outs):
                c = 2 * gi + pair
                y_ref[0, :, c * 128:(c + 1) * 128] = o


def _dil_prompt(zd, bias_tiles, b, s):
    col = lambda base: [pl.BlockSpec((1, s, 128), lambda i, j, c=base + 2 * gi: (i, 0, c + j)) for gi in range(3)]
    return pl.pallas_call(
        functools.partial(_dil_prompt_body, s=s),
        out_shape=SDS((b, s, ATT), f32), grid=(b, 2),
        in_specs=col(0) + col(6) + col(12) + [pl.BlockSpec((3, 1, 2, 3, DT, DT), lambda i, j: (0, j, 0, 0, 0, 0))],
        out_specs=pl.BlockSpec((1, s, ATT), lambda i, j: (i, 0, 0)),
        scratch_shapes=[pltpu.VMEM((3, s, 128), f32), pltpu.VMEM((3, s, 128), f32)],
        compiler_params=_cp(("parallel", "arbitrary"), 48), name="dil_prompt")(*([zd] * 9), bias_tiles)


def _shifted_state(old, new, n_q):
    w = old.shape[1]
    rolled = pltpu.roll(old, w - n_q, 1)
    lane = lax.broadcasted_iota(jnp.int32, new.shape, 1)
    tail = jnp.where(lane >= PAGE - n_q, pltpu.roll(new, PAGE - n_q, 1), rolled[:, w - PAGE:])
    return rolled[:, :w - PAGE], tail


def _dil_sample_body(q_ref, s0, s1, s2, new_ref, b0, b1, b2, y_ref, n0, n1, n2, *, n_q):
    lane = lax.broadcasted_iota(jnp.int32, (n_q, 256), 1)
    os_, ls_ = [], []
    for gi, (st, bias, nst) in enumerate(((s0, b0, n0), (s1, b1, n1), (s2, b2, n2))):
        w = st.shape[4]
        old = st[0, 0].reshape(8 * DH, w)
        new = new_ref[0, gi]
        head, tail = _shifted_state(old, new, n_q)
        if w > PAGE:
            nst[0, :, :, 0:w - PAGE] = head.reshape(8, DH, w - PAGE)
        nst[0, :, :, w - PAGE:w] = tail.reshape(8, DH, PAGE)
        q = q_ref[0, gi]
        s = jnp.concatenate([_dot(q, old[0:256].astype(bf16)), _dot(q, new[0:256].astype(bf16))], axis=1) + bias[...]
        m = jnp.max(s, axis=-1, keepdims=True)
        e = jnp.exp(s - m)
        l = jnp.maximum(jnp.sum(e, axis=-1, keepdims=True), 1e-30)
        eb = e.astype(bf16)
        res = (_nt(eb[:, 0:w], old[256:512].astype(bf16)) + _nt(eb[:, w:], new[256:512].astype(bf16))) / l
        lse = m + jnp.log(l)
        o = jnp.zeros((n_q, 256), f32)
        lm = jnp.zeros((n_q, 256), f32)
        for hg in range(4):
            inb = (lane >= hg * DH) & (lane < (hg + 1) * DH)
            o = jnp.where(inb, res[hg * n_q:(hg + 1) * n_q, :], o)
            lm = jnp.where(inb, lse[hg * n_q:(hg + 1) * n_q, :], lm)
        os_.append(o)
        ls_.append(lm)
    for gi, o in enumerate(_dil_combine(os_, ls_)):
        y_ref[0, :, gi * 256:(gi + 1) * 256] = o


def _dil_sample(qbd, states, li, new_t, biases, n_q):
    db = new_t.shape[0]
    per_b = lambda shape: pl.BlockSpec((1,) + shape, lambda i: (i,) + (0,) * len(shape))
    wins = [st.shape[4] for st in states]
    return pl.pallas_call(
        functools.partial(_dil_sample_body, n_q=n_q),
        out_shape=(SDS((db, n_q, ATT), f32),) + tuple(SDS((db, 8, DH, w), f32) for w in wins), grid=(db,),
        in_specs=[per_b((3, 4 * n_q, 256))]
                 + [pl.BlockSpec((1, 1, 8, DH, w), lambda i: (li, i, 0, 0, 0)) for w in wins]
                 + [per_b((3, 8 * DH, PAGE))] + [_resident(bb.shape) for bb in biases],
        out_specs=(per_b((n_q, ATT)),) + tuple(per_b((8, DH, w)) for w in wins),
        compiler_params=_cp(("parallel",), 48), name="dil_sample")(qbd, *states, new_t, *biases)


def _rel_bucket(dist):
    dist = jnp.maximum(dist, 0)
    d32 = jnp.maximum(dist, 1).astype(f32)
    large = 16 + (jnp.log(d32 / 16) / math.log(2048 / 16) * 16).astype(jnp.int32)
    return jnp.where(dist < 16, dist, jnp.minimum(large, 31))


def _bias_rev(tbl, n, keep, scale=1):
    dist = (n - 1 - jnp.arange(n)) * scale
    return jnp.where(keep(dist), tbl[_rel_bucket(dist)].T, NEG)


def _toeplitz_tiles(bdr, n_delta, t):
    hn = bdr.shape[0]
    bdp = jnp.concatenate([bdr, jnp.full((hn, t - 1), NEG, f32)], axis=1)
    lp = bdp.shape[1]
    gq = jnp.stack([bdp[:, lp + 1 - d * t - 2 * t:lp - d * t] for d in range(n_delta)])
    gq = jnp.pad(gq, ((0, 0), (0, 0), (0, 1)))
    x = jnp.broadcast_to(gq[:, :, None, :], (n_delta, hn, t, 2 * t)).reshape(n_delta, hn, t * 2 * t)
    y = x[..., :t * (2 * t - 1)].reshape(n_delta, hn, t, 2 * t - 1)
    return y[..., t - 1:]


def _rows_by_query(bdr, n_q, n_keys, first_dist):
    last = bdr.shape[1] - 1
    return jnp.stack([bdr[:, last - first_dist - i:last - first_dist - i + n_keys] for i in range(n_q)], axis=1)


def _new_token_bias(tbl, n_q, keep):
    tri = jnp.arange(n_q)[:, None] - jnp.arange(PAGE)[None, :]
    ok = (tri >= 0) & keep(tri)
    return jnp.where(ok[None], jnp.transpose(tbl[_rel_bucket(tri)], (2, 0, 1)), NEG)


def _nsa_tables(rel_bias, s, n_q, past):
    everywhere = lambda d: d >= 0
    in_window = lambda d: d < WIN
    n = past + n_q
    key_blk = jnp.arange(s // TQ)[:, None, None] * (TQ // SEL_BLOCK) + jnp.arange(TQ)[None, :, None] // SEL_BLOCK
    bias_new = _new_token_bias(rel_bias, n_q, everywhere).reshape(H * n_q, PAGE)
    bias_win = _rows_by_query(_bias_rev(rel_bias, n, in_window), n_q, WIN, WIN).reshape(H * n_q, WIN)
    return dict(
        perm=_chunk_perm(),
        bias_s=_toeplitz_tiles(_bias_rev(rel_bias, s, everywhere), s // TQ, TQ) * LOG2E,
        bias_w=_toeplitz_tiles(_bias_rev(rel_bias, WIN + TQ, in_window), WIN // TQ + 1, TQ) * LOG2E,
        et_tiles=(key_blk == jnp.arange(128)[None, None, :]).astype(bf16),
        e_full=(jnp.arange(past // SEL_BLOCK)[:, None] == jnp.arange(past)[None, :] // SEL_BLOCK).astype(bf16),
        bias_slc=_rows_by_query(_bias_rev(rel_bias, n, everywhere), n_q, past, past).reshape(H * n_q, past),
        bias_new=bias_new,
        bias_win=jnp.concatenate([bias_win, bias_new], axis=1))


def _nsa_layer(yp, ys, g_pre, g_post, w_in, w_out, ffn, cmp_pos, cmp_w1, cmp_w2, tb, pool, page_table, win_states,
               li, b, s, db, n_q, past):
    w_pad = jnp.pad(w_in, ((0, 0), (0, NSA_PAD - w_in.shape[1]))).astype(bf16)
    w_out = w_out.astype(bf16)
    w1cat = jnp.concatenate([cmp_w1[:, :1024], cmp_w1[:, 1024:]], axis=-1).astype(bf16)
    pos2 = cmp_pos.reshape(2, 2, 1024)
    w2 = cmp_w2.astype(bf16)

    q_hm, q2_hm, kvt, kwin, ks, vs, kw, vw, gates = _nsa_proj_prompt(yp, g_pre, w_pad, b, s)
    kc, vc = _compress_prompt(kvt, tb["perm"], w1cat, pos2, w2, b, s)
    o_cmp, selneg = _cmp_sel_prompt(q_hm, kc, vc, gates, b, s)
    o_sw = _flash_prompt(q2_hm, ks, vs, kw, vw, tb["bias_s"], tb["bias_w"], selneg, tb["et_tiles"], gates, b, s)
    yp = _mix_ffn(yp, g_post, w_out, ffn, o_cmp.reshape(b * s, ATT), o_sw.reshape(b * s, ATT))
    kv_p = jnp.transpose(kvt.reshape(b, 4, G, DH, s), (0, 4, 1, 2, 3))
    win_p = kwin.reshape(b, s, 2, G, DH)[:, s - min(WIN, s):]

    z = _proj(ys, g_pre, w_pad)
    q = (z[:, :ATT] * (DH ** -0.5)).reshape(db, n_q, H, DH).transpose(0, 2, 1, 3)
    qg = q.reshape(db, G, R * n_q, DH).astype(bf16)
    eye = jnp.eye(G, dtype=f32)
    qbd = (q.reshape(db, G, R * n_q, 1, DH) * eye[None, :, None, :, None]).reshape(db, H * n_q, G * DH).astype(bf16)
    kv_new = z[:, ATT:ATT + 1024].reshape(db, n_q, 1024)
    win_new = z[:, ATT + 1024:ATT + 1536].reshape(db, n_q, 512)
    gt = z[:, ATT + 1536:ATT + 1536 + 3 * H].reshape(db, n_q, 3, H).transpose(0, 3, 1, 2)
    gsw = jnp.pad(gt.reshape(db, H * n_q, 3), ((0, 0), (0, 0), (0, 125)))
    kc_s, vc_s = _compress_sample(pool, page_table, tb["perm"], w1cat, pos2, w2)
    o_cmp_s, negmask = _cmp_sel_sample(qg, kc_s, vc_s, gsw, tb["e_full"], past)
    page_t = lambda x: jnp.pad(jnp.transpose(x, (0, 2, 1)), ((0, 0), (0, 0), (0, PAGE - n_q)))
    win_view = jnp.transpose(win_states, (0, 1, 3, 4, 5, 2)).reshape(win_states.shape[0], db, 8, DH, WIN)
    o_sw_s, win_t = _flash_sample(pool, page_table, qbd, tb["bias_slc"], negmask, page_t(kv_new[:, :, 512:]),
                                  tb["bias_new"], win_view, li, page_t(win_new), tb["bias_win"], gsw, n_q)
    ys = _mix_ffn(ys, g_post, w_out, ffn, o_cmp_s.reshape(db * n_q, ATT), o_sw_s.reshape(db * n_q, ATT))
    kv_s = kv_new.reshape(db, n_q, 4, G, DH)
    win_s = jnp.transpose(win_t.reshape(db, 2, G, DH, WIN), (0, 4, 1, 2, 3))
    return yp, ys, kv_p, win_p, kv_s, win_s


def _dil_tables(rel_bias, n_q):
    tiles, biases = [], []
    for gi, (win, dil) in enumerate(DIL):
        tbl = rel_bias[:, gi * 4:(gi + 1) * 4]
        on_grid = lambda d, win=win, dil=dil: (d % dil == 0) & (d <= win)
        tl = _toeplitz_tiles(_bias_rev(tbl, 2 * DT, on_grid, scale=dil), 2, DT)
        tiles.append(jnp.concatenate([tl, jnp.full_like(tl[:1], NEG)]))
        b_all = jnp.concatenate([_rows_by_query(_bias_rev(tbl, win + n_q, on_grid), n_q, win, win),
                                 _new_token_bias(tbl, n_q, on_grid)], axis=2)
        biases.append(b_all.reshape(4 * n_q, win + PAGE))
    return dict(tiles=jnp.stack(tiles).transpose(0, 2, 1, 3, 4).reshape(3, 2, 2, 3, DT, DT), sample=biases)


def _dil_layer(yp, ys, g_pre, g_post, w_in, w_out, ffn, tb, states, li, b, s, db, n_q):
    w_in = w_in.astype(bf16)
    w_out = w_out.astype(bf16)

    zd, kvt = _dil_proj(yp, g_pre, w_in, b, s)
    y = _dil_prompt(zd.reshape(b, s, 3 * ATT), tb["tiles"], b, s)
    yp = _mix_ffn(yp, g_post, w_out, ffn, y.reshape(b * s, ATT))
    kvt = kvt.reshape(b, 2, H, DH, s)
    st_p = []
    for gi, (win, _) in enumerate(DIL):
        keep = min(win, s)
        st_p.append(jnp.transpose(kvt[:, :, gi * 4:(gi + 1) * 4, :, s - keep:], (0, 4, 1, 2, 3)))

    z = _proj(ys, g_pre, w_in).reshape(db, n_q, 3, 3, 4, DH)
    q = (z[:, :, 0] * (DH ** -0.5)).transpose(0, 2, 3, 1, 4)
    eye = jnp.eye(4, dtype=f32)
    qbd = (q[:, :, :, :, None, :] * eye[None, None, :, None, :, None]).reshape(db, 3, 4 * n_q, 256).astype(bf16)
    new_t = jnp.transpose(z[:, :, 1:3], (0, 3, 2, 4, 5, 1)).reshape(db, 3, 8 * DH, n_q)
    new_t = jnp.pad(new_t, ((0, 0), (0, 0), (0, 0), (0, PAGE - n_q)))
    views = [jnp.transpose(st, (0, 1, 3, 4, 5, 2)).reshape(st.shape[0], db, 8, DH, st.shape[2]) for st in states]
    y_s, n0, n1, n2 = _dil_sample(qbd, views, li, new_t, tb["sample"], n_q)
    ys = _mix_ffn(ys, g_post, w_out, ffn, y_s.reshape(db * n_q, ATT))
    st_s = [jnp.transpose(n.reshape(db, 2, 4, DH, n.shape[3]), (0, 4, 1, 2, 3)) for n in (n0, n1, n2)]
    return yp, ys, st_p, st_s


def kernel(x_prompt, x_sample, cache_nsa_kv, page_table, state_nsa_win, state_dil_w128, state_dil_w512,
           state_dil_w2048, norm_g, ffn_w_in, ffn_w_out, nsa_w_in, nsa_w_out, nsa_cmp_pos, nsa_cmp_w1,
           nsa_cmp_w2, dil_w_in, dil_w_out, rel_bias):
    b, s, _ = x_prompt.shape
    db, n_q, _ = x_sample.shape
    depth = norm_g.shape[0]
    past = page_table.shape[1] * PAGE
    yp = x_prompt.reshape(b * s, D)
    ys = x_sample.reshape(db * n_q, D)
    nsa_kv_p, nsa_win_p, nsa_kv_s, nsa_win_s = [], [], [], []
    dil_p, dil_s = [[], [], []], [[], [], []]
    nsa_tb = _nsa_tables(rel_bias, s, n_q, past)
    dil_tb = _dil_tables(rel_bias, n_q)
    for i in range(depth):
        li = i // 2
        g = norm_g[i]
        w_i, w_o = ffn_w_in[i, 0].astype(bf16), ffn_w_out[i, 0].astype(bf16)
        yp = _half_ffn(yp, g[0:2], w_i, w_o)
        ys = _half_ffn(ys, g[0:2], w_i, w_o)
        ffn_b = (g[4:6], ffn_w_in[i, 1].astype(bf16), ffn_w_out[i, 1].astype(bf16))
        if i % 2 == 0:
            n_pool = cache_nsa_kv.shape[1]
            pool = jnp.transpose(cache_nsa_kv, (0, 1, 3, 4, 5, 2)).reshape(
                cache_nsa_kv.shape[0] * n_pool, 4 * G, DH, PAGE)
            yp, ys, kvp, winp, kvs, wins = _nsa_layer(
                yp, ys, g[2:3], g[3:4], nsa_w_in[li], nsa_w_out[li], ffn_b, nsa_cmp_pos[li], nsa_cmp_w1[li],
                nsa_cmp_w2[li], nsa_tb, pool, page_table + li * n_pool, state_nsa_win, li, b, s, db, n_q, past)
            nsa_kv_p.append(kvp)
            nsa_win_p.append(winp)
            nsa_kv_s.append(kvs)
            nsa_win_s.append(wins)
        else:
            states = (state_dil_w128, state_dil_w512, state_dil_w2048)
            yp, ys, stp, sts = _dil_layer(yp, ys, g[2:3], g[3:4], dil_w_in[li], dil_w_out[li], ffn_b, dil_tb, states,
                                          li, b, s, db, n_q)
            for gi in range(3):
                dil_p[gi].append(stp[gi])
                dil_s[gi].append(sts[gi])
    return (yp.reshape(b, s, D), ys.reshape(db, n_q, D),
            jnp.stack(nsa_kv_p), jnp.stack(nsa_win_p),
            jnp.stack(dil_p[0]), jnp.stack(dil_p[1]), jnp.stack(dil_p[2]),
            jnp.stack(nsa_kv_s), jnp.stack(nsa_win_s),
            jnp.stack(dil_s[0]), jnp.stack(dil_s[1]), jnp.stack(dil_s[2]))
```
